```python
import math
import jax, jax.numpy as jnp
from jax import lax
import numpy as np

D_MODEL = 1024
BATCH = 8
SEQ = 4096
DEPTH = 2

S5_WIDTH = D_MODEL // 2
S5_GROUP = 16
S5_GROUPS = S5_WIDTH // S5_GROUP
S5_STATE = 64
DT_MIN = 0.001
DT_MAX = 0.1
HEAD_DIM = 64
N_Q_HEADS = 8
N_KV_HEADS = 2
Q_PER_KV = N_Q_HEADS // N_KV_HEADS
ATT_WIDTH = N_Q_HEADS * HEAD_DIM
KV_WIDTH = N_KV_HEADS * HEAD_DIM
WINDOW = 128
BLOCK = 128
NUM_BUCKETS = 32
MAX_DISTANCE = 128
D_FF = ((8 * D_MODEL // 3 + 255) // 256) * 256
RMS_EPS = 1e-6
NEG_INF = -1e30
IN_WIDTH = S5_WIDTH + ATT_WIDTH + 2 * KV_WIDTH + 2 * D_MODEL

kernel_name = "hybrid_s5_swa_gated_encoder"


def rmsnorm(x, g):
    xf = x.astype(jnp.float32)
    y = xf * lax.rsqrt(jnp.mean(xf * xf, axis=-1, keepdims=True) + RMS_EPS) * g.astype(jnp.float32)
    return y.astype(x.dtype)


def t5_bucket(rel):
    half = NUM_BUCKETS // 2
    max_exact = half // 2
    ret = jnp.where(rel > 0, half, 0)
    n = jnp.abs(rel)
    nf = jnp.maximum(n, 1).astype(jnp.float32)
    large = max_exact + (jnp.log(nf / max_exact) / math.log(MAX_DISTANCE / max_exact)
                         * (half - max_exact)).astype(jnp.int32)
    large = jnp.minimum(large, half - 1)
    return ret + jnp.where(n < max_exact, n, large)


def band_geometry(seq):
    n_blocks = seq // BLOCK
    q_loc = jnp.arange(BLOCK, dtype=jnp.int32)
    k_loc = jnp.arange(3 * BLOCK, dtype=jnp.int32)
    rel = (k_loc[None, :] - BLOCK) - q_loc[:, None]
    in_window = jnp.abs(rel) <= WINDOW
    k_glob = jnp.arange(n_blocks, dtype=jnp.int32)[:, None] * BLOCK - BLOCK + k_loc[None, :]
    k_valid = (k_glob >= 0) & (k_glob < seq)
    mask = in_window[None, :, :] & k_valid[:, None, :]
    return rel, mask


def _complex_affine_combine(e1, e2):
    a1r, a1i, b1r, b1i = e1
    a2r, a2i, b2r, b2i = e2
    ar = a2r * a1r - a2i * a1i
    ai = a2r * a1i + a2i * a1r
    br = a2r * b1r - a2i * b1i + b2r
    bi = a2r * b1i + a2i * b1r + b2i
    return (ar, ai, br, bi)


def s5_branch(u, lam_re, lam_im, log_dt, b_re, b_im, c_re, c_im, d, w_glu):
    bsz, seq, _ = u.shape
    uf = u.astype(jnp.float32)
    ug = uf.reshape(bsz, seq, S5_GROUPS, S5_GROUP)
    y = d.astype(jnp.float32) * uf
    for direction, reverse in enumerate((False, True)):
        lr = lam_re[direction].astype(jnp.float32)
        li = lam_im[direction].astype(jnp.float32)
        dt = jnp.exp(log_dt[direction].astype(jnp.float32))[:, None]
        mag = jnp.exp(lr * dt)
        ab_re = mag * jnp.cos(li * dt)
        ab_im = mag * jnp.sin(li * dt)
        nr = ab_re - 1.0
        den = lr * lr + li * li
        coef_re = (nr * lr + ab_im * li) / den
        coef_im = (ab_im * lr - nr * li) / den
        br = b_re[direction].astype(jnp.float32)
        bi = b_im[direction].astype(jnp.float32)
        bb_re = coef_re[..., None] * br - coef_im[..., None] * bi
        bb_im = coef_re[..., None] * bi + coef_im[..., None] * br
        bu_re = jnp.einsum('blgh,gph->blgp', ug, bb_re)
        bu_im = jnp.einsum('blgh,gph->blgp', ug, bb_im)
        a_re = jnp.broadcast_to(ab_re, bu_re.shape)
        a_im = jnp.broadcast_to(ab_im, bu_im.shape)
        _, _, s_re, s_im = lax.associative_scan(
            _complex_affine_combine, (a_re, a_im, bu_re, bu_im), axis=1, reverse=reverse)
        y_dir = (jnp.einsum('blgp,ghp->blgh', s_re, c_re[direction].astype(jnp.float32))
                 - jnp.einsum('blgp,ghp->blgh', s_im, c_im[direction].astype(jnp.float32)))
        y = y + y_dir.reshape(bsz, seq, S5_WIDTH)
    z = jax.nn.gelu(y)
    out = z * jax.nn.sigmoid(z @ w_glu.astype(jnp.float32))
    return out.astype(u.dtype)


def windowed_gqa(q, k, v, sink, bias, mask):
    bsz, seq, _ = q.shape
    nb = seq // BLOCK
    qb = q.reshape(bsz, nb, BLOCK, N_KV_HEADS, Q_PER_KV, HEAD_DIM)

    def blocks(t):
        t = t.reshape(bsz, seq, N_KV_HEADS, HEAD_DIM)
        t = jnp.pad(t, ((0, 0), (BLOCK, BLOCK), (0, 0), (0, 0)))
        t = t.reshape(bsz, nb + 2, BLOCK, N_KV_HEADS, HEAD_DIM)
        return jnp.concatenate([t[:, :-2], t[:, 1:-1], t[:, 2:]], axis=2)

    kb = blocks(k)
    vb = blocks(v)
    scale = HEAD_DIM ** -0.5
    logits = jnp.einsum('bnqkgd,bnskd->bnkgqs', qb, kb).astype(jnp.float32) * scale
    logits = logits + bias.reshape(N_KV_HEADS, Q_PER_KV, BLOCK, 3 * BLOCK).astype(jnp.float32)
    logits = jnp.where(mask[None, :, None, None, :, :], logits, NEG_INF)
    sink_col = jnp.broadcast_to(sink.astype(jnp.float32).reshape(N_KV_HEADS, Q_PER_KV, 1, 1),
                                logits.shape[:-1] + (1,))
    probs = jax.nn.softmax(jnp.concatenate([logits, sink_col], axis=-1), axis=-1)[..., :-1]
    out = jnp.einsum('bnkgqs,bnskd->bnqkgd', probs.astype(v.dtype), vb)
    return out.reshape(bsz, seq, ATT_WIDTH)


def setup_inputs(seed: int = 0) -> dict:
    key = jax.random.key(seed)
    ks = jax.random.split(key, 24)
    f32 = jnp.float32
    nrm = lambda k, shape, s: jax.random.normal(k, shape, f32) * s
    G, P, H = S5_GROUPS, S5_STATE, S5_GROUP
    lam_im = jnp.broadcast_to(math.pi * jnp.arange(P, dtype=f32), (DEPTH, 2, G, P))
    return {
        "x": jax.random.normal(ks[0], (BATCH, SEQ, D_MODEL), f32),
        "norm1_g": 1.0 + nrm(ks[1], (DEPTH, D_MODEL), 0.02),
        "norm2_g": 1.0 + nrm(ks[2], (DEPTH, D_MODEL), 0.02),
        "final_g": 1.0 + nrm(ks[3], (D_MODEL,), 0.02),
        "w_in": nrm(ks[4], (DEPTH, D_MODEL, IN_WIDTH), D_MODEL ** -0.5),
        "s5_lambda_re": -0.5 + nrm(ks[5], (DEPTH, 2, G, P), 0.01),
        "s5_lambda_im": lam_im + nrm(ks[6], (DEPTH, 2, G, P), 0.01),
        "s5_log_dt": jax.random.uniform(ks[7], (DEPTH, 2, G), f32, math.log(DT_MIN), math.log(DT_MAX)),
        "s5_b_re": nrm(ks[8], (DEPTH, 2, G, P, H), (2 * H) ** -0.5),
        "s5_b_im": nrm(ks[9], (DEPTH, 2, G, P, H), (2 * H) ** -0.5),
        "s5_c_re": nrm(ks[10], (DEPTH, 2, G, H, P), P ** -0.5),
        "s5_c_im": nrm(ks[11], (DEPTH, 2, G, H, P), P ** -0.5),
        "s5_d": nrm(ks[12], (DEPTH, S5_WIDTH), 1.0),
        "s5_w_glu": nrm(ks[13], (DEPTH, S5_WIDTH, S5_WIDTH), S5_WIDTH ** -0.5),
        "attn_sink": nrm(ks[14], (DEPTH, N_Q_HEADS), 0.1),
        "rel_bias": nrm(ks[15], (NUM_BUCKETS, N_Q_HEADS), 0.1),
        "w_branch_a": nrm(ks[16], (DEPTH, S5_WIDTH, D_MODEL), S5_WIDTH ** -0.5),
        "w_branch_b": nrm(ks[17], (DEPTH, ATT_WIDTH, D_MODEL), ATT_WIDTH ** -0.5),
        "w_out": nrm(ks[18], (DEPTH, D_MODEL, D_MODEL), D_MODEL ** -0.5),
        "ffn_w_gate": nrm(ks[19], (DEPTH, D_MODEL, D_FF), D_MODEL ** -0.5),
        "ffn_w_up": nrm(ks[20], (DEPTH, D_MODEL, D_FF), D_MODEL ** -0.5),
        "ffn_w_down": nrm(ks[21], (DEPTH, D_FF, D_MODEL), D_FF ** -0.5),
    }


def reference(x, norm1_g, norm2_g, final_g, w_in, s5_lambda_re, s5_lambda_im, s5_log_dt,
              s5_b_re, s5_b_im, s5_c_re, s5_c_im, s5_d, s5_w_glu, attn_sink, rel_bias,
              w_branch_a, w_branch_b, w_out, ffn_w_gate, ffn_w_up, ffn_w_down):
    seq = x.shape[1]
    rel, mask = band_geometry(seq)
    bias = jnp.transpose(rel_bias[t5_bucket(rel)], (2, 0, 1))
    o_q = S5_WIDTH
    o_k = o_q + ATT_WIDTH
    o_v = o_k + KV_WIDTH
    o_ga = o_v + KV_WIDTH
    o_gb = o_ga + D_MODEL
    h = x
    for layer in range(DEPTH):
        hn = rmsnorm(h, norm1_g[layer])
        proj = hn @ w_in[layer]
        u = proj[..., :o_q]
        q = proj[..., o_q:o_k]
        k = proj[..., o_k:o_v]
        v = proj[..., o_v:o_ga]
        gate_a = jax.nn.sigmoid(proj[..., o_ga:o_gb].astype(jnp.float32)).astype(h.dtype)
        gate_b = jax.nn.sigmoid(proj[..., o_gb:].astype(jnp.float32)).astype(h.dtype)
        y_a = s5_branch(u, s5_lambda_re[layer], s5_lambda_im[layer], s5_log_dt[layer],
                        s5_b_re[layer], s5_b_im[layer], s5_c_re[layer], s5_c_im[layer],
                        s5_d[layer], s5_w_glu[layer])
        y_b = windowed_gqa(q, k, v, attn_sink[layer], bias, mask)
        merged = gate_a * (y_a @ w_branch_a[layer]) + gate_b * (y_b @ w_branch_b[layer])
        h = h + merged @ w_out[layer]
        hn2 = rmsnorm(h, norm2_g[layer])
        ff = jax.nn.silu(hn2 @ ffn_w_gate[layer]) * (hn2 @ ffn_w_up[layer])
        h = h + ff @ ffn_w_down[layer]
    return rmsnorm(h, final_g)
```

```python
import functools
import math

import jax
import jax.numpy as jnp
from jax import lax
from jax.experimental import pallas as pl
from jax.experimental.pallas import tpu as pltpu

F32 = jnp.float32
BF16 = jnp.bfloat16

D_MODEL = 1024
S5_WIDTH = 512
S5_GROUP = 16
S5_GROUPS = 32
S5_STATE = 64
HEAD_DIM = 64
N_Q_HEADS = 8
N_KV_HEADS = 2
Q_PER_KV = N_Q_HEADS // N_KV_HEADS
ATT_WIDTH = N_Q_HEADS * HEAD_DIM
KV_WIDTH = N_KV_HEADS * HEAD_DIM
WINDOW = 128
BLOCK = 128
NUM_BUCKETS = 32
MAX_DISTANCE = 128
D_FF = 2816
RMS_EPS = 1e-6
NEG_INF = -1e30

LANES = 128
SUBLANES = 8
CHUNK = 16
GROUPS_PER_SLAB = LANES // S5_GROUP
N_SLABS = S5_WIDTH // LANES
SLAB_K = CHUNK * LANES
STATE_COLS = 2 * 2 * GROUPS_PER_SLAB * S5_STATE
SCAN_TILE = STATE_COLS // 2
S5_TOKENS = 512
FF_TILE = 256
VMEM_LIMIT = 56 * 1024 * 1024


def _rms(x, g):
    return x * lax.rsqrt(jnp.mean(x * x, axis=-1, keepdims=True) + RMS_EPS) * g


def _gelu_tanh(x):
    return 0.5 * x * (1.0 + jnp.tanh(math.sqrt(2.0 / math.pi) * (x + 0.044715 * (x * x * x))))


def _sigmoid(x):
    return 1.0 / (1.0 + jnp.exp(-x))


def _dot(a, b):
    return jnp.dot(a, b, preferred_element_type=F32)


def _const_spec(shape):
    nd = len(shape)
    return pl.BlockSpec(shape, lambda *_: (0,) * nd, pipeline_mode=pl.Buffered(1))


def _params(sem):
    return pltpu.CompilerParams(dimension_semantics=sem, vmem_limit_bytes=VMEM_LIMIT)


def _inproj_kernel(x_ref, g_ref, w_ref, u_ref, q_ref, kv_ref):
    hn = _rms(x_ref[...], g_ref[...]).astype(BF16)
    r = _dot(hn, w_ref[...])
    u_ref[...] = r[:, :S5_WIDTH]
    q_ref[...] = r[:, S5_WIDTH:S5_WIDTH + ATT_WIDTH].astype(BF16)
    kv_ref[...] = r[:, S5_WIDTH + ATT_WIDTH:].astype(BF16)


def _inproj(h, g, w, tm=512):
    t = h.shape[0]
    n = w.shape[1]
    return pl.pallas_call(
        _inproj_kernel,
        grid=(t // tm,),
        in_specs=[pl.BlockSpec((tm, D_MODEL), lambda i: (i, 0)),
                  _const_spec((1, D_MODEL)), _const_spec((D_MODEL, n))],
        out_specs=[pl.BlockSpec((tm, S5_WIDTH), lambda i: (i, 0)),
                   pl.BlockSpec((tm, ATT_WIDTH), lambda i: (i, 0)),
                   pl.BlockSpec((tm, 2 * KV_WIDTH), lambda i: (i, 0))],
        out_shape=[jax.ShapeDtypeStruct((t, S5_WIDTH), F32),
                   jax.ShapeDtypeStruct((t, ATT_WIDTH), BF16),
                   jax.ShapeDtypeStruct((t, 2 * KV_WIDTH), BF16)],
        compiler_params=_params(("parallel",)),
        name="inproj",
    )(h, g, w)


def _chunk_rows(u_ref, m):
    rows = [jnp.concatenate([u_ref[:, c * CHUNK + t, :] for t in range(CHUNK)], axis=1) for c in range(m)]
    return jnp.concatenate(rows, axis=0).astype(BF16)


def _s5_in_kernel(u_ref, w_ref, o_ref):
    m = u_ref.shape[1] // CHUNK
    o_ref[...] = _dot(_chunk_rows(u_ref, m), w_ref[...])


def _s5_state_in(u3, w_in_state, tl=S5_TOKENS):
    bsz, seq, _ = u3.shape
    m = tl // CHUNK
    return pl.pallas_call(
        _s5_in_kernel,
        grid=(N_SLABS, seq // tl),
        in_specs=[pl.BlockSpec((bsz, tl, LANES), lambda s, i: (0, i, s)),
                  pl.BlockSpec((None, SLAB_K, STATE_COLS), lambda s, i: (s, 0, 0))],
        out_specs=pl.BlockSpec((m * bsz, STATE_COLS), lambda s, i: (i, s)),
        out_shape=jax.ShapeDtypeStruct((seq // CHUNK * bsz, N_SLABS * STATE_COLS), F32),
        compiler_params=_params(("parallel", "parallel")),
        name="s5_state_in",
    )(u3, w_in_state)


def _s5_scan_kernel(sin_ref, a_ref, o_ref, st_ref, *, cpt):
    half = SCAN_TILE // 2
    backward = pl.program_id(1) == 1

    @pl.when(pl.program_id(2) == 0)
    def _():
        st_ref[...] = jnp.zeros_like(st_ref)

    ar = a_ref[:, :half]
    ai = a_ref[:, half:]

    def body(i, carry):
        sr, si = carry
        c = jnp.where(backward, cpt - 1 - i, i)
        rows = pl.ds(pl.multiple_of(c * SUBLANES, SUBLANES), SUBLANES)
        o_ref[rows, :half] = sr
        o_ref[rows, half:] = si
        xr = sin_ref[rows, :half]
        xi = sin_ref[rows, half:]
        return ar * sr - ai * si + xr, ar * si + ai * sr + xi

    sr, si = lax.fori_loop(0, cpt, body, (st_ref[:, :half], st_ref[:, half:]), unroll=4)
    st_ref[:, :half] = sr
    st_ref[:, half:] = si


def _s5_scan(sin, a16, n_chunks, cpt=64):
    nct = n_chunks // cpt
    blk = lambda s, d, k: (k + d * (nct - 1 - 2 * k), s * 2 + d)
    return pl.pallas_call(
        functools.partial(_s5_scan_kernel, cpt=cpt),
        grid=(N_SLABS, 2, nct),
        in_specs=[pl.BlockSpec((cpt * SUBLANES, SCAN_TILE), blk),
                  pl.BlockSpec((SUBLANES, SCAN_TILE), lambda s, d, k: (0, s * 2 + d))],
        out_specs=pl.BlockSpec((cpt * SUBLANES, SCAN_TILE), blk),
        out_shape=jax.ShapeDtypeStruct(sin.shape, F32),
        scratch_shapes=[pltpu.VMEM((SUBLANES, SCAN_TILE), F32)],
        compiler_params=_params(("parallel", "parallel", "arbitrary")),
        name="s5_scan",
    )(sin, a16)


def _s5_out_kernel(u_ref, c_ref, wt_ref, wo_ref, y_ref):
    bsz = u_ref.shape[0]
    m = u_ref.shape[1] // CHUNK
    r = _dot(_chunk_rows(u_ref, m), wt_ref[...]) + _dot(c_ref[...].astype(BF16), wo_ref[...])
    for c in range(m):
        for t in range(CHUNK):
            y_ref[:, c * CHUNK + t, :] = r[c * bsz:(c + 1) * bsz, t * LANES:(t + 1) * LANES]


def _s5_out(u3, carry, w_toep, w_out_state, tl=S5_TOKENS):
    bsz, seq, _ = u3.shape
    m = tl // CHUNK
    return pl.pallas_call(
        _s5_out_kernel,
        grid=(N_SLABS, seq // tl),
        in_specs=[pl.BlockSpec((bsz, tl, LANES), lambda s, i: (0, i, s)),
                  pl.BlockSpec((m * bsz, STATE_COLS), lambda s, i: (i, s)),
                  pl.BlockSpec((None, SLAB_K, SLAB_K), lambda s, i: (s, 0, 0)),
                  pl.BlockSpec((None, STATE_COLS, SLAB_K), lambda s, i: (s, 0, 0))],
        out_specs=pl.BlockSpec((bsz, tl, LANES), lambda s, i: (0, i, s)),
        out_shape=jax.ShapeDtypeStruct((bsz, seq, S5_WIDTH), F32),
        compiler_params=_params(("parallel", "parallel")),
        name="s5_out",
    )(u3, carry, w_toep, w_out_state)


def _s5_weights(lam_re, lam_im, log_dt, b_re, b_im, c_re, c_im, d):
    hp = lax.Precision.HIGHEST
    G, P, H, C, S = S5_GROUPS, S5_STATE, S5_GROUP, CHUNK, GROUPS_PER_SLAB
    lr = lam_re.astype(F32)
    li = lam_im.astype(F32)
    dt = jnp.exp(log_dt.astype(F32))[..., None]
    mag = jnp.exp(lr * dt)
    ab_re = mag * jnp.cos(li * dt)
    ab_im = mag * jnp.sin(li * dt)
    nr = ab_re - 1.0
    den = lr * lr + li * li
    coef_re = (nr * lr + ab_im * li) / den
    coef_im = (ab_im * lr - nr * li) / den
    br = b_re.astype(F32)
    bi = b_im.astype(F32)
    bb_re = coef_re[..., None] * br - coef_im[..., None] * bi
    bb_im = coef_re[..., None] * bi + coef_im[..., None] * br
    cr = c_re.astype(F32)
    ci = c_im.astype(F32)

    k = jnp.arange(C + 1, dtype=F32)[:, None, None, None]
    pmag = jnp.exp(k * (lr * dt))
    pw_re = pmag * jnp.cos(k * (li * dt))
    pw_im = pmag * jnp.sin(k * (li * dt))

    cpw_re = cr[None] * pw_re[:C, :, :, None, :] - ci[None] * pw_im[:C, :, :, None, :]
    cpw_im = cr[None] * pw_im[:C, :, :, None, :] + ci[None] * pw_re[:C, :, :, None, :]
    klag = (jnp.einsum('kdghp,dgpj->dkghj', cpw_re, bb_re, precision=hp)
            - jnp.einsum('kdghp,dgpj->dkghj', cpw_im, bb_im, precision=hp))
    tp = jnp.arange(C)[:, None]
    tt = jnp.arange(C)[None, :]
    fwd = jnp.where((tt >= tp)[None, :, :, None, None], klag[0][jnp.clip(tt - tp, 0, C - 1)].transpose(2, 0, 1, 3, 4), 0.0)
    bwd = jnp.where((tt <= tp)[None, :, :, None, None], klag[1][jnp.clip(tp - tt, 0, C - 1)].transpose(2, 0, 1, 3, 4), 0.0)
    skip = (jnp.eye(C, dtype=F32)[None, :, :, None, None] * jnp.eye(H, dtype=F32)[None, None, None]
            * d.astype(F32).reshape(G, 1, 1, H, 1))
    toep = fwd + bwd + skip
    eye = jnp.eye(S, dtype=F32)
    toep = toep.reshape(N_SLABS, S, C, C, H, H)
    w_toep = jnp.einsum('sapthj,ab->spajtbh', toep, eye)
    w_toep = w_toep.reshape(N_SLABS, SLAB_K, SLAB_K).astype(BF16)

    e_re = jnp.stack([pw_re[C - 1::-1, 0][:C], pw_re[:C, 1]], axis=0)
    e_im = jnp.stack([pw_im[C - 1::-1, 0][:C], pw_im[:C, 1]], axis=0)
    vin_re = e_re[..., None] * bb_re[:, None] - e_im[..., None] * bb_im[:, None]
    vin_im = e_re[..., None] * bb_im[:, None] + e_im[..., None] * bb_re[:, None]
    vin = jnp.stack([vin_re, vin_im], axis=0)
    vin = vin.reshape(2, 2, C, N_SLABS, S, P, H)
    w_in = jnp.einsum('rdtsapj,ab->stajdrbp', vin, eye)
    w_in_state = w_in.reshape(N_SLABS, SLAB_K, STATE_COLS).astype(BF16)

    f_re = jnp.stack([pw_re[1:, 0], pw_re[C:0:-1, 1]], axis=0)
    f_im = jnp.stack([pw_im[1:, 0], pw_im[C:0:-1, 1]], axis=0)
    wo_re = cr[:, None] * f_re[:, :, :, None, :] - ci[:, None] * f_im[:, :, :, None, :]
    wo_im = cr[:, None] * f_im[:, :, :, None, :] + ci[:, None] * f_re[:, :, :, None, :]
    wo = jnp.stack([wo_re, -wo_im], axis=0)
    wo = wo.reshape(2, 2, C, N_SLABS, S, H, P)
    w_out = jnp.einsum('rdtsbhp,ab->sdraptbh', wo, eye)
    w_out_state = w_out.reshape(N_SLABS, STATE_COLS, SLAB_K).astype(BF16)

    a16 = jnp.stack([pw_re[C], pw_im[C]], axis=0)
    a16 = a16.reshape(2, 2, N_SLABS, S, P).transpose(2, 1, 0, 3, 4)
    a16 = jnp.broadcast_to(a16.reshape(1, N_SLABS * STATE_COLS), (SUBLANES, N_SLABS * STATE_COLS))
    return w_in_state, w_toep, w_out_state, a16


def _attn_kernel(sink_ref, q_ref, kl_ref, kc_ref, kr_ref, bias_ref, o_ref):
    j = pl.program_id(1)
    nb = pl.num_programs(1)
    kv = jnp.concatenate([kl_ref[...], kc_ref[...], kr_ref[...]], axis=0)
    col = lax.broadcasted_iota(jnp.int32, (1, 3 * BLOCK), 1)
    valid = ((col >= BLOCK) | (j > 0)) & ((col < 2 * BLOCK) | (j < nb - 1))
    outs = []
    for kvh in range(N_KV_HEADS):
        k = kv[:, kvh * HEAD_DIM:(kvh + 1) * HEAD_DIM]
        v = kv[:, KV_WIDTH + kvh * HEAD_DIM:KV_WIDTH + (kvh + 1) * HEAD_DIM]
        for g in range(Q_PER_KV):
            hd = kvh * Q_PER_KV + g
            qh = q_ref[:, hd * HEAD_DIM:(hd + 1) * HEAD_DIM]
            s = lax.dot_general(qh, k, (((1,), (1,)), ((), ())), preferred_element_type=F32)
            s = jnp.where(valid, s + bias_ref[hd], NEG_INF)
            sk = sink_ref[hd]
            mx = jnp.maximum(jnp.max(s, axis=-1, keepdims=True), sk)
            p = jnp.exp(s - mx)
            den = jnp.sum(p, axis=-1, keepdims=True) + jnp.exp(sk - mx)
            outs.append(_dot(p.astype(BF16), v) / den)
    o_ref[...] = jnp.concatenate(outs, axis=1).astype(BF16)


def _attention(q, kv, sink, bias, bsz, seq):
    nb = seq // BLOCK
    row = lambda b, j, *_: (b * nb + j, 0)
    left = lambda b, j, *_: (b * nb + jnp.maximum(j - 1, 0), 0)
    right = lambda b, j, *_: (b * nb + jnp.minimum(j + 1, nb - 1), 0)
    grid_spec = pltpu.PrefetchScalarGridSpec(
        num_scalar_prefetch=0,
        grid=(bsz, nb),
        in_specs=[pl.BlockSpec(memory_space=pltpu.SMEM),
                  pl.BlockSpec((BLOCK, ATT_WIDTH), row),
                  pl.BlockSpec((BLOCK, 2 * KV_WIDTH), left),
                  pl.BlockSpec((BLOCK, 2 * KV_WIDTH), row),
                  pl.BlockSpec((BLOCK, 2 * KV_WIDTH), right),
                  _const_spec((N_Q_HEADS, BLOCK, 3 * BLOCK))],
        out_specs=pl.BlockSpec((BLOCK, ATT_WIDTH), row),
    )
    return pl.pallas_call(
        _attn_kernel,
        grid_spec=grid_spec,
        out_shape=jax.ShapeDtypeStruct((bsz * seq, ATT_WIDTH), BF16),
        compiler_params=_params(("parallel", "parallel")),
        name="attention",
    )(sink, q, kv, kv, kv, bias)


def _t5_bucket(rel):
    half = NUM_BUCKETS // 2
    max_exact = half // 2
    ret = jnp.where(rel > 0, half, 0)
    n = jnp.abs(rel)
    nf = jnp.maximum(n, 1).astype(jnp.float32)
    large = max_exact + (jnp.log(nf / max_exact) / math.log(MAX_DISTANCE / max_exact)
                         * (half - max_exact)).astype(jnp.int32)
    large = jnp.minimum(large, half - 1)
    return ret + jnp.where(n < max_exact, n, large)


def _band_bias(rel_bias):
    q_loc = jnp.arange(BLOCK, dtype=jnp.int32)
    k_loc = jnp.arange(3 * BLOCK, dtype=jnp.int32)
    rel = (k_loc[None, :] - BLOCK) - q_loc[:, None]
    bias = jnp.transpose(rel_bias[_t5_bucket(rel)], (2, 0, 1)).astype(F32)
    return jnp.where((jnp.abs(rel) <= WINDOW)[None], bias, NEG_INF)


def _merge_kernel(h_ref, y_ref, yb_ref, g1_ref, wg_ref, wglu_ref, wa_ref, wb_ref, wo_ref, o_ref):
    h = h_ref[...]
    hn = _rms(h, g1_ref[...]).astype(BF16)
    gates = _sigmoid(_dot(hn, wg_ref[...]))
    z = _gelu_tanh(y_ref[...])
    za = (z * _sigmoid(_dot(z.astype(BF16), wglu_ref[...]))).astype(BF16)
    merged = (gates[:, :D_MODEL] * _dot(za, wa_ref[...])
              + gates[:, D_MODEL:] * _dot(yb_ref[...], wb_ref[...]))
    o_ref[...] = h + _dot(merged.astype(BF16), wo_ref[...])


def _merge(h, y, yb, g1, w_gates, w_glu, w_a, w_b, w_o, tm=512):
    t = h.shape[0]
    return pl.pallas_call(
        _merge_kernel,
        grid=(t // tm,),
        in_specs=[pl.BlockSpec((tm, D_MODEL), lambda i: (i, 0)),
                  pl.BlockSpec((tm, S5_WIDTH), lambda i: (i, 0)),
                  pl.BlockSpec((tm, ATT_WIDTH), lambda i: (i, 0)),
                  _const_spec((1, D_MODEL)),
                  _const_spec(w_gates.shape), _const_spec(w_glu.shape), _const_spec(w_a.shape),
                  _const_spec(w_b.shape), _const_spec(w_o.shape)],
        out_specs=pl.BlockSpec((tm, D_MODEL), lambda i: (i, 0)),
        out_shape=jax.ShapeDtypeStruct((t, D_MODEL), F32),
        compiler_params=_params(("parallel",)),
        name="merge",
    )(h, y, yb, g1, w_gates, w_glu, w_a, w_b, w_o)


def _ffn_kernel(h_ref, g2_ref, gf_ref, wg_ref, wu_ref, wd_ref, o_ref, acc_ref, *, final_norm):
    h = h_ref[...]
    hn = _rms(h, g2_ref[...]).astype(BF16)
    acc_ref[...] = h
    for j in range(D_FF // FF_TILE):
        cols = slice(j * FF_TILE, (j + 1) * FF_TILE)
        gate = _dot(hn, wg_ref[:, cols])
        up = _dot(hn, wu_ref[:, cols])
        act = (gate * _sigmoid(gate) * up).astype(BF16)
        acc_ref[...] += _dot(act, wd_ref[cols, :])
    out = acc_ref[...]
    o_ref[...] = _rms(out, gf_ref[...]) if final_norm else out


def _ffn(h, g2, gf, w_gate, w_up, w_down, final_norm, tm=512):
    t = h.shape[0]
    return pl.pallas_call(
        functools.partial(_ffn_kernel, final_norm=final_norm),
        grid=(t // tm,),
        in_specs=[pl.BlockSpec((tm, D_MODEL), lambda i: (i, 0)),
                  _const_spec((1, D_MODEL)), _const_spec((1, D_MODEL)),
                  _const_spec(w_gate.shape), _const_spec(w_up.shape), _const_spec(w_down.shape)],
        out_specs=pl.BlockSpec((tm, D_MODEL), lambda i: (i, 0)),
        out_shape=jax.ShapeDtypeStruct((t, D_MODEL), F32),
        scratch_shapes=[pltpu.VMEM((tm, D_MODEL), F32)],
        compiler_params=_params(("parallel",)),
        name="ffn",
    )(h, g2, gf, w_gate, w_up, w_down)


def kernel(x, norm1_g, norm2_g, final_g, w_in, s5_lambda_re, s5_lambda_im, s5_log_dt, s5_b_re, s5_b_im,
           s5_c_re, s5_c_im, s5_d, s5_w_glu, attn_sink, rel_bias, w_branch_a, w_branch_b, w_out,
           ffn_w_gate, ffn_w_up, ffn_w_down):
    bsz, seq, _ = x.shape
    depth = w_in.shape[0]
    assert seq % BLOCK == 0 and seq % CHUNK == 0 and bsz == SUBLANES
    n_chunks = seq // CHUNK
    t = bsz * seq
    bias = _band_bias(rel_bias)
    o_k = S5_WIDTH + ATT_WIDTH
    o_g = o_k + 2 * KV_WIDTH
    col_scale = jnp.concatenate([jnp.ones((S5_WIDTH,), F32), jnp.full((ATT_WIDTH,), HEAD_DIM ** -0.5, F32),
                                 jnp.ones((2 * KV_WIDTH,), F32)])
    gf = final_g.reshape(1, D_MODEL).astype(F32)
    h = x.reshape(t, D_MODEL)
    for layer in range(depth):
        g1 = norm1_g[layer].reshape(1, D_MODEL).astype(F32)
        g2 = norm2_g[layer].reshape(1, D_MODEL).astype(F32)
        w_uqkv = (w_in[layer][:, :o_g] * col_scale).astype(BF16)
        w_gates = w_in[layer][:, o_g:].astype(BF16)
        u, q, kv = _inproj(h, g1, w_uqkv)
        w_in_state, w_toep, w_out_state, a16 = _s5_weights(
            s5_lambda_re[layer], s5_lambda_im[layer], s5_log_dt[layer], s5_b_re[layer], s5_b_im[layer],
            s5_c_re[layer], s5_c_im[layer], s5_d[layer])
        u3 = u.reshape(bsz, seq, S5_WIDTH)
        sin = _s5_state_in(u3, w_in_state)
        carry = _s5_scan(sin, a16, n_chunks)
        y = _s5_out(u3, carry, w_toep, w_out_state).reshape(t, S5_WIDTH)
        yb = _attention(q, kv, attn_sink[layer].astype(F32), bias, bsz, seq)
        h = _merge(h, y, yb, g1, w_gates, s5_w_glu[layer].astype(BF16), w_branch_a[layer].astype(BF16),
                   w_branch_b[layer].astype(BF16), w_out[layer].astype(BF16))
        h = _ffn(h, g2, gf, ffn_w_gate[layer].astype(BF16), ffn_w_up[layer].astype(BF16),
                 ffn_w_down[layer].astype(BF16), final_norm=(layer == depth - 1))
    return h.reshape(bsz, seq, D_MODEL)
```

```python
import functools
import math

import jax
import jax.numpy as jnp
from jax import lax
from jax.experimental import pallas as pl
from jax.experimental.pallas import tpu as pltpu

F32 = jnp.float32
BF16 = jnp.bfloat16

D_MODEL = 1024
S5_WIDTH = 512
S5_GROUP = 16
S5_GROUPS = 32
S5_STATE = 64
HEAD_DIM = 64
N_Q_HEADS = 8
N_KV_HEADS = 2
Q_PER_KV = N_Q_HEADS // N_KV_HEADS
ATT_WIDTH = N_Q_HEADS * HEAD_DIM
KV_WIDTH = N_KV_HEADS * HEAD_DIM
WINDOW = 128
BLOCK = 128
NUM_BUCKETS = 32
MAX_DISTANCE = 128
D_FF = 2816
RMS_EPS = 1e-6
NEG_INF = -1e30

LANES = 128
SUBLANES = 8
CHUNK = 16
GROUPS_PER_SLAB = LANES // S5_GROUP
N_SLABS = S5_WIDTH // LANES
SLAB_K = CHUNK * LANES
STATE_COLS = 2 * 2 * GROUPS_PER_SLAB * S5_STATE
SCAN_TILE = STATE_COLS // 2
S5_TOKENS = 512
FF_TILE = 256
VMEM_LIMIT = 56 * 1024 * 1024


def _rms(x, g):
    return x * lax.rsqrt(jnp.mean(x * x, axis=-1, keepdims=True) + RMS_EPS) * g


def _gelu_tanh(x):
    return 0.5 * x * (1.0 + jnp.tanh(math.sqrt(2.0 / math.pi) * (x + 0.044715 * (x * x * x))))


def _sigmoid(x):
    return 1.0 / (1.0 + jnp.exp(-x))


def _dot(a, b):
    return jnp.dot(a, b, preferred_element_type=F32)


def _const_spec(shape):
    nd = len(shape)
    return pl.BlockSpec(shape, lambda *_: (0,) * nd, pipeline_mode=pl.Buffered(1))


def _params(sem):
    return pltpu.CompilerParams(dimension_semantics=sem, vmem_limit_bytes=VMEM_LIMIT)


def _inproj_kernel(x_ref, g_ref, w_ref, u_ref, q_ref, kv_ref):
    hn = _rms(x_ref[...], g_ref[...]).astype(BF16)
    r = _dot(hn, w_ref[...])
    u_ref[...] = r[:, :S5_WIDTH]
    q_ref[...] = r[:, S5_WIDTH:S5_WIDTH + ATT_WIDTH].astype(BF16)
    kv_ref[...] = r[:, S5_WIDTH + ATT_WIDTH:].astype(BF16)


def _inproj(h, g, w, tm=512):
    t = h.shape[0]
    n = w.shape[1]
    return pl.pallas_call(
        _inproj_kernel,
        grid=(t // tm,),
        in_specs=[pl.BlockSpec((tm, D_MODEL), lambda i: (i, 0)),
                  _const_spec((1, D_MODEL)), _const_spec((D_MODEL, n))],
        out_specs=[pl.BlockSpec((tm, S5_WIDTH), lambda i: (i, 0)),
                   pl.BlockSpec((tm, ATT_WIDTH), lambda i: (i, 0)),
                   pl.BlockSpec((tm, 2 * KV_WIDTH), lambda i: (i, 0))],
        out_shape=[jax.ShapeDtypeStruct((t, S5_WIDTH), F32),
                   jax.ShapeDtypeStruct((t, ATT_WIDTH), BF16),
                   jax.ShapeDtypeStruct((t, 2 * KV_WIDTH), BF16)],
        compiler_params=_params(("parallel",)),
        name="inproj",
    )(h, g, w)


def _chunk_rows(u_ref, m):
    rows = [jnp.concatenate([u_ref[:, c * CHUNK + t, :] for t in range(CHUNK)], axis=1) for c in range(m)]
    return jnp.concatenate(rows, axis=0).astype(BF16)


def _s5_in_kernel(u_ref, w_ref, o_ref):
    m = u_ref.shape[1] // CHUNK
    o_ref[...] = _dot(_chunk_rows(u_ref, m), w_ref[...])


def _s5_state_in(u3, w_in_state, tl=S5_TOKENS):
    bsz, seq, _ = u3.shape
    m = tl // CHUNK
    return pl.pallas_call(
        _s5_in_kernel,
        grid=(N_SLABS, seq // tl),
        in_specs=[pl.BlockSpec((bsz, tl, LANES), lambda s, i: (0, i, s)),
                  pl.BlockSpec((None, SLAB_K, STATE_COLS), lambda s, i: (s, 0, 0))],
        out_specs=pl.BlockSpec((m * bsz, STATE_COLS), lambda s, i: (i, s)),
        out_shape=jax.ShapeDtypeStruct((seq // CHUNK * bsz, N_SLABS * STATE_COLS), F32),
        compiler_params=_params(("parallel", "parallel")),
        name="s5_state_in",
    )(u3, w_in_state)


def _s5_scan_kernel(sin_ref, a_ref, o_ref, st_ref, *, cpt):
    half = SCAN_TILE // 2
    backward = pl.program_id(1) == 1

    @pl.when(pl.program_id(2) == 0)
    def _():
        st_ref[...] = jnp.zeros_like(st_ref)

    ar = a_ref[:, :half]
    ai = a_ref[:, half:]

    def body(i, carry):
        sr, si = carry
        c = jnp.where(backward, cpt - 1 - i, i)
        rows = pl.ds(pl.multiple_of(c * SUBLANES, SUBLANES), SUBLANES)
        o_ref[rows, :half] = sr
        o_ref[rows, half:] = si
        xr = sin_ref[rows, :half]
        xi = sin_ref[rows, half:]
        return ar * sr - ai * si + xr, ar * si + ai * sr + xi

    sr, si = lax.fori_loop(0, cpt, body, (st_ref[:, :half], st_ref[:, half:]), unroll=4)
    st_ref[:, :half] = sr
    st_ref[:, half:] = si


def _s5_scan(sin, a16, n_chunks, cpt=64):
    nct = n_chunks // cpt
    blk = lambda s, d, k: (k + d * (nct - 1 - 2 * k), s * 2 + d)
    return pl.pallas_call(
        functools.partial(_s5_scan_kernel, cpt=cpt),
        grid=(N_SLABS, 2, nct),
        in_specs=[pl.BlockSpec((cpt * SUBLANES, SCAN_TILE), blk),
                  pl.BlockSpec((SUBLANES, SCAN_TILE), lambda s, d, k: (0, s * 2 + d))],
        out_specs=pl.BlockSpec((cpt * SUBLANES, SCAN_TILE), blk),
        out_shape=jax.ShapeDtypeStruct(sin.shape, F32),
        scratch_shapes=[pltpu.VMEM((SUBLANES, SCAN_TILE), F32)],
        compiler_params=_params(("parallel", "parallel", "arbitrary")),
        name="s5_scan",
    )(sin, a16)


def _s5_out_kernel(u_ref, c_ref, wt_ref, wo_ref, y_ref):
    bsz = u_ref.shape[0]
    m = u_ref.shape[1] // CHUNK
    r = _dot(_chunk_rows(u_ref, m), wt_ref[...]) + _dot(c_ref[...].astype(BF16), wo_ref[...])
    for c in range(m):
        for t in range(CHUNK):
            y_ref[:, c * CHUNK + t, :] = r[c * bsz:(c + 1) * bsz, t * LANES:(t + 1) * LANES]


def _s5_out(u3, carry, w_toep, w_out_state, tl=S5_TOKENS):
    bsz, seq, _ = u3.shape
    m = tl // CHUNK
    return pl.pallas_call(
        _s5_out_kernel,
        grid=(N_SLABS, seq // tl),
        in_specs=[pl.BlockSpec((bsz, tl, LANES), lambda s, i: (0, i, s)),
                  pl.BlockSpec((m * bsz, STATE_COLS), lambda s, i: (i, s)),
                  pl.BlockSpec((None, SLAB_K, SLAB_K), lambda s, i: (s, 0, 0)),
                  pl.BlockSpec((None, STATE_COLS, SLAB_K), lambda s, i: (s, 0, 0))],
        out_specs=pl.BlockSpec((bsz, tl, LANES), lambda s, i: (0, i, s)),
        out_shape=jax.ShapeDtypeStruct((bsz, seq, S5_WIDTH), F32),
        compiler_params=_params(("parallel", "parallel")),
        name="s5_out",
    )(u3, carry, w_toep, w_out_state)


def _s5_weights(lam_re, lam_im, log_dt, b_re, b_im, c_re, c_im, d):
    hp = lax.Precision.HIGHEST
    G, P, H, C, S = S5_GROUPS, S5_STATE, S5_GROUP, CHUNK, GROUPS_PER_SLAB
    lr = lam_re.astype(F32)
    li = lam_im.astype(F32)
    dt = jnp.exp(log_dt.astype(F32))[..., None]
    mag = jnp.exp(lr * dt)
    ab_re = mag * jnp.cos(li * dt)
    ab_im = mag * jnp.sin(li * dt)
    nr = ab_re - 1.0
    den = lr * lr + li * li
    coef_re = (nr * lr + ab_im * li) / den
    coef_im = (ab_im * lr - nr * li) / den
    br = b_re.astype(F32)
    bi = b_im.astype(F32)
    bb_re = coef_re[..., None] * br - coef_im[..., None] * bi
    bb_im = coef_re[..., None] * bi + coef_im[..., None] * br
    cr = c_re.astype(F32)
    ci = c_im.astype(F32)

    k = jnp.arange(C + 1, dtype=F32)[:, None, None, None]
    pmag = jnp.exp(k * (lr * dt))
    pw_re = pmag * jnp.cos(k * (li * dt))
    pw_im = pmag * jnp.sin(k * (li * dt))

    cpw_re = cr[None] * pw_re[:C, :, :, None, :] - ci[None] * pw_im[:C, :, :, None, :]
    cpw_im = cr[None] * pw_im[:C, :, :, None, :] + ci[None] * pw_re[:C, :, :, None, :]
    klag = (jnp.einsum('kdghp,dgpj->dkghj', cpw_re, bb_re, precision=hp)
            - jnp.einsum('kdghp,dgpj->dkghj', cpw_im, bb_im, precision=hp))
    eye = jnp.eye(S, dtype=F32)
    klag_t = klag.transpose(0, 1, 2, 4, 3)
    centre = klag_t[0, 0] + klag_t[1, 0] + jnp.eye(H, dtype=F32)[None] * d.astype(F32).reshape(G, 1, H)
    lags = jnp.concatenate([klag_t[1, C - 1:0:-1], centre[None], klag_t[0, 1:]], axis=0)
    lags = lags.reshape(2 * C - 1, N_SLABS, S, H, H)
    bd = lags[:, :, :, :, None, :] * eye[None, None, :, None, :, None]
    bd = bd.transpose(1, 0, 2, 3, 4, 5).reshape(N_SLABS, 2 * C - 1, LANES, LANES)

    def rows_by_state(x):
        x = x.reshape(2, N_SLABS, S, P, H).transpose(1, 2, 4, 0, 3)
        x = x[:, :, :, :, None, :] * eye[None, :, None, None, :, None]
        return x.reshape(N_SLABS, LANES, STATE_COLS // 2)

    def state_by_cols(x):
        x = x.reshape(2, N_SLABS, S, H, P).transpose(1, 0, 2, 4, 3)
        x = x[:, :, :, :, None, :] * eye[None, None, :, None, :, None]
        return x.reshape(N_SLABS, STATE_COLS // 2, LANES)

    def power_rows(x):
        return x.reshape(2, C, N_SLABS, S, P).transpose(2, 1, 0, 3, 4).reshape(N_SLABS, C, STATE_COLS // 2)

    def power_cols(x):
        return x.reshape(2, C, N_SLABS, S, P).transpose(2, 0, 3, 4, 1).reshape(N_SLABS, STATE_COLS // 2, C)

    e_re = jnp.stack([pw_re[C - 1::-1, 0], pw_re[:C, 1]], axis=0)
    e_im = jnp.stack([pw_im[C - 1::-1, 0], pw_im[:C, 1]], axis=0)
    f_re = jnp.stack([pw_re[1:, 0], pw_re[C:0:-1, 1]], axis=0)
    f_im = jnp.stack([pw_im[1:, 0], pw_im[C:0:-1, 1]], axis=0)

    a16 = jnp.stack([pw_re[C], pw_im[C]], axis=0)
    a16 = a16.reshape(2, 2, N_SLABS, S, P).transpose(2, 1, 0, 3, 4)
    a16 = jnp.broadcast_to(a16.reshape(1, N_SLABS * STATE_COLS), (SUBLANES, N_SLABS * STATE_COLS))
    w_toep, w_in_state, w_out_state = _s5_expand(
        bd, rows_by_state(bb_re), rows_by_state(bb_im), power_rows(e_re), power_rows(e_im),
        state_by_cols(cr), state_by_cols(ci), power_cols(f_re), power_cols(f_im))
    return w_in_state, w_toep, w_out_state, a16


def _s5_expand_kernel(bd_ref, bbr_ref, bbi_ref, er_ref, ei_ref, cr_ref, ci_ref, fr_ref, fi_ref,
                      wt_ref, wi_ref, wo_ref):
    k = pl.program_id(1)
    half = STATE_COLS // 4
    for t in range(CHUNK):
        wt_ref[:, t * LANES:(t + 1) * LANES] = bd_ref[t - k + (CHUNK - 1)].astype(BF16)
    lane = lax.broadcasted_iota(jnp.int32, (STATE_COLS // 2, CHUNK), 1)
    fr = jnp.sum(jnp.where(lane == k, fr_ref[...], 0.0), axis=1, keepdims=True)
    fi = jnp.sum(jnp.where(lane == k, fi_ref[...], 0.0), axis=1, keepdims=True)
    for dr in range(2):
        sl = slice(dr * half, (dr + 1) * half)
        er = er_ref[pl.ds(k, 1), sl]
        ei = ei_ref[pl.ds(k, 1), sl]
        br = bbr_ref[:, sl]
        bi = bbi_ref[:, sl]
        wi_ref[:, 2 * dr * half:(2 * dr + 1) * half] = (br * er - bi * ei).astype(BF16)
        wi_ref[:, (2 * dr + 1) * half:(2 * dr + 2) * half] = (br * ei + bi * er).astype(BF16)
        cr = cr_ref[sl, :]
        ci = ci_ref[sl, :]
        wo_ref[2 * dr * half:(2 * dr + 1) * half, :] = (cr * fr[sl] - ci * fi[sl]).astype(BF16)
        wo_ref[(2 * dr + 1) * half:(2 * dr + 2) * half, :] = (-(cr * fi[sl] + ci * fr[sl])).astype(BF16)


def _s5_expand(bd, bbr, bbi, er, ei, cr, ci, fr, fi):
    nl = 2 * CHUNK - 1
    hs = STATE_COLS // 2
    slab = lambda shape: pl.BlockSpec((None,) + shape, lambda s, k: (s,) + (0,) * len(shape))
    big = jax.ShapeDtypeStruct((N_SLABS, SLAB_K, SLAB_K), BF16)
    return pl.pallas_call(
        _s5_expand_kernel,
        grid=(N_SLABS, CHUNK),
        in_specs=[slab((nl, LANES, LANES)), slab((LANES, hs)), slab((LANES, hs)), slab((CHUNK, hs)),
                  slab((CHUNK, hs)), slab((hs, LANES)), slab((hs, LANES)), slab((hs, CHUNK)), slab((hs, CHUNK))],
        out_specs=[pl.BlockSpec((None, LANES, SLAB_K), lambda s, k: (s, k, 0)),
                   pl.BlockSpec((None, LANES, STATE_COLS), lambda s, k: (s, k, 0)),
                   pl.BlockSpec((None, STATE_COLS, LANES), lambda s, k: (s, 0, k))],
        out_shape=[big, big, big],
        compiler_params=_params(("parallel", "parallel")),
        name="s5_expand",
    )(bd, bbr, bbi, er, ei, cr, ci, fr, fi)


def _attn_kernel(sink_ref, q_ref, kl_ref, kc_ref, kr_ref, bias_ref, o_ref):
    j = pl.program_id(1)
    nb = pl.num_programs(1)
    nk = 3 * BLOCK
    kv = jnp.concatenate([kl_ref[...], kc_ref[...], kr_ref[...]], axis=0)
    col = lax.broadcasted_iota(jnp.int32, (1, 2 * nk), 1) % nk
    valid = ((col >= BLOCK) | (j > 0)) & ((col < 2 * BLOCK) | (j < nb - 1))
    lane = lax.broadcasted_iota(jnp.int32, (nk, LANES), 1)
    low = lane < HEAD_DIM
    zero = jnp.zeros((nk, LANES), BF16)

    def block_diag(tile, kvh):
        other = pltpu.roll(tile, HEAD_DIM, 1)
        first, second = (tile, other) if kvh == 0 else (other, tile)
        return jnp.concatenate([jnp.where(low, first, zero), jnp.where(low, zero, second)], axis=0)

    for kvh in range(N_KV_HEADS):
        kbd = block_diag(kv[:, :LANES], kvh)
        vbd = block_diag(kv[:, LANES:], kvh)
        for pair in range(Q_PER_KV // 2):
            pi = kvh * (Q_PER_KV // 2) + pair
            qp = q_ref[:, pi * LANES:(pi + 1) * LANES]
            s = lax.dot_general(qp, kbd, (((1,), (1,)), ((), ())), preferred_element_type=F32)
            s = jnp.where(valid, s + bias_ref[pi], NEG_INF)
            ps, dens = [], []
            for e in range(2):
                se = s[:, e * nk:(e + 1) * nk]
                sk = sink_ref[2 * pi + e]
                mx = jnp.maximum(jnp.max(se, axis=-1, keepdims=True), sk)
                p = jnp.exp(se - mx)
                den = jnp.sum(p, axis=-1, keepdims=True) + jnp.exp(sk - mx)
                ps.append(p.astype(BF16))
                dens.append(jnp.broadcast_to(1.0 / den, (BLOCK, HEAD_DIM)))
            o = _dot(jnp.concatenate(ps, axis=1), vbd) * jnp.concatenate(dens, axis=1)
            o_ref[:, pi * LANES:(pi + 1) * LANES] = o.astype(BF16)


def _attention(q, kv, sink, bias, bsz, seq):
    nb = seq // BLOCK
    row = lambda b, j, *_: (b * nb + j, 0)
    left = lambda b, j, *_: (b * nb + jnp.maximum(j - 1, 0), 0)
    right = lambda b, j, *_: (b * nb + jnp.minimum(j + 1, nb - 1), 0)
    grid_spec = pltpu.PrefetchScalarGridSpec(
        num_scalar_prefetch=0,
        grid=(bsz, nb),
        in_specs=[pl.BlockSpec(memory_space=pltpu.SMEM),
                  pl.BlockSpec((BLOCK, ATT_WIDTH), row),
                  pl.BlockSpec((BLOCK, 2 * KV_WIDTH), left),
                  pl.BlockSpec((BLOCK, 2 * KV_WIDTH), row),
                  pl.BlockSpec((BLOCK, 2 * KV_WIDTH), right),
                  _const_spec((N_Q_HEADS // 2, BLOCK, 6 * BLOCK))],
        out_specs=pl.BlockSpec((BLOCK, ATT_WIDTH), row),
    )
    return pl.pallas_call(
        _attn_kernel,
        grid_spec=grid_spec,
        out_shape=jax.ShapeDtypeStruct((bsz * seq, ATT_WIDTH), BF16),
        compiler_params=_params(("parallel", "parallel")),
        name="attention",
    )(sink, q, kv, kv, kv, bias)


def _t5_bucket(rel):
    half = NUM_BUCKETS // 2
    max_exact = half // 2
    ret = jnp.where(rel > 0, half, 0)
    n = jnp.abs(rel)
    nf = jnp.maximum(n, 1).astype(jnp.float32)
    large = max_exact + (jnp.log(nf / max_exact) / math.log(MAX_DISTANCE / max_exact)
                         * (half - max_exact)).astype(jnp.int32)
    large = jnp.minimum(large, half - 1)
    return ret + jnp.where(n < max_exact, n, large)


def _band_bias(rel_bias):
    q_loc = jnp.arange(BLOCK, dtype=jnp.int32)
    k_loc = jnp.arange(3 * BLOCK, dtype=jnp.int32)
    rel = (k_loc[None, :] - BLOCK) - q_loc[:, None]
    bias = jnp.transpose(rel_bias[_t5_bucket(rel)], (2, 0, 1)).astype(F32)
    bias = jnp.where((jnp.abs(rel) <= WINDOW)[None], bias, NEG_INF)
    bias = bias.reshape(N_Q_HEADS // 2, 2, BLOCK, 3 * BLOCK).transpose(0, 2, 1, 3)
    return bias.reshape(N_Q_HEADS // 2, BLOCK, 6 * BLOCK)


def _merge_kernel(h_ref, y_ref, yb_ref, g1_ref, wg_ref, wglu_ref, wa_ref, wb_ref, wo_ref, o_ref):
    h = h_ref[...]
    hn = _rms(h, g1_ref[...]).astype(BF16)
    gates = _sigmoid(_dot(hn, wg_ref[...]))
    z = _gelu_tanh(y_ref[...])
    za = (z * _sigmoid(_dot(z.astype(BF16), wglu_ref[...]))).astype(BF16)
    merged = (gates[:, :D_MODEL] * _dot(za, wa_ref[...])
              + gates[:, D_MODEL:] * _dot(yb_ref[...], wb_ref[...]))
    o_ref[...] = h + _dot(merged.astype(BF16), wo_ref[...])


def _merge(h, y, yb, g1, w_gates, w_glu, w_a, w_b, w_o, tm=512):
    t = h.shape[0]
    return pl.pallas_call(
        _merge_kernel,
        grid=(t // tm,),
        in_specs=[pl.BlockSpec((tm, D_MODEL), lambda i: (i, 0)),
                  pl.BlockSpec((tm, S5_WIDTH), lambda i: (i, 0)),
                  pl.BlockSpec((tm, ATT_WIDTH), lambda i: (i, 0)),
                  _const_spec((1, D_MODEL)),
                  _const_spec(w_gates.shape), _const_spec(w_glu.shape), _const_spec(w_a.shape),
                  _const_spec(w_b.shape), _const_spec(w_o.shape)],
        out_specs=pl.BlockSpec((tm, D_MODEL), lambda i: (i, 0)),
        out_shape=jax.ShapeDtypeStruct((t, D_MODEL), F32),
        compiler_params=_params(("parallel",)),
        name="merge",
    )(h, y, yb, g1, w_gates, w_glu, w_a, w_b, w_o)


def _ffn_kernel(h_ref, g2_ref, gf_ref, wg_ref, wu_ref, wd_ref, o_ref, acc_ref, *, final_norm):
    h = h_ref[...]
    hn = _rms(h, g2_ref[...]).astype(BF16)
    acc_ref[...] = h
    for j in range(D_FF // FF_TILE):
        cols = slice(j * FF_TILE, (j + 1) * FF_TILE)
        gate = _dot(hn, wg_ref[:, cols])
        up = _dot(hn, wu_ref[:, cols])
        act = (gate * _sigmoid(gate) * up).astype(BF16)
        acc_ref[...] += _dot(act, wd_ref[cols, :])
    out = acc_ref[...]
    o_ref[...] = _rms(out, gf_ref[...]) if final_norm else out


def _ffn(h, g2, gf, w_gate, w_up, w_down, final_norm, tm=512):
    t = h.shape[0]
    return pl.pallas_call(
        functools.partial(_ffn_kernel, final_norm=final_norm),
        grid=(t // tm,),
        in_specs=[pl.BlockSpec((tm, D_MODEL), lambda i: (i, 0)),
                  _const_spec((1, D_MODEL)), _const_spec((1, D_MODEL)),
                  _const_spec(w_gate.shape), _const_spec(w_up.shape), _const_spec(w_down.shape)],
        out_specs=pl.BlockSpec((tm, D_MODEL), lambda i: (i, 0)),
        out_shape=jax.ShapeDtypeStruct((t, D_MODEL), F32),
        scratch_shapes=[pltpu.VMEM((tm, D_MODEL), F32)],
        compiler_params=_params(("parallel",)),
        name="ffn",
    )(h, g2, gf, w_gate, w_up, w_down)


def kernel(x, norm1_g, norm2_g, final_g, w_in, s5_lambda_re, s5_lambda_im, s5_log_dt, s5_b_re, s5_b_im,
           s5_c_re, s5_c_im, s5_d, s5_w_glu, attn_sink, rel_bias, w_branch_a, w_branch_b, w_out,
           ffn_w_gate, ffn_w_up, ffn_w_down):
    bsz, seq, _ = x.shape
    depth = w_in.shape[0]
    assert seq % BLOCK == 0 and seq % CHUNK == 0 and bsz == SUBLANES
    n_chunks = seq // CHUNK
    t = bsz * seq
    bias = _band_bias(rel_bias)
    o_k = S5_WIDTH + ATT_WIDTH
    o_g = o_k + 2 * KV_WIDTH
    col_scale = jnp.concatenate([jnp.ones((S5_WIDTH,), F32), jnp.full((ATT_WIDTH,), HEAD_DIM ** -0.5, F32),
                                 jnp.ones((2 * KV_WIDTH,), F32)])
    gf = final_g.reshape(1, D_MODEL).astype(F32)
    h = x.reshape(t, D_MODEL)
    for layer in range(depth):
        g1 = norm1_g[layer].reshape(1, D_MODEL).astype(F32)
        g2 = norm2_g[layer].reshape(1, D_MODEL).astype(F32)
        w_uqkv = (w_in[layer][:, :o_g] * col_scale).astype(BF16)
        w_gates = w_in[layer][:, o_g:].astype(BF16)
        u, q, kv = _inproj(h, g1, w_uqkv)
        w_in_state, w_toep, w_out_state, a16 = _s5_weights(
            s5_lambda_re[layer], s5_lambda_im[layer], s5_log_dt[layer], s5_b_re[layer], s5_b_im[layer],
            s5_c_re[layer], s5_c_im[layer], s5_d[layer])
        u3 = u.reshape(bsz, seq, S5_WIDTH)
        sin = _s5_state_in(u3, w_in_state)
        carry = _s5_scan(sin, a16, n_chunks)
        y = _s5_out(u3, carry, w_toep, w_out_state).reshape(t, S5_WIDTH)
        yb = _attention(q, kv, attn_sink[layer].astype(F32), bias, bsz, seq)
        h = _merge(h, y, yb, g1, w_gates, s5_w_glu[layer].astype(BF16), w_branch_a[layer].astype(BF16),
                   w_branch_b[layer].astype(BF16), w_out[layer].astype(BF16))
        h = _ffn(h, g2, gf, ffn_w_gate[layer].astype(BF16), ffn_w_up[layer].astype(BF16),
                 ffn_w_down[layer].astype(BF16), final_norm=(layer == depth - 1))
    return h.reshape(bsz, seq, D_MODEL)
```

```python
import functools
import math

import jax
import jax.numpy as jnp
from jax import lax
from jax.experimental import pallas as pl
from jax.experimental.pallas import tpu as pltpu

F32 = jnp.float32
BF16 = jnp.bfloat16

D_MODEL = 1024
S5_WIDTH = 512
S5_GROUP = 16
S5_GROUPS = 32
S5_STATE = 64
HEAD_DIM = 64
N_Q_HEADS = 8
N_KV_HEADS = 2
Q_PER_KV = N_Q_HEADS // N_KV_HEADS
ATT_WIDTH = N_Q_HEADS * HEAD_DIM
KV_WIDTH = N_KV_HEADS * HEAD_DIM
WINDOW = 128
BLOCK = 128
NUM_BUCKETS = 32
MAX_DISTANCE = 128
D_FF = 2816
RMS_EPS = 1e-6
NEG_INF = -1e30
LOG2E = math.log2(math.e)

LANES = 128
SUBLANES = 8
CHUNK = 16
GROUPS_PER_SLAB = LANES // S5_GROUP
N_SLABS = S5_WIDTH // LANES
SLAB_K = CHUNK * LANES
STATE_COLS = 2 * 2 * GROUPS_PER_SLAB * S5_STATE
SCAN_TILE = STATE_COLS // 2
S5_TOKENS = 512
FF_TILE = 256
VMEM_LIMIT = 56 * 1024 * 1024


def _rms(x, g):
    return x * lax.rsqrt(jnp.mean(x * x, axis=-1, keepdims=True) + RMS_EPS) * g


def _gelu_tanh(x):
    return 0.5 * x * (1.0 + jnp.tanh(math.sqrt(2.0 / math.pi) * (x + 0.044715 * (x * x * x))))


def _sigmoid(x):
    return 1.0 / (1.0 + jnp.exp(-x))


def _dot(a, b):
    return jnp.dot(a, b, preferred_element_type=F32)


def _const_spec(shape):
    nd = len(shape)
    return pl.BlockSpec(shape, lambda *_: (0,) * nd, pipeline_mode=pl.Buffered(1))


def _params(sem):
    return pltpu.CompilerParams(dimension_semantics=sem, vmem_limit_bytes=VMEM_LIMIT)


def _inproj_kernel(x_ref, g_ref, w_ref, u_ref, q_ref, kv_ref):
    hn = _rms(x_ref[...], g_ref[...]).astype(BF16)
    r = _dot(hn, w_ref[...])
    u_ref[...] = r[:, :S5_WIDTH]
    q_ref[...] = r[:, S5_WIDTH:S5_WIDTH + ATT_WIDTH].astype(BF16)
    kv_ref[...] = r[:, S5_WIDTH + ATT_WIDTH:].astype(BF16)


def _inproj(h, g, w, tm=512):
    t = h.shape[0]
    n = w.shape[1]
    return pl.pallas_call(
        _inproj_kernel,
        grid=(t // tm,),
        in_specs=[pl.BlockSpec((tm, D_MODEL), lambda i: (i, 0)),
                  _const_spec((1, D_MODEL)), _const_spec((D_MODEL, n))],
        out_specs=[pl.BlockSpec((tm, S5_WIDTH), lambda i: (i, 0)),
                   pl.BlockSpec((tm, ATT_WIDTH), lambda i: (i, 0)),
                   pl.BlockSpec((tm, 2 * KV_WIDTH), lambda i: (i, 0))],
        out_shape=[jax.ShapeDtypeStruct((t, S5_WIDTH), F32),
                   jax.ShapeDtypeStruct((t, ATT_WIDTH), BF16),
                   jax.ShapeDtypeStruct((t, 2 * KV_WIDTH), BF16)],
        compiler_params=_params(("parallel",)),
        name="inproj",
    )(h, g, w)


def _chunk_rows(u_ref, m):
    rows = [jnp.concatenate([u_ref[:, c * CHUNK + t, :] for t in range(CHUNK)], axis=1) for c in range(m)]
    return jnp.concatenate(rows, axis=0).astype(BF16)


def _s5_in_kernel(u_ref, w_ref, o_ref):
    m = u_ref.shape[1] // CHUNK
    o_ref[...] = _dot(_chunk_rows(u_ref, m), w_ref[...])


def _s5_state_in(u3, w_in_state, tl=S5_TOKENS):
    bsz, seq, _ = u3.shape
    m = tl // CHUNK
    return pl.pallas_call(
        _s5_in_kernel,
        grid=(N_SLABS, seq // tl),
        in_specs=[pl.BlockSpec((bsz, tl, LANES), lambda s, i: (0, i, s)),
                  pl.BlockSpec((None, SLAB_K, STATE_COLS), lambda s, i: (s, 0, 0))],
        out_specs=pl.BlockSpec((m * bsz, STATE_COLS), lambda s, i: (i, s)),
        out_shape=jax.ShapeDtypeStruct((seq // CHUNK * bsz, N_SLABS * STATE_COLS), F32),
        compiler_params=_params(("parallel", "parallel")),
        name="s5_state_in",
    )(u3, w_in_state)


def _s5_scan_kernel(sin_ref, a_ref, o_ref, st_ref, *, cpt):
    half = SCAN_TILE // 2
    backward = pl.program_id(1) == 1

    @pl.when(pl.program_id(2) == 0)
    def _():
        st_ref[...] = jnp.zeros_like(st_ref)

    ar = a_ref[:, :half]
    ai = a_ref[:, half:]

    def body(i, carry):
        sr, si = carry
        c = jnp.where(backward, cpt - 1 - i, i)
        rows = pl.ds(pl.multiple_of(c * SUBLANES, SUBLANES), SUBLANES)
        o_ref[rows, :half] = sr
        o_ref[rows, half:] = si
        xr = sin_ref[rows, :half]
        xi = sin_ref[rows, half:]
        return ar * sr - ai * si + xr, ar * si + ai * sr + xi

    sr, si = lax.fori_loop(0, cpt, body, (st_ref[:, :half], st_ref[:, half:]), unroll=4)
    st_ref[:, :half] = sr
    st_ref[:, half:] = si


def _s5_scan(sin, a16, n_chunks, cpt=64):
    nct = n_chunks // cpt
    blk = lambda s, d, k: (k + d * (nct - 1 - 2 * k), s * 2 + d)
    return pl.pallas_call(
        functools.partial(_s5_scan_kernel, cpt=cpt),
        grid=(N_SLABS, 2, nct),
        in_specs=[pl.BlockSpec((cpt * SUBLANES, SCAN_TILE), blk),
                  pl.BlockSpec((SUBLANES, SCAN_TILE), lambda s, d, k: (0, s * 2 + d))],
        out_specs=pl.BlockSpec((cpt * SUBLANES, SCAN_TILE), blk),
        out_shape=jax.ShapeDtypeStruct(sin.shape, F32),
        scratch_shapes=[pltpu.VMEM((SUBLANES, SCAN_TILE), F32)],
        compiler_params=_params(("parallel", "parallel", "arbitrary")),
        name="s5_scan",
    )(sin, a16)


def _s5_out_kernel(u_ref, c_ref, wt_ref, wo_ref, y_ref):
    bsz = u_ref.shape[0]
    m = u_ref.shape[1] // CHUNK
    r = _dot(_chunk_rows(u_ref, m), wt_ref[...]) + _dot(c_ref[...].astype(BF16), wo_ref[...])
    for c in range(m):
        for t in range(CHUNK):
            y_ref[:, c * CHUNK + t, :] = r[c * bsz:(c + 1) * bsz, t * LANES:(t + 1) * LANES]


def _s5_out(u3, carry, w_toep, w_out_state, tl=S5_TOKENS):
    bsz, seq, _ = u3.shape
    m = tl // CHUNK
    return pl.pallas_call(
        _s5_out_kernel,
        grid=(N_SLABS, seq // tl),
        in_specs=[pl.BlockSpec((bsz, tl, LANES), lambda s, i: (0, i, s)),
                  pl.BlockSpec((m * bsz, STATE_COLS), lambda s, i: (i, s)),
                  pl.BlockSpec((None, SLAB_K, SLAB_K), lambda s, i: (s, 0, 0)),
                  pl.BlockSpec((None, STATE_COLS, SLAB_K), lambda s, i: (s, 0, 0))],
        out_specs=pl.BlockSpec((bsz, tl, LANES), lambda s, i: (0, i, s)),
        out_shape=jax.ShapeDtypeStruct((bsz, seq, S5_WIDTH), F32),
        compiler_params=_params(("parallel", "parallel")),
        name="s5_out",
    )(u3, carry, w_toep, w_out_state)


def _s5_weights(lam_re, lam_im, log_dt, b_re, b_im, c_re, c_im, d):
    G, P, H, C, S = S5_GROUPS, S5_STATE, S5_GROUP, CHUNK, GROUPS_PER_SLAB
    lr = lam_re.astype(F32)
    li = lam_im.astype(F32)
    dt = jnp.exp(log_dt.astype(F32))[..., None]
    mag = jnp.exp(lr * dt)
    ab_re = mag * jnp.cos(li * dt)
    ab_im = mag * jnp.sin(li * dt)
    nr = ab_re - 1.0
    den = lr * lr + li * li
    coef_re = (nr * lr + ab_im * li) / den
    coef_im = (ab_im * lr - nr * li) / den
    br = b_re.astype(F32)
    bi = b_im.astype(F32)
    bb_re = coef_re[..., None] * br - coef_im[..., None] * bi
    bb_im = coef_re[..., None] * bi + coef_im[..., None] * br
    cr = c_re.astype(F32)
    ci = c_im.astype(F32)

    k = jnp.arange(C + 1, dtype=F32)[:, None, None, None]
    pmag = jnp.exp(k * (lr * dt))
    pw_re = pmag * jnp.cos(k * (li * dt))
    pw_im = pmag * jnp.sin(k * (li * dt))

    cpw_re = cr[None] * pw_re[:C, :, :, None, :] - ci[None] * pw_im[:C, :, :, None, :]
    cpw_im = cr[None] * pw_im[:C, :, :, None, :] + ci[None] * pw_re[:C, :, :, None, :]
    bbt_re = bb_re.transpose(0, 1, 3, 2)[None, :, :, :, None, :]
    bbt_im = bb_im.transpose(0, 1, 3, 2)[None, :, :, :, None, :]
    klag_t = jnp.sum(cpw_re[:, :, :, None] * bbt_re - cpw_im[:, :, :, None] * bbt_im, axis=-1)
    klag_t = klag_t.transpose(1, 0, 2, 3, 4)
    eye = jnp.eye(S, dtype=F32)
    centre = klag_t[0, 0] + klag_t[1, 0] + jnp.eye(H, dtype=F32)[None] * d.astype(F32).reshape(G, 1, H)
    lags = jnp.concatenate([klag_t[1, C - 1:0:-1], centre[None], klag_t[0, 1:]], axis=0)
    lags = lags.reshape(2 * C - 1, N_SLABS, S, H, H)
    bd = lags[:, :, :, :, None, :] * eye[None, None, :, None, :, None]
    bd = bd.transpose(1, 0, 2, 3, 4, 5).reshape(N_SLABS, 2 * C - 1, LANES, LANES)

    def rows_by_state(x):
        x = x.reshape(2, N_SLABS, S, P, H).transpose(1, 2, 4, 0, 3)
        x = x[:, :, :, :, None, :] * eye[None, :, None, None, :, None]
        return x.reshape(N_SLABS, LANES, STATE_COLS // 2)

    def state_by_cols(x):
        x = x.reshape(2, N_SLABS, S, H, P).transpose(1, 0, 2, 4, 3)
        x = x[:, :, :, :, None, :] * eye[None, None, :, None, :, None]
        return x.reshape(N_SLABS, STATE_COLS // 2, LANES)

    def power_rows(x):
        return x.reshape(2, C, N_SLABS, S, P).transpose(2, 1, 0, 3, 4).reshape(N_SLABS, C, STATE_COLS // 2)

    def power_cols(x):
        return x.reshape(2, C, N_SLABS, S, P).transpose(2, 0, 3, 4, 1).reshape(N_SLABS, STATE_COLS // 2, C)

    e_re = jnp.stack([pw_re[C - 1::-1, 0], pw_re[:C, 1]], axis=0)
    e_im = jnp.stack([pw_im[C - 1::-1, 0], pw_im[:C, 1]], axis=0)
    f_re = jnp.stack([pw_re[1:, 0], pw_re[C:0:-1, 1]], axis=0)
    f_im = jnp.stack([pw_im[1:, 0], pw_im[C:0:-1, 1]], axis=0)

    a16 = jnp.stack([pw_re[C], pw_im[C]], axis=0)
    a16 = a16.reshape(2, 2, N_SLABS, S, P).transpose(2, 1, 0, 3, 4)
    a16 = jnp.broadcast_to(a16.reshape(1, N_SLABS * STATE_COLS), (SUBLANES, N_SLABS * STATE_COLS))
    w_toep, w_in_state, w_out_state = _s5_expand(
        bd, rows_by_state(bb_re), rows_by_state(bb_im), power_rows(e_re), power_rows(e_im),
        state_by_cols(cr), state_by_cols(ci), power_cols(f_re), power_cols(f_im))
    return w_in_state, w_toep, w_out_state, a16


def _s5_expand_kernel(bd_ref, bbr_ref, bbi_ref, er_ref, ei_ref, cr_ref, ci_ref, fr_ref, fi_ref,
                      wt_ref, wi_ref, wo_ref):
    k = pl.program_id(1)
    half = STATE_COLS // 4
    for t in range(CHUNK):
        wt_ref[:, t * LANES:(t + 1) * LANES] = bd_ref[t - k + (CHUNK - 1)].astype(BF16)
    lane = lax.broadcasted_iota(jnp.int32, (STATE_COLS // 2, CHUNK), 1)
    fr = jnp.sum(jnp.where(lane == k, fr_ref[...], 0.0), axis=1, keepdims=True)
    fi = jnp.sum(jnp.where(lane == k, fi_ref[...], 0.0), axis=1, keepdims=True)
    for dr in range(2):
        sl = slice(dr * half, (dr + 1) * half)
        er = er_ref[pl.ds(k, 1), sl]
        ei = ei_ref[pl.ds(k, 1), sl]
        br = bbr_ref[:, sl]
        bi = bbi_ref[:, sl]
        wi_ref[:, 2 * dr * half:(2 * dr + 1) * half] = (br * er - bi * ei).astype(BF16)
        wi_ref[:, (2 * dr + 1) * half:(2 * dr + 2) * half] = (br * ei + bi * er).astype(BF16)
        cr = cr_ref[sl, :]
        ci = ci_ref[sl, :]
        wo_ref[2 * dr * half:(2 * dr + 1) * half, :] = (cr * fr[sl] - ci * fi[sl]).astype(BF16)
        wo_ref[(2 * dr + 1) * half:(2 * dr + 2) * half, :] = (-(cr * fi[sl] + ci * fr[sl])).astype(BF16)


def _s5_expand(bd, bbr, bbi, er, ei, cr, ci, fr, fi):
    nl = 2 * CHUNK - 1
    hs = STATE_COLS // 2
    slab = lambda shape: pl.BlockSpec((None,) + shape, lambda s, k: (s,) + (0,) * len(shape))
    big = jax.ShapeDtypeStruct((N_SLABS, SLAB_K, SLAB_K), BF16)
    return pl.pallas_call(
        _s5_expand_kernel,
        grid=(N_SLABS, CHUNK),
        in_specs=[slab((nl, LANES, LANES)), slab((LANES, hs)), slab((LANES, hs)), slab((CHUNK, hs)),
                  slab((CHUNK, hs)), slab((hs, LANES)), slab((hs, LANES)), slab((hs, CHUNK)), slab((hs, CHUNK))],
        out_specs=[pl.BlockSpec((None, LANES, SLAB_K), lambda s, k: (s, k, 0)),
                   pl.BlockSpec((None, LANES, STATE_COLS), lambda s, k: (s, k, 0)),
                   pl.BlockSpec((None, STATE_COLS, LANES), lambda s, k: (s, 0, k))],
        out_shape=[big, big, big],
        compiler_params=_params(("parallel", "parallel")),
        name="s5_expand",
    )(bd, bbr, bbi, er, ei, cr, ci, fr, fi)


def _attn_kernel(sink_ref, q_ref, kl_ref, kc_ref, kr_ref, bias_ref, o_ref):
    j = pl.program_id(1)
    nb = pl.num_programs(1)
    nk = 3 * BLOCK
    kv = jnp.concatenate([kl_ref[...], kc_ref[...], kr_ref[...]], axis=0)
    col = lax.broadcasted_iota(jnp.int32, (1, 2 * nk), 1) % nk
    valid = ((col >= BLOCK) | (j > 0)) & ((col < 2 * BLOCK) | (j < nb - 1))
    lane = lax.broadcasted_iota(jnp.int32, (nk, LANES), 1)
    low = lane < HEAD_DIM
    zero = jnp.zeros((nk, LANES), BF16)

    def block_diag(tile, kvh):
        other = pltpu.roll(tile, HEAD_DIM, 1)
        first, second = (tile, other) if kvh == 0 else (other, tile)
        return jnp.concatenate([jnp.where(low, first, zero), jnp.where(low, zero, second)], axis=0)

    top = lax.broadcasted_iota(jnp.int32, (2 * nk, LANES), 0) < nk
    ones_bd = (top == (lax.broadcasted_iota(jnp.int32, (2 * nk, LANES), 1) < HEAD_DIM)).astype(BF16)

    for kvh in range(N_KV_HEADS):
        kbd = block_diag(kv[:, :LANES], kvh)
        vbd = jnp.concatenate([block_diag(kv[:, LANES:], kvh), ones_bd], axis=1)
        for pair in range(Q_PER_KV // 2):
            pi = kvh * (Q_PER_KV // 2) + pair
            qp = q_ref[:, pi * LANES:(pi + 1) * LANES]
            s = lax.dot_general(qp, kbd, (((1,), (1,)), ((), ())), preferred_element_type=F32)
            s = jnp.where(valid, s + bias_ref[pi], NEG_INF)
            ps, sinks = [], []
            for e in range(2):
                se = s[:, e * nk:(e + 1) * nk]
                sk = sink_ref[2 * pi + e]
                mx = jnp.maximum(jnp.max(se, axis=-1, keepdims=True), sk)
                ps.append(jnp.exp2(se - mx).astype(BF16))
                sinks.append(jnp.broadcast_to(jnp.exp2(sk - mx), (BLOCK, HEAD_DIM)))
            nd = _dot(jnp.concatenate(ps, axis=1), vbd)
            o = nd[:, :LANES] / (nd[:, LANES:] + jnp.concatenate(sinks, axis=1))
            o_ref[:, pi * LANES:(pi + 1) * LANES] = o.astype(BF16)


def _attention(q, kv, sink, bias, bsz, seq):
    nb = seq // BLOCK
    row = lambda b, j, *_: (b * nb + j, 0)
    left = lambda b, j, *_: (b * nb + jnp.maximum(j - 1, 0), 0)
    right = lambda b, j, *_: (b * nb + jnp.minimum(j + 1, nb - 1), 0)
    grid_spec = pltpu.PrefetchScalarGridSpec(
        num_scalar_prefetch=0,
        grid=(bsz, nb),
        in_specs=[pl.BlockSpec(memory_space=pltpu.SMEM),
                  pl.BlockSpec((BLOCK, ATT_WIDTH), row),
                  pl.BlockSpec((BLOCK, 2 * KV_WIDTH), left),
                  pl.BlockSpec((BLOCK, 2 * KV_WIDTH), row),
                  pl.BlockSpec((BLOCK, 2 * KV_WIDTH), right),
                  _const_spec((N_Q_HEADS // 2, BLOCK, 6 * BLOCK))],
        out_specs=pl.BlockSpec((BLOCK, ATT_WIDTH), row),
    )
    return pl.pallas_call(
        _attn_kernel,
        grid_spec=grid_spec,
        out_shape=jax.ShapeDtypeStruct((bsz * seq, ATT_WIDTH), BF16),
        compiler_params=_params(("parallel", "parallel")),
        name="attention",
    )(sink, q, kv, kv, kv, bias)


def _t5_bucket(rel):
    half = NUM_BUCKETS // 2
    max_exact = half // 2
    ret = jnp.where(rel > 0, half, 0)
    n = jnp.abs(rel)
    nf = jnp.maximum(n, 1).astype(jnp.float32)
    large = max_exact + (jnp.log(nf / max_exact) / math.log(MAX_DISTANCE / max_exact)
                         * (half - max_exact)).astype(jnp.int32)
    large = jnp.minimum(large, half - 1)
    return ret + jnp.where(n < max_exact, n, large)


def _band_bias(rel_bias):
    q_loc = jnp.arange(BLOCK, dtype=jnp.int32)
    k_loc = jnp.arange(3 * BLOCK, dtype=jnp.int32)
    rel = (k_loc[None, :] - BLOCK) - q_loc[:, None]
    onehot = (_t5_bucket(rel)[None] == jnp.arange(NUM_BUCKETS, dtype=jnp.int32)[:, None, None]).astype(F32)
    bias = jnp.einsum('bh,bqk->hqk', rel_bias.astype(F32), onehot, precision=lax.Precision.HIGHEST)
    bias = jnp.where((jnp.abs(rel) <= WINDOW)[None], bias * LOG2E, NEG_INF)
    bias = bias.reshape(N_Q_HEADS // 2, 2, BLOCK, 3 * BLOCK).transpose(0, 2, 1, 3)
    return bias.reshape(N_Q_HEADS // 2, BLOCK, 6 * BLOCK)


def _merge_kernel(h_ref, y_ref, yb_ref, g1_ref, wg_ref, wglu_ref, wa_ref, wb_ref, wo_ref, o_ref):
    h = h_ref[...]
    hn = _rms(h, g1_ref[...]).astype(BF16)
    gates = _sigmoid(_dot(hn, wg_ref[...]))
    z = _gelu_tanh(y_ref[...])
    za = (z * _sigmoid(_dot(z.astype(BF16), wglu_ref[...]))).astype(BF16)
    merged = (gates[:, :D_MODEL] * _dot(za, wa_ref[...])
              + gates[:, D_MODEL:] * _dot(yb_ref[...], wb_ref[...]))
    o_ref[...] = h + _dot(merged.astype(BF16), wo_ref[...])


def _merge(h, y, yb, g1, w_gates, w_glu, w_a, w_b, w_o, tm=512):
    t = h.shape[0]
    return pl.pallas_call(
        _merge_kernel,
        grid=(t // tm,),
        in_specs=[pl.BlockSpec((tm, D_MODEL), lambda i: (i, 0)),
                  pl.BlockSpec((tm, S5_WIDTH), lambda i: (i, 0)),
                  pl.BlockSpec((tm, ATT_WIDTH), lambda i: (i, 0)),
                  _const_spec((1, D_MODEL)),
                  _const_spec(w_gates.shape), _const_spec(w_glu.shape), _const_spec(w_a.shape),
                  _const_spec(w_b.shape), _const_spec(w_o.shape)],
        out_specs=pl.BlockSpec((tm, D_MODEL), lambda i: (i, 0)),
        out_shape=jax.ShapeDtypeStruct((t, D_MODEL), F32),
        compiler_params=_params(("parallel",)),
        name="merge",
    )(h, y, yb, g1, w_gates, w_glu, w_a, w_b, w_o)


def _ffn_kernel(h_ref, g2_ref, gf_ref, wg_ref, wu_ref, wd_ref, o_ref, acc_ref, *, final_norm):
    h = h_ref[...]
    hn = _rms(h, g2_ref[...]).astype(BF16)
    acc_ref[...] = h
    for j in range(D_FF // FF_TILE):
        cols = slice(j * FF_TILE, (j + 1) * FF_TILE)
        gate = _dot(hn, wg_ref[:, cols])
        up = _dot(hn, wu_ref[:, cols])
        act = (gate * _sigmoid(gate) * up).astype(BF16)
        acc_ref[...] += _dot(act, wd_ref[cols, :])
    out = acc_ref[...]
    o_ref[...] = _rms(out, gf_ref[...]) if final_norm else out


def _ffn(h, g2, gf, w_gate, w_up, w_down, final_norm, tm=512):
    t = h.shape[0]
    return pl.pallas_call(
        functools.partial(_ffn_kernel, final_norm=final_norm),
        grid=(t // tm,),
        in_specs=[pl.BlockSpec((tm, D_MODEL), lambda i: (i, 0)),
                  _const_spec((1, D_MODEL)), _const_spec((1, D_MODEL)),
                  _const_spec(w_gate.shape), _const_spec(w_up.shape), _const_spec(w_down.shape)],
        out_specs=pl.BlockSpec((tm, D_MODEL), lambda i: (i, 0)),
        out_shape=jax.ShapeDtypeStruct((t, D_MODEL), F32),
        scratch_shapes=[pltpu.VMEM((tm, D_MODEL), F32)],
        compiler_params=_params(("parallel",)),
        name="ffn",
    )(h, g2, gf, w_gate, w_up, w_down)


def kernel(x, norm1_g, norm2_g, final_g, w_in, s5_lambda_re, s5_lambda_im, s5_log_dt, s5_b_re, s5_b_im,
           s5_c_re, s5_c_im, s5_d, s5_w_glu, attn_sink, rel_bias, w_branch_a, w_branch_b, w_out,
           ffn_w_gate, ffn_w_up, ffn_w_down):
    bsz, seq, _ = x.shape
    depth = w_in.shape[0]
    assert seq % BLOCK == 0 and seq % CHUNK == 0 and bsz == SUBLANES
    n_chunks = seq // CHUNK
    t = bsz * seq
    bias = _band_bias(rel_bias)
    o_k = S5_WIDTH + ATT_WIDTH
    o_g = o_k + 2 * KV_WIDTH
    col_scale = jnp.concatenate([jnp.ones((S5_WIDTH,), F32), jnp.full((ATT_WIDTH,), LOG2E * HEAD_DIM ** -0.5, F32),
                                 jnp.ones((2 * KV_WIDTH,), F32)])
    gf = final_g.reshape(1, D_MODEL).astype(F32)
    h = x.reshape(t, D_MODEL)
    for layer in range(depth):
        g1 = norm1_g[layer].reshape(1, D_MODEL).astype(F32)
        g2 = norm2_g[layer].reshape(1, D_MODEL).astype(F32)
        w_uqkv = (w_in[layer][:, :o_g] * col_scale).astype(BF16)
        w_gates = w_in[layer][:, o_g:].astype(BF16)
        u, q, kv = _inproj(h, g1, w_uqkv)
        w_in_state, w_toep, w_out_state, a16 = _s5_weights(
            s5_lambda_re[layer], s5_lambda_im[layer], s5_log_dt[layer], s5_b_re[layer], s5_b_im[layer],
            s5_c_re[layer], s5_c_im[layer], s5_d[layer])
        u3 = u.reshape(bsz, seq, S5_WIDTH)
        sin = _s5_state_in(u3, w_in_state)
        carry = _s5_scan(sin, a16, n_chunks)
        y = _s5_out(u3, carry, w_toep, w_out_state).reshape(t, S5_WIDTH)
        yb = _attention(q, kv, attn_sink[layer].astype(F32) * LOG2E, bias, bsz, seq)
        h = _merge(h, y, yb, g1, w_gates, s5_w_glu[layer].astype(BF16), w_branch_a[layer].astype(BF16),
                   w_branch_b[layer].astype(BF16), w_out[layer].astype(BF16))
        h = _ffn(h, g2, gf, ffn_w_gate[layer].astype(BF16), ffn_w_up[layer].astype(BF16),
                 ffn_w_down[layer].astype(BF16), final_norm=(layer == depth - 1))
    return h.reshape(bsz, seq, D_MODEL)
```

```python
import functools
import math

import jax
import jax.numpy as jnp
from jax import lax
from jax.experimental import pallas as pl
from jax.experimental.pallas import tpu as pltpu

F32 = jnp.float32
BF16 = jnp.bfloat16

D_MODEL = 1024
S5_WIDTH = 512
S5_GROUP = 16
S5_GROUPS = 32
S5_STATE = 64
HEAD_DIM = 64
N_Q_HEADS = 8
N_KV_HEADS = 2
Q_PER_KV = N_Q_HEADS // N_KV_HEADS
ATT_WIDTH = N_Q_HEADS * HEAD_DIM
KV_WIDTH = N_KV_HEADS * HEAD_DIM
WINDOW = 128
BLOCK = 128
NUM_BUCKETS = 32
MAX_DISTANCE = 128
D_FF = 2816
RMS_EPS = 1e-6
NEG_INF = -1e30
LOG2E = math.log2(math.e)

LANES = 128
SUBLANES = 8
CHUNK = 16
GROUPS_PER_SLAB = LANES // S5_GROUP
N_SLABS = S5_WIDTH // LANES
PAIRS_PER_SLAB = GROUPS_PER_SLAB // 2
GROUP_K = CHUNK * S5_GROUP
STATE_COLS = 2 * 2 * GROUPS_PER_SLAB * S5_STATE
SCAN_TILE = STATE_COLS // 2
TOK_TILE = 64
TILE_ROWS = SUBLANES * TOK_TILE
S5_ROWS = 8 * TILE_ROWS
FF_TILE = 256
VMEM_LIMIT = 56 * 1024 * 1024


def _rms(x, g):
    return x * lax.rsqrt(jnp.mean(x * x, axis=-1, keepdims=True) + RMS_EPS) * g


def _gelu_tanh(x):
    return 0.5 * x * (1.0 + jnp.tanh(math.sqrt(2.0 / math.pi) * (x + 0.044715 * (x * x * x))))


def _sigmoid(x):
    return 1.0 / (1.0 + jnp.exp(-x))


def _dot(a, b):
    return jnp.dot(a, b, preferred_element_type=F32)


def _const_spec(shape):
    nd = len(shape)
    return pl.BlockSpec(shape, lambda *_: (0,) * nd, pipeline_mode=pl.Buffered(1))


def _params(sem):
    return pltpu.CompilerParams(dimension_semantics=sem, vmem_limit_bytes=VMEM_LIMIT)


def _to_chunk_order():
    cpt = TOK_TILE // CHUNK
    dst = jnp.arange(TILE_ROWS)
    t, c, b = dst // (cpt * SUBLANES), (dst // SUBLANES) % cpt, dst % SUBLANES
    src = b * TOK_TILE + c * CHUNK + t
    return (src[:, None] == jnp.arange(TILE_ROWS)[None, :]).astype(BF16)


def _inproj_kernel(x_ref, g_ref, w_ref, perm_ref, u_ref, q_ref, kv_ref):
    bsz = x_ref.shape[0]
    hn = _rms(x_ref[...].reshape(TILE_ROWS, D_MODEL), g_ref[...]).astype(BF16)
    r = _dot(hn, w_ref[...])
    u_ref[...] = _dot(perm_ref[...], r[:, :S5_WIDTH].astype(BF16)).astype(BF16)
    q_ref[...] = r[:, S5_WIDTH:S5_WIDTH + ATT_WIDTH].astype(BF16).reshape(bsz, TOK_TILE, ATT_WIDTH)
    kv_ref[...] = r[:, S5_WIDTH + ATT_WIDTH:].astype(BF16).reshape(bsz, TOK_TILE, 2 * KV_WIDTH)


def _inproj(h3, g, w, perm):
    bsz, seq, _ = h3.shape
    assert bsz * TOK_TILE == TILE_ROWS
    n = w.shape[1]
    tok = lambda width: pl.BlockSpec((bsz, TOK_TILE, width), lambda i: (0, i, 0))
    return pl.pallas_call(
        _inproj_kernel,
        grid=(seq // TOK_TILE,),
        in_specs=[tok(D_MODEL), _const_spec((1, D_MODEL)), _const_spec((D_MODEL, n)),
                  _const_spec((TILE_ROWS, TILE_ROWS))],
        out_specs=[pl.BlockSpec((TILE_ROWS, S5_WIDTH), lambda i: (i, 0)), tok(ATT_WIDTH), tok(2 * KV_WIDTH)],
        out_shape=[jax.ShapeDtypeStruct((bsz * seq, S5_WIDTH), BF16),
                   jax.ShapeDtypeStruct((bsz, seq, ATT_WIDTH), BF16),
                   jax.ShapeDtypeStruct((bsz, seq, 2 * KV_WIDTH), BF16)],
        compiler_params=_params(("parallel",)),
        name="inproj",
    )(h3, g, w, perm)


def _block_transpose(x):
    x = list(x)
    blk = lax.broadcasted_iota(jnp.int32, x[0].shape, 1) // S5_GROUP
    for d in (4, 2, 1):
        keep = (blk & d) == 0
        for i in range(GROUPS_PER_SLAB):
            if i & d:
                continue
            xi, xj = x[i], x[i + d]
            x[i] = jnp.where(keep, xi, pltpu.roll(xj, d * S5_GROUP, 1))
            x[i + d] = jnp.where(keep, pltpu.roll(xi, LANES - d * S5_GROUP, 1), xj)
    return x


def _token_rows(t, k):
    return pl.ds(k * TILE_ROWS + t * (TILE_ROWS // CHUNK), TILE_ROWS // CHUNK)


def _group_rows(u_ref):
    tiles = u_ref.shape[0] // TILE_ROWS
    tok = [jnp.concatenate([u_ref[_token_rows(t, k), :] for k in range(tiles)], axis=0) for t in range(CHUNK)]
    lo = _block_transpose(tok[:CHUNK // 2])
    hi = _block_transpose(tok[CHUNK // 2:])
    return [jnp.concatenate([lo[a], hi[a]], axis=1) for a in range(GROUPS_PER_SLAB)]


def _state_tiles(pq):
    return [pl.ds((dr * 2 + ri) * (STATE_COLS // 4) + pq * LANES, LANES) for dr in range(2) for ri in range(2)]


def _s5_in_kernel(u_ref, w_ref, o_ref):
    g = _group_rows(u_ref)
    for pq in range(PAIRS_PER_SLAB):
        r = _dot(jnp.concatenate([g[2 * pq], g[2 * pq + 1]], axis=1), w_ref[pq])
        for k, tile in enumerate(_state_tiles(pq)):
            o_ref[:, tile] = r[:, k * LANES:(k + 1) * LANES]


def _s5_state_in(u, w_in_state, rows=S5_ROWS):
    t = u.shape[0]
    return pl.pallas_call(
        _s5_in_kernel,
        grid=(N_SLABS, t // rows),
        in_specs=[pl.BlockSpec((rows, LANES), lambda s, i: (i, s)),
                  pl.BlockSpec((None,) + w_in_state.shape[1:], lambda s, i: (s, 0, 0, 0))],
        out_specs=pl.BlockSpec((rows // CHUNK, STATE_COLS), lambda s, i: (i, s)),
        out_shape=jax.ShapeDtypeStruct((t // CHUNK, N_SLABS * STATE_COLS), F32),
        compiler_params=_params(("parallel", "parallel")),
        name="s5_state_in",
    )(u, w_in_state)


def _s5_scan_kernel(sin_ref, a_ref, o_ref, st_ref, *, cpt):
    half = SCAN_TILE // 2
    backward = pl.program_id(1) == 1

    @pl.when(pl.program_id(2) == 0)
    def _():
        st_ref[...] = jnp.zeros_like(st_ref)

    ar = a_ref[:, :half]
    ai = a_ref[:, half:]

    def body(i, carry):
        sr, si = carry
        c = jnp.where(backward, cpt - 1 - i, i)
        rows = pl.ds(pl.multiple_of(c * SUBLANES, SUBLANES), SUBLANES)
        o_ref[rows, :half] = sr
        o_ref[rows, half:] = si
        xr = sin_ref[rows, :half]
        xi = sin_ref[rows, half:]
        return ar * sr - ai * si + xr, ar * si + ai * sr + xi

    sr, si = lax.fori_loop(0, cpt, body, (st_ref[:, :half], st_ref[:, half:]), unroll=4)
    st_ref[:, :half] = sr
    st_ref[:, half:] = si


def _s5_scan(sin, a16, n_chunks, cpt=64):
    nct = n_chunks // cpt
    blk = lambda s, d, k: (k + d * (nct - 1 - 2 * k), s * 2 + d)
    return pl.pallas_call(
        functools.partial(_s5_scan_kernel, cpt=cpt),
        grid=(N_SLABS, 2, nct),
        in_specs=[pl.BlockSpec((cpt * SUBLANES, SCAN_TILE), blk),
                  pl.BlockSpec((SUBLANES, SCAN_TILE), lambda s, d, k: (0, s * 2 + d))],
        out_specs=pl.BlockSpec((cpt * SUBLANES, SCAN_TILE), blk),
        out_shape=jax.ShapeDtypeStruct(sin.shape, F32),
        scratch_shapes=[pltpu.VMEM((SUBLANES, SCAN_TILE), F32)],
        compiler_params=_params(("parallel", "parallel", "arbitrary")),
        name="s5_scan",
    )(sin, a16)


def _s5_out_kernel(u_ref, c_ref, wt_ref, wo_ref, y_ref):
    g = _group_rows(u_ref)
    ys = []
    for pq in range(PAIRS_PER_SLAB):
        carry = jnp.concatenate([c_ref[:, tile] for tile in _state_tiles(pq)], axis=1).astype(BF16)
        from_state = _dot(carry, wo_ref[pq])
        for a2 in range(2):
            a = 2 * pq + a2
            ys.append(_dot(g[a], wt_ref[a]) + from_state[:, a2 * GROUP_K:(a2 + 1) * GROUP_K])
    halves = (_block_transpose([y[:, :LANES] for y in ys]), _block_transpose([y[:, LANES:] for y in ys]))
    per_tile = TILE_ROWS // CHUNK
    for t in range(CHUNK):
        tok = halves[t // (CHUNK // 2)][t % (CHUNK // 2)].astype(BF16)
        for k in range(y_ref.shape[0] // TILE_ROWS):
            y_ref[_token_rows(t, k), :] = tok[k * per_tile:(k + 1) * per_tile, :]


def _s5_out(u, carry, w_toep, w_out_state, rows=S5_ROWS):
    t = u.shape[0]
    return pl.pallas_call(
        _s5_out_kernel,
        grid=(N_SLABS, t // rows),
        in_specs=[pl.BlockSpec((rows, LANES), lambda s, i: (i, s)),
                  pl.BlockSpec((rows // CHUNK, STATE_COLS), lambda s, i: (i, s)),
                  pl.BlockSpec((None,) + w_toep.shape[1:], lambda s, i: (s, 0, 0, 0)),
                  pl.BlockSpec((None,) + w_out_state.shape[1:], lambda s, i: (s, 0, 0, 0))],
        out_specs=pl.BlockSpec((rows, LANES), lambda s, i: (i, s)),
        out_shape=jax.ShapeDtypeStruct((t, S5_WIDTH), BF16),
        compiler_params=_params(("parallel", "parallel")),
        name="s5_out",
    )(u, carry, w_toep, w_out_state)


def _s5_weights(lam_re, lam_im, log_dt, b_re, b_im, c_re, c_im, d):
    G, P, H, C, S = S5_GROUPS, S5_STATE, S5_GROUP, CHUNK, GROUPS_PER_SLAB
    lr = lam_re.astype(F32)
    li = lam_im.astype(F32)
    dt = jnp.exp(log_dt.astype(F32))[..., None]
    mag = jnp.exp(lr * dt)
    ab_re = mag * jnp.cos(li * dt)
    ab_im = mag * jnp.sin(li * dt)
    nr = ab_re - 1.0
    den = lr * lr + li * li
    coef_re = (nr * lr + ab_im * li) / den
    coef_im = (ab_im * lr - nr * li) / den
    br = b_re.astype(F32)
    bi = b_im.astype(F32)
    bb_re = coef_re[..., None] * br - coef_im[..., None] * bi
    bb_im = coef_re[..., None] * bi + coef_im[..., None] * br
    cr = c_re.astype(F32)
    ci = c_im.astype(F32)

    k = jnp.arange(C + 1, dtype=F32)[:, None, None, None]
    pmag = jnp.exp(k * (lr * dt))
    pw_re = pmag * jnp.cos(k * (li * dt))
    pw_im = pmag * jnp.sin(k * (li * dt))

    cpw_re = cr[None] * pw_re[:C, :, :, None, :] - ci[None] * pw_im[:C, :, :, None, :]
    cpw_im = cr[None] * pw_im[:C, :, :, None, :] + ci[None] * pw_re[:C, :, :, None, :]
    bbt_re = bb_re.transpose(0, 1, 3, 2)[None, :, :, :, None, :]
    bbt_im = bb_im.transpose(0, 1, 3, 2)[None, :, :, :, None, :]
    klag_t = jnp.sum(cpw_re[:, :, :, None] * bbt_re - cpw_im[:, :, :, None] * bbt_im, axis=-1)
    klag_t = klag_t.transpose(1, 0, 2, 3, 4)
    centre = klag_t[0, 0] + klag_t[1, 0] + jnp.eye(H, dtype=F32)[None] * d.astype(F32).reshape(G, 1, H)
    lags = jnp.concatenate([klag_t[1, C - 1:0:-1], centre[None], klag_t[0, 1:]], axis=0)
    lag_of = jnp.arange(C)[None, :] - jnp.arange(C)[:, None] + (C - 1)
    w_toep = lags[lag_of].transpose(2, 0, 3, 1, 4)
    w_toep = w_toep.reshape(N_SLABS, S, GROUP_K, GROUP_K).astype(BF16)

    eye2 = jnp.eye(2, dtype=F32)
    n_pairs = G // 2
    e_re = jnp.stack([pw_re[C - 1::-1, 0], pw_re[:C, 1]], axis=0)
    e_im = jnp.stack([pw_im[C - 1::-1, 0], pw_im[:C, 1]], axis=0)
    vin_re = e_re[..., None] * bb_re[:, None] - e_im[..., None] * bb_im[:, None]
    vin_im = e_re[..., None] * bb_im[:, None] + e_im[..., None] * bb_re[:, None]
    vin = jnp.stack([vin_re, vin_im], axis=0).transpose(3, 2, 5, 1, 0, 4)
    vin = vin.reshape(n_pairs, 2, C, H, 2, 2, 1, P) * eye2[None, :, None, None, None, None, :, None]
    w_in_state = vin.reshape(N_SLABS, PAIRS_PER_SLAB, 2 * GROUP_K, 4 * LANES).astype(BF16)

    f_re = jnp.stack([pw_re[1:, 0], pw_re[C:0:-1, 1]], axis=0)
    f_im = jnp.stack([pw_im[1:, 0], pw_im[C:0:-1, 1]], axis=0)
    wo_re = cr[:, None] * f_re[:, :, :, None, :] - ci[:, None] * f_im[:, :, :, None, :]
    wo_im = cr[:, None] * f_im[:, :, :, None, :] + ci[:, None] * f_re[:, :, :, None, :]
    wo = jnp.stack([wo_re, -wo_im], axis=0).transpose(3, 1, 0, 5, 2, 4)
    wo = wo.reshape(n_pairs, 2, 2, 2, P, C, H).transpose(0, 2, 3, 1, 4, 5, 6)
    wo = wo[:, :, :, :, :, None] * eye2[None, None, None, :, None, :, None, None]
    w_out_state = wo.reshape(N_SLABS, PAIRS_PER_SLAB, 4 * LANES, 2 * GROUP_K).astype(BF16)

    a16 = jnp.stack([pw_re[C], pw_im[C]], axis=0)
    a16 = a16.reshape(2, 2, N_SLABS, S, P).transpose(2, 1, 0, 3, 4)
    a16 = jnp.broadcast_to(a16.reshape(1, N_SLABS * STATE_COLS), (SUBLANES, N_SLABS * STATE_COLS))
    return w_in_state, w_toep, w_out_state, a16


def _attn_kernel(sink_ref, q_ref, kl_ref, kc_ref, kr_ref, bias_ref, o_ref):
    j = pl.program_id(1)
    nb = pl.num_programs(1)
    nk = 3 * BLOCK
    kv = jnp.concatenate([kl_ref[...], kc_ref[...], kr_ref[...]], axis=0)
    col = lax.broadcasted_iota(jnp.int32, (1, 2 * nk), 1) % nk
    valid = ((col >= BLOCK) | (j > 0)) & ((col < 2 * BLOCK) | (j < nb - 1))
    lane = lax.broadcasted_iota(jnp.int32, (nk, LANES), 1)
    low = lane < HEAD_DIM
    zero = jnp.zeros((nk, LANES), BF16)

    def block_diag(tile, kvh):
        other = pltpu.roll(tile, HEAD_DIM, 1)
        first, second = (tile, other) if kvh == 0 else (other, tile)
        return jnp.concatenate([jnp.where(low, first, zero), jnp.where(low, zero, second)], axis=0)

    top = lax.broadcasted_iota(jnp.int32, (2 * nk, LANES), 0) < nk
    ones_bd = (top == (lax.broadcasted_iota(jnp.int32, (2 * nk, LANES), 1) < HEAD_DIM)).astype(BF16)

    for kvh in range(N_KV_HEADS):
        kbd = block_diag(kv[:, :LANES], kvh)
        vbd = jnp.concatenate([block_diag(kv[:, LANES:], kvh), ones_bd], axis=1)
        for pair in range(Q_PER_KV // 2):
            pi = kvh * (Q_PER_KV // 2) + pair
            qp = q_ref[:, pi * LANES:(pi + 1) * LANES]
            s = lax.dot_general(qp, kbd, (((1,), (1,)), ((), ())), preferred_element_type=F32)
            s = jnp.where(valid, s + bias_ref[pi], NEG_INF)
            ps, sinks = [], []
            for e in range(2):
                se = s[:, e * nk:(e + 1) * nk]
                sk = sink_ref[2 * pi + e]
                mx = jnp.maximum(jnp.max(se, axis=-1, keepdims=True), sk)
                ps.append(jnp.exp2(se - mx).astype(BF16))
                sinks.append(jnp.broadcast_to(jnp.exp2(sk - mx), (BLOCK, HEAD_DIM)))
            nd = _dot(jnp.concatenate(ps, axis=1), vbd)
            o = nd[:, :LANES] / (nd[:, LANES:] + jnp.concatenate(sinks, axis=1))
            o_ref[:, pi * LANES:(pi + 1) * LANES] = o.astype(BF16)


def _attention(q, kv, sink, bias, bsz, seq):
    nb = seq // BLOCK
    row = lambda b, j, *_: (b * nb + j, 0)
    left = lambda b, j, *_: (b * nb + jnp.maximum(j - 1, 0), 0)
    right = lambda b, j, *_: (b * nb + jnp.minimum(j + 1, nb - 1), 0)
    grid_spec = pltpu.PrefetchScalarGridSpec(
        num_scalar_prefetch=0,
        grid=(bsz, nb),
        in_specs=[pl.BlockSpec(memory_space=pltpu.SMEM),
                  pl.BlockSpec((BLOCK, ATT_WIDTH), row),
                  pl.BlockSpec((BLOCK, 2 * KV_WIDTH), left),
                  pl.BlockSpec((BLOCK, 2 * KV_WIDTH), row),
                  pl.BlockSpec((BLOCK, 2 * KV_WIDTH), right),
                  _const_spec((N_Q_HEADS // 2, BLOCK, 6 * BLOCK))],
        out_specs=pl.BlockSpec((BLOCK, ATT_WIDTH), row),
    )
    return pl.pallas_call(
        _attn_kernel,
        grid_spec=grid_spec,
        out_shape=jax.ShapeDtypeStruct((bsz * seq, ATT_WIDTH), BF16),
        compiler_params=_params(("parallel", "parallel")),
        name="attention",
    )(sink, q, kv, kv, kv, bias)


def _t5_bucket(rel):
    half = NUM_BUCKETS // 2
    max_exact = half // 2
    ret = jnp.where(rel > 0, half, 0)
    n = jnp.abs(rel)
    nf = jnp.maximum(n, 1).astype(jnp.float32)
    large = max_exact + (jnp.log(nf / max_exact) / math.log(MAX_DISTANCE / max_exact)
                         * (half - max_exact)).astype(jnp.int32)
    large = jnp.minimum(large, half - 1)
    return ret + jnp.where(n < max_exact, n, large)


def _band_bias(rel_bias):
    q_loc = jnp.arange(BLOCK, dtype=jnp.int32)
    k_loc = jnp.arange(3 * BLOCK, dtype=jnp.int32)
    rel = (k_loc[None, :] - BLOCK) - q_loc[:, None]
    onehot = (_t5_bucket(rel)[None] == jnp.arange(NUM_BUCKETS, dtype=jnp.int32)[:, None, None]).astype(F32)
    bias = jnp.einsum('bh,bqk->hqk', rel_bias.astype(F32), onehot, precision=lax.Precision.HIGHEST)
    bias = jnp.where((jnp.abs(rel) <= WINDOW)[None], bias * LOG2E, NEG_INF)
    bias = bias.reshape(N_Q_HEADS // 2, 2, BLOCK, 3 * BLOCK).transpose(0, 2, 1, 3)
    return bias.reshape(N_Q_HEADS // 2, BLOCK, 6 * BLOCK)


def _merge_kernel(h_ref, y_ref, yb_ref, g1_ref, unperm_ref, wg_ref, wglu_ref, wa_ref, wb_ref, wo_ref, o_ref):
    bsz = h_ref.shape[0]
    h = h_ref[...].reshape(TILE_ROWS, D_MODEL)
    hn = _rms(h, g1_ref[...]).astype(BF16)
    gates = _sigmoid(_dot(hn, wg_ref[...]))
    z = _gelu_tanh(_dot(unperm_ref[...], y_ref[...]))
    za = (z * _sigmoid(_dot(z.astype(BF16), wglu_ref[...]))).astype(BF16)
    yb = yb_ref[...].reshape(TILE_ROWS, ATT_WIDTH)
    merged = gates[:, :D_MODEL] * _dot(za, wa_ref[...]) + gates[:, D_MODEL:] * _dot(yb, wb_ref[...])
    o_ref[...] = (h + _dot(merged.astype(BF16), wo_ref[...])).reshape(bsz, TOK_TILE, D_MODEL)


def _merge(h3, y, yb3, g1, unperm, w_gates, w_glu, w_a, w_b, w_o):
    bsz, seq, _ = h3.shape
    tok = lambda width: pl.BlockSpec((bsz, TOK_TILE, width), lambda i: (0, i, 0))
    return pl.pallas_call(
        _merge_kernel,
        grid=(seq // TOK_TILE,),
        in_specs=[tok(D_MODEL), pl.BlockSpec((TILE_ROWS, S5_WIDTH), lambda i: (i, 0)), tok(ATT_WIDTH),
                  _const_spec((1, D_MODEL)), _const_spec(unperm.shape),
                  _const_spec(w_gates.shape), _const_spec(w_glu.shape), _const_spec(w_a.shape),
                  _const_spec(w_b.shape), _const_spec(w_o.shape)],
        out_specs=tok(D_MODEL),
        out_shape=jax.ShapeDtypeStruct((bsz, seq, D_MODEL), F32),
        compiler_params=_params(("parallel",)),
        name="merge",
    )(h3, y, yb3, g1, unperm, w_gates, w_glu, w_a, w_b, w_o)


def _ffn_kernel(h_ref, g2_ref, gf_ref, wg_ref, wu_ref, wd_ref, o_ref, acc_ref, *, final_norm):
    h = h_ref[...]
    hn = _rms(h, g2_ref[...]).astype(BF16)
    acc_ref[...] = h
    for j in range(D_FF // FF_TILE):
        cols = slice(j * FF_TILE, (j + 1) * FF_TILE)
        gate = _dot(hn, wg_ref[:, cols])
        up = _dot(hn, wu_ref[:, cols])
        act = (gate * _sigmoid(gate) * up).astype(BF16)
        acc_ref[...] += _dot(act, wd_ref[cols, :])
    out = acc_ref[...]
    o_ref[...] = _rms(out, gf_ref[...]) if final_norm else out


def _ffn(h, g2, gf, w_gate, w_up, w_down, final_norm, tm=512):
    t = h.shape[0]
    return pl.pallas_call(
        functools.partial(_ffn_kernel, final_norm=final_norm),
        grid=(t // tm,),
        in_specs=[pl.BlockSpec((tm, D_MODEL), lambda i: (i, 0)),
                  _const_spec((1, D_MODEL)), _const_spec((1, D_MODEL)),
                  _const_spec(w_gate.shape), _const_spec(w_up.shape), _const_spec(w_down.shape)],
        out_specs=pl.BlockSpec((tm, D_MODEL), lambda i: (i, 0)),
        out_shape=jax.ShapeDtypeStruct((t, D_MODEL), F32),
        scratch_shapes=[pltpu.VMEM((tm, D_MODEL), F32)],
        compiler_params=_params(("parallel",)),
        name="ffn",
    )(h, g2, gf, w_gate, w_up, w_down)


def kernel(x, norm1_g, norm2_g, final_g, w_in, s5_lambda_re, s5_lambda_im, s5_log_dt, s5_b_re, s5_b_im,
           s5_c_re, s5_c_im, s5_d, s5_w_glu, attn_sink, rel_bias, w_branch_a, w_branch_b, w_out,
           ffn_w_gate, ffn_w_up, ffn_w_down):
    bsz, seq, _ = x.shape
    depth = w_in.shape[0]
    assert seq % BLOCK == 0 and seq % CHUNK == 0 and bsz == SUBLANES
    n_chunks = seq // CHUNK
    t = bsz * seq
    bias = _band_bias(rel_bias)
    o_k = S5_WIDTH + ATT_WIDTH
    o_g = o_k + 2 * KV_WIDTH
    col_scale = jnp.concatenate([jnp.ones((S5_WIDTH,), F32), jnp.full((ATT_WIDTH,), LOG2E * HEAD_DIM ** -0.5, F32),
                                 jnp.ones((2 * KV_WIDTH,), F32)])
    gf = final_g.reshape(1, D_MODEL).astype(F32)
    perm = _to_chunk_order()
    unperm = perm.T
    h = x
    for layer in range(depth):
        g1 = norm1_g[layer].reshape(1, D_MODEL).astype(F32)
        g2 = norm2_g[layer].reshape(1, D_MODEL).astype(F32)
        w_uqkv = (w_in[layer][:, :o_g] * col_scale).astype(BF16)
        w_gates = w_in[layer][:, o_g:].astype(BF16)
        u, q, kv = _inproj(h, g1, w_uqkv, perm)
        w_in_state, w_toep, w_out_state, a16 = _s5_weights(
            s5_lambda_re[layer], s5_lambda_im[layer], s5_log_dt[layer], s5_b_re[layer], s5_b_im[layer],
            s5_c_re[layer], s5_c_im[layer], s5_d[layer])
        sin = _s5_state_in(u, w_in_state)
        carry = _s5_scan(sin, a16, n_chunks)
        y = _s5_out(u, carry, w_toep, w_out_state)
        yb = _attention(q.reshape(t, ATT_WIDTH), kv.reshape(t, 2 * KV_WIDTH),
                        attn_sink[layer].astype(F32) * LOG2E, bias, bsz, seq)
        h = _merge(h, y, yb.reshape(bsz, seq, ATT_WIDTH), g1, unperm, w_gates, s5_w_glu[layer].astype(BF16),
                   w_branch_a[layer].astype(BF16), w_branch_b[layer].astype(BF16), w_out[layer].astype(BF16))
        h = _ffn(h.reshape(t, D_MODEL), g2, gf, ffn_w_gate[layer].astype(BF16), ffn_w_up[layer].astype(BF16),
                 ffn_w_down[layer].astype(BF16), final_norm=(layer == depth - 1)).reshape(bsz, seq, D_MODEL)
    return h
```

```python
import functools
import math

import jax
import jax.numpy as jnp
from jax import lax
from jax.experimental import pallas as pl
from jax.experimental.pallas import tpu as pltpu

F32 = jnp.float32
BF16 = jnp.bfloat16

D_MODEL = 1024
S5_WIDTH = 512
S5_GROUP = 16
S5_GROUPS = 32
S5_STATE = 64
HEAD_DIM = 64
N_Q_HEADS = 8
N_KV_HEADS = 2
Q_PER_KV = N_Q_HEADS // N_KV_HEADS
ATT_WIDTH = N_Q_HEADS * HEAD_DIM
KV_WIDTH = N_KV_HEADS * HEAD_DIM
WINDOW = 128
BLOCK = 128
NUM_BUCKETS = 32
MAX_DISTANCE = 128
D_FF = 2816
RMS_EPS = 1e-6
NEG_INF = -1e30
LOG2E = math.log2(math.e)

LANES = 128
SUBLANES = 8
CHUNK = 16
GROUPS_PER_SLAB = LANES // S5_GROUP
N_SLABS = S5_WIDTH // LANES
PAIRS_PER_SLAB = GROUPS_PER_SLAB // 2
GROUP_K = CHUNK * S5_GROUP
STATE_COLS = 2 * 2 * GROUPS_PER_SLAB * S5_STATE
SCAN_TILE = STATE_COLS // 2
TOK_TILE = 64
TILE_ROWS = SUBLANES * TOK_TILE
S5_ROWS = 8 * TILE_ROWS
FF_TILE = 256
VMEM_LIMIT = 56 * 1024 * 1024


def _rms(x, g):
    return x * lax.rsqrt(jnp.mean(x * x, axis=-1, keepdims=True) + RMS_EPS) * g


def _gelu_tanh(x):
    return 0.5 * x * (1.0 + jnp.tanh(math.sqrt(2.0 / math.pi) * (x + 0.044715 * (x * x * x))))


def _sigmoid(x):
    return 1.0 / (1.0 + jnp.exp(-x))


def _dot(a, b):
    return jnp.dot(a, b, preferred_element_type=F32)


def _const_spec(shape):
    nd = len(shape)
    return pl.BlockSpec(shape, lambda *_: (0,) * nd, pipeline_mode=pl.Buffered(1))


def _params(sem):
    return pltpu.CompilerParams(dimension_semantics=sem, vmem_limit_bytes=VMEM_LIMIT)


def _to_chunk_order():
    cpt = TOK_TILE // CHUNK
    dst = jnp.arange(TILE_ROWS)
    t, c, b = dst // (cpt * SUBLANES), (dst // SUBLANES) % cpt, dst % SUBLANES
    src = b * TOK_TILE + c * CHUNK + t
    return (src[:, None] == jnp.arange(TILE_ROWS)[None, :]).astype(BF16)


def _inproj_kernel(x_ref, g_ref, w_ref, perm_ref, u_ref, q_ref, kv_ref):
    bsz = x_ref.shape[0]
    hn = _rms(x_ref[...].reshape(TILE_ROWS, D_MODEL), g_ref[...]).astype(BF16)
    r = _dot(hn, w_ref[...])
    u_ref[...] = _dot(perm_ref[...], r[:, :S5_WIDTH].astype(BF16)).astype(BF16)
    q_ref[...] = r[:, S5_WIDTH:S5_WIDTH + ATT_WIDTH].astype(BF16).reshape(bsz, TOK_TILE, ATT_WIDTH)
    kv_ref[...] = r[:, S5_WIDTH + ATT_WIDTH:].astype(BF16).reshape(bsz, TOK_TILE, 2 * KV_WIDTH)


def _inproj(h3, g, w, perm):
    bsz, seq, _ = h3.shape
    assert bsz * TOK_TILE == TILE_ROWS
    n = w.shape[1]
    tok = lambda width: pl.BlockSpec((bsz, TOK_TILE, width), lambda i: (0, i, 0))
    return pl.pallas_call(
        _inproj_kernel,
        grid=(seq // TOK_TILE,),
        in_specs=[tok(D_MODEL), _const_spec((1, D_MODEL)), _const_spec((D_MODEL, n)),
                  _const_spec((TILE_ROWS, TILE_ROWS))],
        out_specs=[pl.BlockSpec((TILE_ROWS, S5_WIDTH), lambda i: (i, 0)), tok(ATT_WIDTH), tok(2 * KV_WIDTH)],
        out_shape=[jax.ShapeDtypeStruct((bsz * seq, S5_WIDTH), BF16),
                   jax.ShapeDtypeStruct((bsz, seq, ATT_WIDTH), BF16),
                   jax.ShapeDtypeStruct((bsz, seq, 2 * KV_WIDTH), BF16)],
        compiler_params=_params(("parallel",)),
        name="inproj",
    )(h3, g, w, perm)


def _block_transpose(x):
    x = list(x)
    blk = lax.broadcasted_iota(jnp.int32, x[0].shape, 1) // S5_GROUP
    for d in (4, 2, 1):
        keep = (blk & d) == 0
        for i in range(GROUPS_PER_SLAB):
            if i & d:
                continue
            xi, xj = x[i], x[i + d]
            x[i] = jnp.where(keep, xi, pltpu.roll(xj, d * S5_GROUP, 1))
            x[i + d] = jnp.where(keep, pltpu.roll(xi, LANES - d * S5_GROUP, 1), xj)
    return x


def _token_rows(t, k):
    return pl.ds(k * TILE_ROWS + t * (TILE_ROWS // CHUNK), TILE_ROWS // CHUNK)


def _group_rows(u_ref):
    tiles = u_ref.shape[0] // TILE_ROWS
    tok = [jnp.concatenate([u_ref[_token_rows(t, k), :] for k in range(tiles)], axis=0) for t in range(CHUNK)]
    lo = _block_transpose(tok[:CHUNK // 2])
    hi = _block_transpose(tok[CHUNK // 2:])
    return [jnp.concatenate([lo[a], hi[a]], axis=1) for a in range(GROUPS_PER_SLAB)]


def _state_tiles(pq):
    return [pl.ds((dr * 2 + ri) * (STATE_COLS // 4) + pq * LANES, LANES) for dr in range(2) for ri in range(2)]


def _s5_in_kernel(u_ref, w_ref, o_ref):
    g = _group_rows(u_ref)
    for pq in range(PAIRS_PER_SLAB):
        r = _dot(jnp.concatenate([g[2 * pq], g[2 * pq + 1]], axis=1), w_ref[pq])
        for k, tile in enumerate(_state_tiles(pq)):
            o_ref[:, tile] = r[:, k * LANES:(k + 1) * LANES]


def _s5_state_in(u, w_in_state, rows=S5_ROWS):
    t = u.shape[0]
    return pl.pallas_call(
        _s5_in_kernel,
        grid=(N_SLABS, t // rows),
        in_specs=[pl.BlockSpec((rows, LANES), lambda s, i: (i, s)),
                  pl.BlockSpec((None,) + w_in_state.shape[1:], lambda s, i: (s, 0, 0, 0))],
        out_specs=pl.BlockSpec((rows // CHUNK, STATE_COLS), lambda s, i: (i, s)),
        out_shape=jax.ShapeDtypeStruct((t // CHUNK, N_SLABS * STATE_COLS), F32),
        compiler_params=_params(("parallel", "parallel")),
        name="s5_state_in",
    )(u, w_in_state)


def _s5_scan_kernel(sin_ref, a_ref, o_ref, st_ref, *, cpt):
    half = SCAN_TILE // 2
    backward = pl.program_id(1) == 1

    @pl.when(pl.program_id(2) == 0)
    def _():
        st_ref[...] = jnp.zeros_like(st_ref)

    ar = a_ref[:, :half]
    ai = a_ref[:, half:]

    def body(i, carry):
        sr, si = carry
        c = jnp.where(backward, cpt - 1 - i, i)
        rows = pl.ds(pl.multiple_of(c * SUBLANES, SUBLANES), SUBLANES)
        o_ref[rows, :half] = sr
        o_ref[rows, half:] = si
        xr = sin_ref[rows, :half]
        xi = sin_ref[rows, half:]
        return ar * sr - ai * si + xr, ar * si + ai * sr + xi

    sr, si = lax.fori_loop(0, cpt, body, (st_ref[:, :half], st_ref[:, half:]), unroll=4)
    st_ref[:, :half] = sr
    st_ref[:, half:] = si


def _s5_scan(sin, a16, n_chunks, cpt=64):
    nct = n_chunks // cpt
    blk = lambda s, d, k: (k + d * (nct - 1 - 2 * k), s * 2 + d)
    return pl.pallas_call(
        functools.partial(_s5_scan_kernel, cpt=cpt),
        grid=(N_SLABS, 2, nct),
        in_specs=[pl.BlockSpec((cpt * SUBLANES, SCAN_TILE), blk),
                  pl.BlockSpec((SUBLANES, SCAN_TILE), lambda s, d, k: (0, s * 2 + d))],
        out_specs=pl.BlockSpec((cpt * SUBLANES, SCAN_TILE), blk),
        out_shape=jax.ShapeDtypeStruct(sin.shape, F32),
        scratch_shapes=[pltpu.VMEM((SUBLANES, SCAN_TILE), F32)],
        compiler_params=_params(("parallel", "parallel", "arbitrary")),
        name="s5_scan",
    )(sin, a16)


def _s5_out_kernel(u_ref, c_ref, wt_ref, wo_ref, y_ref):
    g = _group_rows(u_ref)
    ys = []
    for pq in range(PAIRS_PER_SLAB):
        carry = jnp.concatenate([c_ref[:, tile] for tile in _state_tiles(pq)], axis=1).astype(BF16)
        from_state = _dot(carry, wo_ref[pq])
        for a2 in range(2):
            a = 2 * pq + a2
            ys.append(_dot(g[a], wt_ref[a]) + from_state[:, a2 * GROUP_K:(a2 + 1) * GROUP_K])
    halves = (_block_transpose([y[:, :LANES] for y in ys]), _block_transpose([y[:, LANES:] for y in ys]))
    per_tile = TILE_ROWS // CHUNK
    for t in range(CHUNK):
        tok = halves[t // (CHUNK // 2)][t % (CHUNK // 2)].astype(BF16)
        for k in range(y_ref.shape[0] // TILE_ROWS):
            y_ref[_token_rows(t, k), :] = tok[k * per_tile:(k + 1) * per_tile, :]


def _s5_out(u, carry, w_toep, w_out_state, rows=S5_ROWS):
    t = u.shape[0]
    return pl.pallas_call(
        _s5_out_kernel,
        grid=(N_SLABS, t // rows),
        in_specs=[pl.BlockSpec((rows, LANES), lambda s, i: (i, s)),
                  pl.BlockSpec((rows // CHUNK, STATE_COLS), lambda s, i: (i, s)),
                  pl.BlockSpec((None,) + w_toep.shape[1:], lambda s, i: (s, 0, 0, 0)),
                  pl.BlockSpec((None,) + w_out_state.shape[1:], lambda s, i: (s, 0, 0, 0))],
        out_specs=pl.BlockSpec((rows, LANES), lambda s, i: (i, s)),
        out_shape=jax.ShapeDtypeStruct((t, S5_WIDTH), BF16),
        compiler_params=_params(("parallel", "parallel")),
        name="s5_out",
    )(u, carry, w_toep, w_out_state)


def _s5_weights(lam_re, lam_im, log_dt, b_re, b_im, c_re, c_im, d):
    G, P, H, C, S = S5_GROUPS, S5_STATE, S5_GROUP, CHUNK, GROUPS_PER_SLAB
    lr = lam_re.astype(F32)
    li = lam_im.astype(F32)
    dt = jnp.exp(log_dt.astype(F32))[..., None]
    mag = jnp.exp(lr * dt)
    ab_re = mag * jnp.cos(li * dt)
    ab_im = mag * jnp.sin(li * dt)
    nr = ab_re - 1.0
    den = lr * lr + li * li
    coef_re = (nr * lr + ab_im * li) / den
    coef_im = (ab_im * lr - nr * li) / den
    br = b_re.astype(F32)
    bi = b_im.astype(F32)
    bb_re = coef_re[..., None] * br - coef_im[..., None] * bi
    bb_im = coef_re[..., None] * bi + coef_im[..., None] * br
    cr = c_re.astype(F32)
    ci = c_im.astype(F32)

    k = jnp.arange(C + 1, dtype=F32)[:, None, None, None]
    pmag = jnp.exp(k * (lr * dt))
    pw_re = pmag * jnp.cos(k * (li * dt))
    pw_im = pmag * jnp.sin(k * (li * dt))

    n_pairs = G // 2
    pw = jnp.stack([pw_re, pw_im], axis=0)
    up = jnp.arange(C)

    def table(k_fwd, k_bwd):
        return jnp.stack([pw[:, k_fwd, 0], pw[:, k_bwd, 1]], axis=0).transpose(3, 0, 1, 2, 4)

    by_state = lambda tab: tab.transpose(0, 1, 2, 4, 3).reshape(n_pairs, 2, 2, 2, P, C)
    lag_pw = by_state(table(up, C - 1 - up))
    out_pw = by_state(table(up + 1, C - up))
    in_pw = table(C - 1 - up, up).reshape(n_pairs, 2, 2, 2, C, P)
    in_pw = jnp.tile(in_pw, (1, 1, 1, 1, 1, 2))
    ct = jnp.stack([cr, ci], axis=0).transpose(2, 1, 0, 4, 3).reshape(n_pairs, 2, 2, 2, P, H)
    bbt = jnp.stack([bb_re, bb_im], axis=0).transpose(2, 1, 0, 4, 3).reshape(n_pairs, 2, 2, 2, H, P)
    own = jnp.eye(2, dtype=F32)[None, :, None, None, None, :, None]
    bb_own = (bbt[:, :, :, :, :, None, :] * own).reshape(n_pairs, 2, 2, 2, H, 2 * P)
    skip = jnp.eye(H, dtype=F32)[None] * d.astype(F32).reshape(G, 1, H)
    skip = jnp.pad(skip, ((0, 0), (0, 0), ((C - 1) * H, 0))).reshape(n_pairs, 2, H, GROUP_K)
    rep_h = jnp.tile(jnp.eye(H, dtype=F32), (1, C))
    rep_k = jnp.repeat(jnp.eye(C, dtype=F32), H, axis=1)
    w_toep, w_in_state, w_out_state = _s5_pack(ct, bbt, bb_own, lag_pw, out_pw, in_pw, skip, rep_h, rep_k)

    a16 = jnp.stack([pw_re[C], pw_im[C]], axis=0)
    a16 = a16.reshape(2, 2, N_SLABS, S, P).transpose(2, 1, 0, 3, 4)
    a16 = jnp.broadcast_to(a16.reshape(1, N_SLABS * STATE_COLS), (SUBLANES, N_SLABS * STATE_COLS))
    return w_in_state, w_toep, w_out_state, a16


def _s5_pack_kernel(ct_ref, bbt_ref, bbo_ref, lagp_ref, outp_ref, inp_ref, skip_ref, reph_ref, repk_ref,
                    wt_ref, wi_ref, wo_ref):
    hp = lax.Precision.HIGHEST
    exact_dot = lambda a, b: jnp.dot(a, b, precision=hp, preferred_element_type=F32)
    rep_h = reph_ref[...]
    rep_k = repk_ref[...]
    wide = 2 * GROUP_K
    for a2 in range(2):
        lag_part = []
        for dr in range(2):
            c_re = exact_dot(ct_ref[a2, dr, 0], rep_h)
            c_im = exact_dot(ct_ref[a2, dr, 1], rep_h)

            def times_power(ref):
                p_re = exact_dot(ref[a2, dr, 0], rep_k)
                p_im = exact_dot(ref[a2, dr, 1], rep_k)
                return c_re * p_re - c_im * p_im, c_re * p_im + c_im * p_re

            l_re, l_im = times_power(lagp_ref)
            lag_part.append(exact_dot(bbt_ref[a2, dr, 0], l_re) - exact_dot(bbt_ref[a2, dr, 1], l_im))
            w_re, w_im = times_power(outp_ref)
            for ri, val in ((0, w_re), (1, -w_im)):
                rows = pl.ds((dr * 2 + ri) * LANES + a2 * S5_STATE, S5_STATE)
                wo_ref[rows, a2 * GROUP_K:(a2 + 1) * GROUP_K] = val.astype(BF16)
                wo_ref[rows, (1 - a2) * GROUP_K:(2 - a2) * GROUP_K] = jnp.zeros((S5_STATE, GROUP_K), BF16)
            b_re = bbo_ref[a2, dr, 0]
            b_im = bbo_ref[a2, dr, 1]
            for t in range(CHUNK):
                e_re = inp_ref[a2, dr, 0, t:t + 1, :]
                e_im = inp_ref[a2, dr, 1, t:t + 1, :]
                rows = pl.ds(a2 * GROUP_K + t * S5_GROUP, S5_GROUP)
                wi_ref[rows, (2 * dr) * LANES:(2 * dr + 1) * LANES] = (b_re * e_re - b_im * e_im).astype(BF16)
                wi_ref[rows, (2 * dr + 1) * LANES:(2 * dr + 2) * LANES] = (b_re * e_im + b_im * e_re).astype(BF16)
        zero = jnp.zeros((S5_GROUP, GROUP_K), F32)
        lags = (jnp.concatenate([lag_part[1] + skip_ref[a2], zero], axis=1)
                + pltpu.roll(jnp.concatenate([lag_part[0], zero], axis=1), (CHUNK - 1) * S5_GROUP, 1))
        for t in range(CHUNK):
            shift = (CHUNK - 1 - t) * S5_GROUP
            window = lags if shift == 0 else pltpu.roll(lags, wide - shift, 1)
            wt_ref[a2, t * S5_GROUP:(t + 1) * S5_GROUP, :] = window[:, :GROUP_K].astype(BF16)


def _s5_pack(ct, bbt, bb_own, lag_pw, out_pw, in_pw, skip, rep_h, rep_k):
    n_pairs = ct.shape[0]
    per_pair = lambda a: pl.BlockSpec((None,) + a.shape[1:], lambda i: (i,) + (0,) * (a.ndim - 1))
    tabs = (ct, bbt, bb_own, lag_pw, out_pw, in_pw, skip)
    wt, wi, wo = pl.pallas_call(
        _s5_pack_kernel,
        grid=(n_pairs,),
        in_specs=[per_pair(a) for a in tabs] + [_const_spec(rep_h.shape), _const_spec(rep_k.shape)],
        out_specs=[pl.BlockSpec((2, GROUP_K, GROUP_K), lambda i: (i, 0, 0)),
                   pl.BlockSpec((None, 2 * GROUP_K, 4 * LANES), lambda i: (i, 0, 0)),
                   pl.BlockSpec((None, 4 * LANES, 2 * GROUP_K), lambda i: (i, 0, 0))],
        out_shape=[jax.ShapeDtypeStruct((2 * n_pairs, GROUP_K, GROUP_K), BF16),
                   jax.ShapeDtypeStruct((n_pairs, 2 * GROUP_K, 4 * LANES), BF16),
                   jax.ShapeDtypeStruct((n_pairs, 4 * LANES, 2 * GROUP_K), BF16)],
        compiler_params=_params(("parallel",)),
        name="s5_pack",
    )(*tabs, rep_h, rep_k)
    return (wt.reshape(N_SLABS, GROUPS_PER_SLAB, GROUP_K, GROUP_K),
            wi.reshape(N_SLABS, PAIRS_PER_SLAB, 2 * GROUP_K, 4 * LANES),
            wo.reshape(N_SLABS, PAIRS_PER_SLAB, 4 * LANES, 2 * GROUP_K))


def _attn_kernel(sink_ref, q_ref, kl_ref, kc_ref, kr_ref, bias_ref, o_ref):
    j = pl.program_id(1)
    nb = pl.num_programs(1)
    nk = 3 * BLOCK
    kv = jnp.concatenate([kl_ref[...], kc_ref[...], kr_ref[...]], axis=0)
    col = lax.broadcasted_iota(jnp.int32, (1, 2 * nk), 1) % nk
    valid = ((col >= BLOCK) | (j > 0)) & ((col < 2 * BLOCK) | (j < nb - 1))
    lane = lax.broadcasted_iota(jnp.int32, (nk, LANES), 1)
    low = lane < HEAD_DIM
    zero = jnp.zeros((nk, LANES), BF16)

    def block_diag(tile, kvh):
        other = pltpu.roll(tile, HEAD_DIM, 1)
        first, second = (tile, other) if kvh == 0 else (other, tile)
        return jnp.concatenate([jnp.where(low, first, zero), jnp.where(low, zero, second)], axis=0)

    top = lax.broadcasted_iota(jnp.int32, (2 * nk, LANES), 0) < nk
    ones_bd = (top == (lax.broadcasted_iota(jnp.int32, (2 * nk, LANES), 1) < HEAD_DIM)).astype(BF16)

    for kvh in range(N_KV_HEADS):
        kbd = block_diag(kv[:, :LANES], kvh)
        vbd = jnp.concatenate([block_diag(kv[:, LANES:], kvh), ones_bd], axis=1)
        for pair in range(Q_PER_KV // 2):
            pi = kvh * (Q_PER_KV // 2) + pair
            qp = q_ref[:, pi * LANES:(pi + 1) * LANES]
            s = lax.dot_general(qp, kbd, (((1,), (1,)), ((), ())), preferred_element_type=F32)
            s = jnp.where(valid, s + bias_ref[pi], NEG_INF)
            ps, sinks = [], []
            for e in range(2):
                se = s[:, e * nk:(e + 1) * nk]
                sk = sink_ref[2 * pi + e]
                mx = jnp.maximum(jnp.max(se, axis=-1, keepdims=True), sk)
                ps.append(jnp.exp2(se - mx).astype(BF16))
                sinks.append(jnp.broadcast_to(jnp.exp2(sk - mx), (BLOCK, HEAD_DIM)))
            nd = _dot(jnp.concatenate(ps, axis=1), vbd)
            o = nd[:, :LANES] / (nd[:, LANES:] + jnp.concatenate(sinks, axis=1))
            o_ref[:, pi * LANES:(pi + 1) * LANES] = o.astype(BF16)


def _attention(q, kv, sink, bias, bsz, seq):
    nb = seq // BLOCK
    row = lambda b, j, *_: (b * nb + j, 0)
    left = lambda b, j, *_: (b * nb + jnp.maximum(j - 1, 0), 0)
    right = lambda b, j, *_: (b * nb + jnp.minimum(j + 1, nb - 1), 0)
    grid_spec = pltpu.PrefetchScalarGridSpec(
        num_scalar_prefetch=0,
        grid=(bsz, nb),
        in_specs=[pl.BlockSpec(memory_space=pltpu.SMEM),
                  pl.BlockSpec((BLOCK, ATT_WIDTH), row),
                  pl.BlockSpec((BLOCK, 2 * KV_WIDTH), left),
                  pl.BlockSpec((BLOCK, 2 * KV_WIDTH), row),
                  pl.BlockSpec((BLOCK, 2 * KV_WIDTH), right),
                  _const_spec((N_Q_HEADS // 2, BLOCK, 6 * BLOCK))],
        out_specs=pl.BlockSpec((BLOCK, ATT_WIDTH), row),
    )
    return pl.pallas_call(
        _attn_kernel,
        grid_spec=grid_spec,
        out_shape=jax.ShapeDtypeStruct((bsz * seq, ATT_WIDTH), BF16),
        compiler_params=_params(("parallel", "parallel")),
        name="attention",
    )(sink, q, kv, kv, kv, bias)


def _t5_bucket(rel):
    half = NUM_BUCKETS // 2
    max_exact = half // 2
    ret = jnp.where(rel > 0, half, 0)
    n = jnp.abs(rel)
    nf = jnp.maximum(n, 1).astype(jnp.float32)
    large = max_exact + (jnp.log(nf / max_exact) / math.log(MAX_DISTANCE / max_exact)
                         * (half - max_exact)).astype(jnp.int32)
    large = jnp.minimum(large, half - 1)
    return ret + jnp.where(n < max_exact, n, large)


def _band_bias(rel_bias):
    q_loc = jnp.arange(BLOCK, dtype=jnp.int32)
    k_loc = jnp.arange(3 * BLOCK, dtype=jnp.int32)
    rel = (k_loc[None, :] - BLOCK) - q_loc[:, None]
    onehot = (_t5_bucket(rel)[None] == jnp.arange(NUM_BUCKETS, dtype=jnp.int32)[:, None, None]).astype(F32)
    bias = jnp.einsum('bh,bqk->hqk', rel_bias.astype(F32), onehot, precision=lax.Precision.HIGHEST)
    bias = jnp.where((jnp.abs(rel) <= WINDOW)[None], bias * LOG2E, NEG_INF)
    bias = bias.reshape(N_Q_HEADS // 2, 2, BLOCK, 3 * BLOCK).transpose(0, 2, 1, 3)
    return bias.reshape(N_Q_HEADS // 2, BLOCK, 6 * BLOCK)


def _merge_kernel(h_ref, y_ref, yb_ref, g1_ref, unperm_ref, wg_ref, wglu_ref, wa_ref, wb_ref, wo_ref, o_ref):
    bsz = h_ref.shape[0]
    h = h_ref[...].reshape(TILE_ROWS, D_MODEL)
    hn = _rms(h, g1_ref[...]).astype(BF16)
    gates = _sigmoid(_dot(hn, wg_ref[...]))
    z = _gelu_tanh(_dot(unperm_ref[...], y_ref[...]))
    za = (z * _sigmoid(_dot(z.astype(BF16), wglu_ref[...]))).astype(BF16)
    yb = yb_ref[...].reshape(TILE_ROWS, ATT_WIDTH)
    merged = gates[:, :D_MODEL] * _dot(za, wa_ref[...]) + gates[:, D_MODEL:] * _dot(yb, wb_ref[...])
    o_ref[...] = (h + _dot(merged.astype(BF16), wo_ref[...])).reshape(bsz, TOK_TILE, D_MODEL)


def _merge(h3, y, yb3, g1, unperm, w_gates, w_glu, w_a, w_b, w_o):
    bsz, seq, _ = h3.shape
    tok = lambda width: pl.BlockSpec((bsz, TOK_TILE, width), lambda i: (0, i, 0))
    return pl.pallas_call(
        _merge_kernel,
        grid=(seq // TOK_TILE,),
        in_specs=[tok(D_MODEL), pl.BlockSpec((TILE_ROWS, S5_WIDTH), lambda i: (i, 0)), tok(ATT_WIDTH),
                  _const_spec((1, D_MODEL)), _const_spec(unperm.shape),
                  _const_spec(w_gates.shape), _const_spec(w_glu.shape), _const_spec(w_a.shape),
                  _const_spec(w_b.shape), _const_spec(w_o.shape)],
        out_specs=tok(D_MODEL),
        out_shape=jax.ShapeDtypeStruct((bsz, seq, D_MODEL), F32),
        compiler_params=_params(("parallel",)),
        name="merge",
    )(h3, y, yb3, g1, unperm, w_gates, w_glu, w_a, w_b, w_o)


def _ffn_kernel(h_ref, g2_ref, gf_ref, wg_ref, wu_ref, wd_ref, o_ref, acc_ref, *, final_norm):
    h = h_ref[...]
    hn = _rms(h, g2_ref[...]).astype(BF16)
    acc_ref[...] = h
    for j in range(D_FF // FF_TILE):
        cols = slice(j * FF_TILE, (j + 1) * FF_TILE)
        gate = _dot(hn, wg_ref[:, cols])
        up = _dot(hn, wu_ref[:, cols])
        act = (gate * _sigmoid(gate) * up).astype(BF16)
        acc_ref[...] += _dot(act, wd_ref[cols, :])
    out = acc_ref[...]
    o_ref[...] = _rms(out, gf_ref[...]) if final_norm else out


def _ffn(h, g2, gf, w_gate, w_up, w_down, final_norm, tm=512):
    t = h.shape[0]
    return pl.pallas_call(
        functools.partial(_ffn_kernel, final_norm=final_norm),
        grid=(t // tm,),
        in_specs=[pl.BlockSpec((tm, D_MODEL), lambda i: (i, 0)),
                  _const_spec((1, D_MODEL)), _const_spec((1, D_MODEL)),
                  _const_spec(w_gate.shape), _const_spec(w_up.shape), _const_spec(w_down.shape)],
        out_specs=pl.BlockSpec((tm, D_MODEL), lambda i: (i, 0)),
        out_shape=jax.ShapeDtypeStruct((t, D_MODEL), F32),
        scratch_shapes=[pltpu.VMEM((tm, D_MODEL), F32)],
        compiler_params=_params(("parallel",)),
        name="ffn",
    )(h, g2, gf, w_gate, w_up, w_down)


def kernel(x, norm1_g, norm2_g, final_g, w_in, s5_lambda_re, s5_lambda_im, s5_log_dt, s5_b_re, s5_b_im,
           s5_c_re, s5_c_im, s5_d, s5_w_glu, attn_sink, rel_bias, w_branch_a, w_branch_b, w_out,
           ffn_w_gate, ffn_w_up, ffn_w_down):
    bsz, seq, _ = x.shape
    depth = w_in.shape[0]
    assert seq % BLOCK == 0 and seq % CHUNK == 0 and bsz == SUBLANES
    n_chunks = seq // CHUNK
    t = bsz * seq
    bias = _band_bias(rel_bias)
    o_k = S5_WIDTH + ATT_WIDTH
    o_g = o_k + 2 * KV_WIDTH
    col_scale = jnp.concatenate([jnp.ones((S5_WIDTH,), F32), jnp.full((ATT_WIDTH,), LOG2E * HEAD_DIM ** -0.5, F32),
                                 jnp.ones((2 * KV_WIDTH,), F32)])
    gf = final_g.reshape(1, D_MODEL).astype(F32)
    perm = _to_chunk_order()
    unperm = perm.T
    h = x
    for layer in range(depth):
        g1 = norm1_g[layer].reshape(1, D_MODEL).astype(F32)
        g2 = norm2_g[layer].reshape(1, D_MODEL).astype(F32)
        w_uqkv = (w_in[layer][:, :o_g] * col_scale).astype(BF16)
        w_gates = w_in[layer][:, o_g:].astype(BF16)
        u, q, kv = _inproj(h, g1, w_uqkv, perm)
        w_in_state, w_toep, w_out_state, a16 = _s5_weights(
            s5_lambda_re[layer], s5_lambda_im[layer], s5_log_dt[layer], s5_b_re[layer], s5_b_im[layer],
            s5_c_re[layer], s5_c_im[layer], s5_d[layer])
        sin = _s5_state_in(u, w_in_state)
        carry = _s5_scan(sin, a16, n_chunks)
        y = _s5_out(u, carry, w_toep, w_out_state)
        yb = _attention(q.reshape(t, ATT_WIDTH), kv.reshape(t, 2 * KV_WIDTH),
                        attn_sink[layer].astype(F32) * LOG2E, bias, bsz, seq)
        h = _merge(h, y, yb.reshape(bsz, seq, ATT_WIDTH), g1, unperm, w_gates, s5_w_glu[layer].astype(BF16),
                   w_branch_a[layer].astype(BF16), w_branch_b[layer].astype(BF16), w_out[layer].astype(BF16))
        h = _ffn(h.reshape(t, D_MODEL), g2, gf, ffn_w_gate[layer].astype(BF16), ffn_w_up[layer].astype(BF16),
                 ffn_w_down[layer].astype(BF16), final_norm=(layer == depth - 1)).reshape(bsz, seq, D_MODEL)
    return h
```

```python
import functools
import math

import jax
import jax.numpy as jnp
from jax import lax
from jax.experimental import pallas as pl
from jax.experimental.pallas import tpu as pltpu

F32 = jnp.float32
BF16 = jnp.bfloat16

D_MODEL = 1024
S5_WIDTH = 512
S5_GROUP = 16
S5_GROUPS = 32
S5_STATE = 64
HEAD_DIM = 64
N_Q_HEADS = 8
N_KV_HEADS = 2
Q_PER_KV = N_Q_HEADS // N_KV_HEADS
ATT_WIDTH = N_Q_HEADS * HEAD_DIM
KV_WIDTH = N_KV_HEADS * HEAD_DIM
WINDOW = 128
BLOCK = 128
NUM_BUCKETS = 32
MAX_DISTANCE = 128
D_FF = 2816
RMS_EPS = 1e-6
NEG_INF = -1e30
LOG2E = math.log2(math.e)

LANES = 128
SUBLANES = 8
CHUNK = 16
GROUPS_PER_SLAB = LANES // S5_GROUP
N_SLABS = S5_WIDTH // LANES
PAIRS_PER_SLAB = GROUPS_PER_SLAB // 2
GROUP_K = CHUNK * S5_GROUP
STATE_COLS = 2 * 2 * GROUPS_PER_SLAB * S5_STATE
SCAN_TILE = STATE_COLS // 2
TOK_TILE = 64
TILE_ROWS = SUBLANES * TOK_TILE
S5_ROWS = 8 * TILE_ROWS
FF_TILE = 256
VMEM_LIMIT = 56 * 1024 * 1024


def _rms(x, g):
    return x * lax.rsqrt(jnp.mean(x * x, axis=-1, keepdims=True) + RMS_EPS) * g


def _gelu_tanh(x):
    return 0.5 * x * (1.0 + jnp.tanh(math.sqrt(2.0 / math.pi) * (x + 0.044715 * (x * x * x))))


def _sigmoid(x):
    return 1.0 / (1.0 + jnp.exp(-x))


def _dot(a, b):
    return jnp.dot(a, b, preferred_element_type=F32)


def _const_spec(shape):
    nd = len(shape)
    return pl.BlockSpec(shape, lambda *_: (0,) * nd, pipeline_mode=pl.Buffered(1))


def _params(sem):
    return pltpu.CompilerParams(dimension_semantics=sem, vmem_limit_bytes=VMEM_LIMIT)


def _to_chunk_order():
    cpt = TOK_TILE // CHUNK
    dst = jnp.arange(TILE_ROWS)
    t, c, b = dst // (cpt * SUBLANES), (dst // SUBLANES) % cpt, dst % SUBLANES
    src = b * TOK_TILE + c * CHUNK + t
    return (src[:, None] == jnp.arange(TILE_ROWS)[None, :]).astype(BF16)


def _inproj_kernel(x_ref, g_ref, w_ref, perm_ref, u_ref, q_ref, kv_ref):
    bsz = x_ref.shape[0]
    hn = _rms(x_ref[...].reshape(TILE_ROWS, D_MODEL), g_ref[...]).astype(BF16)
    r = _dot(hn, w_ref[...])
    u_ref[...] = _dot(perm_ref[...], r[:, :S5_WIDTH].astype(BF16)).astype(BF16)
    q_ref[...] = r[:, S5_WIDTH:S5_WIDTH + ATT_WIDTH].astype(BF16).reshape(bsz, TOK_TILE, ATT_WIDTH)
    kv_ref[...] = r[:, S5_WIDTH + ATT_WIDTH:].astype(BF16).reshape(bsz, TOK_TILE, 2 * KV_WIDTH)


def _inproj(h3, g, w, perm):
    bsz, seq, _ = h3.shape
    assert bsz * TOK_TILE == TILE_ROWS
    n = w.shape[1]
    tok = lambda width: pl.BlockSpec((bsz, TOK_TILE, width), lambda i: (0, i, 0))
    return pl.pallas_call(
        _inproj_kernel,
        grid=(seq // TOK_TILE,),
        in_specs=[tok(D_MODEL), _const_spec((1, D_MODEL)), _const_spec((D_MODEL, n)),
                  _const_spec((TILE_ROWS, TILE_ROWS))],
        out_specs=[pl.BlockSpec((TILE_ROWS, S5_WIDTH), lambda i: (i, 0)), tok(ATT_WIDTH), tok(2 * KV_WIDTH)],
        out_shape=[jax.ShapeDtypeStruct((bsz * seq, S5_WIDTH), BF16),
                   jax.ShapeDtypeStruct((bsz, seq, ATT_WIDTH), BF16),
                   jax.ShapeDtypeStruct((bsz, seq, 2 * KV_WIDTH), BF16)],
        compiler_params=_params(("parallel",)),
        name="inproj",
    )(h3, g, w, perm)


def _block_transpose(x):
    x = list(x)
    blk = lax.broadcasted_iota(jnp.int32, x[0].shape, 1) // S5_GROUP
    for d in (4, 2, 1):
        keep = (blk & d) == 0
        for i in range(GROUPS_PER_SLAB):
            if i & d:
                continue
            xi, xj = x[i], x[i + d]
            x[i] = jnp.where(keep, xi, pltpu.roll(xj, d * S5_GROUP, 1))
            x[i + d] = jnp.where(keep, pltpu.roll(xi, LANES - d * S5_GROUP, 1), xj)
    return x


def _token_rows(t, k):
    return pl.ds(k * TILE_ROWS + t * (TILE_ROWS // CHUNK), TILE_ROWS // CHUNK)


def _group_rows(u_ref):
    tiles = u_ref.shape[0] // TILE_ROWS
    tok = [jnp.concatenate([u_ref[_token_rows(t, k), :] for k in range(tiles)], axis=0) for t in range(CHUNK)]
    lo = _block_transpose(tok[:CHUNK // 2])
    hi = _block_transpose(tok[CHUNK // 2:])
    return [jnp.concatenate([lo[a], hi[a]], axis=1) for a in range(GROUPS_PER_SLAB)]


def _state_tiles(pq):
    return [pl.ds((dr * 2 + ri) * (STATE_COLS // 4) + pq * LANES, LANES) for dr in range(2) for ri in range(2)]


def _s5_in_kernel(u_ref, w_ref, o_ref):
    g = _group_rows(u_ref)
    for pq in range(PAIRS_PER_SLAB):
        r = _dot(jnp.concatenate([g[2 * pq], g[2 * pq + 1]], axis=1), w_ref[pq])
        for k, tile in enumerate(_state_tiles(pq)):
            o_ref[:, tile] = r[:, k * LANES:(k + 1) * LANES].astype(BF16)


def _s5_state_in(u, w_in_state, rows=S5_ROWS):
    t = u.shape[0]
    return pl.pallas_call(
        _s5_in_kernel,
        grid=(N_SLABS, t // rows),
        in_specs=[pl.BlockSpec((rows, LANES), lambda s, i: (i, s)),
                  pl.BlockSpec((None,) + w_in_state.shape[1:], lambda s, i: (s, 0, 0, 0))],
        out_specs=pl.BlockSpec((rows // CHUNK, STATE_COLS), lambda s, i: (i, s)),
        out_shape=jax.ShapeDtypeStruct((t // CHUNK, N_SLABS * STATE_COLS), BF16),
        compiler_params=_params(("parallel", "parallel")),
        name="s5_state_in",
    )(u, w_in_state)


def _s5_scan_kernel(sin_ref, a_ref, o_ref, st_ref, *, cpt):
    half = SCAN_TILE // 2
    pair_rows = 2 * SUBLANES

    @pl.when(pl.program_id(2) == 0)
    def _():
        st_ref[...] = jnp.zeros_like(st_ref)

    ar = a_ref[:, :half]
    ai = a_ref[:, half:]

    def advance(sr, si, xr, xi):
        return ar * sr - ai * si + xr, ar * si + ai * sr + xi

    def run(reverse):
        lower, upper = slice(0, SUBLANES), slice(SUBLANES, pair_rows)
        first, second = (upper, lower) if reverse else (lower, upper)

        def body(i, carry):
            s0r, s0i = carry
            c2 = (cpt // 2 - 1 - i) if reverse else i
            rows = pl.ds(pl.multiple_of(c2 * pair_rows, pair_rows), pair_rows)
            x = sin_ref[rows, :].astype(F32)
            s1r, s1i = advance(s0r, s0i, x[first, :half], x[first, half:])
            s2r, s2i = advance(s1r, s1i, x[second, :half], x[second, half:])
            enter_r, enter_i = ((s1r, s0r), (s1i, s0i)) if reverse else ((s0r, s1r), (s0i, s1i))
            o_ref[rows, :half] = jnp.concatenate(enter_r, axis=0).astype(BF16)
            o_ref[rows, half:] = jnp.concatenate(enter_i, axis=0).astype(BF16)
            return s2r, s2i

        sr, si = lax.fori_loop(0, cpt // 2, body, (st_ref[:, :half], st_ref[:, half:]), unroll=2)
        st_ref[:, :half] = sr
        st_ref[:, half:] = si

    backward = pl.program_id(1) == 1
    pl.when(backward)(lambda: run(True))
    pl.when(jnp.logical_not(backward))(lambda: run(False))


def _s5_scan(sin, a16, n_chunks, cpt=64):
    nct = n_chunks // cpt
    blk = lambda s, d, k: (k + d * (nct - 1 - 2 * k), s * 2 + d)
    return pl.pallas_call(
        functools.partial(_s5_scan_kernel, cpt=cpt),
        grid=(N_SLABS, 2, nct),
        in_specs=[pl.BlockSpec((cpt * SUBLANES, SCAN_TILE), blk),
                  pl.BlockSpec((SUBLANES, SCAN_TILE), lambda s, d, k: (0, s * 2 + d))],
        out_specs=pl.BlockSpec((cpt * SUBLANES, SCAN_TILE), blk),
        out_shape=jax.ShapeDtypeStruct(sin.shape, BF16),
        scratch_shapes=[pltpu.VMEM((SUBLANES, SCAN_TILE), F32)],
        compiler_params=_params(("parallel", "parallel", "arbitrary")),
        name="s5_scan",
    )(sin, a16)


def _s5_out_kernel(u_ref, c_ref, wt_ref, wo_ref, y_ref):
    g = _group_rows(u_ref)
    ys = []
    for pq in range(PAIRS_PER_SLAB):
        carry = jnp.concatenate([c_ref[:, tile] for tile in _state_tiles(pq)], axis=1)
        from_state = _dot(carry, wo_ref[pq])
        for a2 in range(2):
            a = 2 * pq + a2
            ys.append(_dot(g[a], wt_ref[a]) + from_state[:, a2 * GROUP_K:(a2 + 1) * GROUP_K])
    halves = (_block_transpose([y[:, :LANES] for y in ys]), _block_transpose([y[:, LANES:] for y in ys]))
    per_tile = TILE_ROWS // CHUNK
    for t in range(CHUNK):
        tok = halves[t // (CHUNK // 2)][t % (CHUNK // 2)].astype(BF16)
        for k in range(y_ref.shape[0] // TILE_ROWS):
            y_ref[_token_rows(t, k), :] = tok[k * per_tile:(k + 1) * per_tile, :]


def _s5_out(u, carry, w_toep, w_out_state, rows=S5_ROWS):
    t = u.shape[0]
    return pl.pallas_call(
        _s5_out_kernel,
        grid=(N_SLABS, t // rows),
        in_specs=[pl.BlockSpec((rows, LANES), lambda s, i: (i, s)),
                  pl.BlockSpec((rows // CHUNK, STATE_COLS), lambda s, i: (i, s)),
                  pl.BlockSpec((None,) + w_toep.shape[1:], lambda s, i: (s, 0, 0, 0)),
                  pl.BlockSpec((None,) + w_out_state.shape[1:], lambda s, i: (s, 0, 0, 0))],
        out_specs=pl.BlockSpec((rows, LANES), lambda s, i: (i, s)),
        out_shape=jax.ShapeDtypeStruct((t, S5_WIDTH), BF16),
        compiler_params=_params(("parallel", "parallel")),
        name="s5_out",
    )(u, carry, w_toep, w_out_state)


def _s5_weights(lam_re, lam_im, log_dt, b_re, b_im, c_re, c_im, d):
    G, P, H, C, S = S5_GROUPS, S5_STATE, S5_GROUP, CHUNK, GROUPS_PER_SLAB
    lr = lam_re.astype(F32)
    li = lam_im.astype(F32)
    dt = jnp.exp(log_dt.astype(F32))[..., None]
    mag = jnp.exp(lr * dt)
    ab_re = mag * jnp.cos(li * dt)
    ab_im = mag * jnp.sin(li * dt)
    nr = ab_re - 1.0
    den = lr * lr + li * li
    coef_re = (nr * lr + ab_im * li) / den
    coef_im = (ab_im * lr - nr * li) / den
    br = b_re.astype(F32)
    bi = b_im.astype(F32)
    bb_re = coef_re[..., None] * br - coef_im[..., None] * bi
    bb_im = coef_re[..., None] * bi + coef_im[..., None] * br
    cr = c_re.astype(F32)
    ci = c_im.astype(F32)

    k = jnp.arange(C + 1, dtype=F32)[:, None, None, None]
    pmag = jnp.exp(k * (lr * dt))
    pw_re = pmag * jnp.cos(k * (li * dt))
    pw_im = pmag * jnp.sin(k * (li * dt))

    n_pairs = G // 2
    pw = jnp.stack([pw_re, pw_im], axis=0)
    up = jnp.arange(C)

    def table(k_fwd, k_bwd):
        return jnp.stack([pw[:, k_fwd, 0], pw[:, k_bwd, 1]], axis=0).transpose(3, 0, 1, 2, 4)

    by_state = lambda tab: tab.transpose(0, 1, 2, 4, 3).reshape(n_pairs, 2, 2, 2, P, C)
    lag_pw = by_state(table(up, C - 1 - up))
    out_pw = by_state(table(up + 1, C - up))
    in_pw = table(C - 1 - up, up).reshape(n_pairs, 2, 2, 2, C, P)
    in_pw = jnp.tile(in_pw, (1, 1, 1, 1, 1, 2))
    ct = jnp.stack([cr, ci], axis=0).transpose(2, 1, 0, 4, 3).reshape(n_pairs, 2, 2, 2, P, H)
    bbt = jnp.stack([bb_re, bb_im], axis=0).transpose(2, 1, 0, 4, 3).reshape(n_pairs, 2, 2, 2, H, P)
    own = jnp.eye(2, dtype=F32)[None, :, None, None, None, :, None]
    bb_own = (bbt[:, :, :, :, :, None, :] * own).reshape(n_pairs, 2, 2, 2, H, 2 * P)
    skip = jnp.eye(H, dtype=F32)[None] * d.astype(F32).reshape(G, 1, H)
    skip = jnp.pad(skip, ((0, 0), (0, 0), ((C - 1) * H, 0))).reshape(n_pairs, 2, H, GROUP_K)
    rep_h = jnp.tile(jnp.eye(H, dtype=F32), (1, C))
    rep_k = jnp.repeat(jnp.eye(C, dtype=F32), H, axis=1)
    w_toep, w_in_state, w_out_state = _s5_pack(ct, bbt, bb_own, lag_pw, out_pw, in_pw, skip, rep_h, rep_k)

    a16 = jnp.stack([pw_re[C], pw_im[C]], axis=0)
    a16 = a16.reshape(2, 2, N_SLABS, S, P).transpose(2, 1, 0, 3, 4)
    a16 = jnp.broadcast_to(a16.reshape(1, N_SLABS * STATE_COLS), (SUBLANES, N_SLABS * STATE_COLS))
    return w_in_state, w_toep, w_out_state, a16


def _s5_pack_kernel(ct_ref, bbt_ref, bbo_ref, lagp_ref, outp_ref, inp_ref, skip_ref, reph_ref, repk_ref,
                    wt_ref, wi_ref, wo_ref):
    hp = lax.Precision.HIGHEST
    exact_dot = lambda a, b: jnp.dot(a, b, precision=hp, preferred_element_type=F32)
    rep_h = reph_ref[...]
    rep_k = repk_ref[...]
    wide = 2 * GROUP_K
    for a2 in range(2):
        lag_part = []
        for dr in range(2):
            c_re = exact_dot(ct_ref[a2, dr, 0], rep_h)
            c_im = exact_dot(ct_ref[a2, dr, 1], rep_h)

            def times_power(ref):
                p_re = exact_dot(ref[a2, dr, 0], rep_k)
                p_im = exact_dot(ref[a2, dr, 1], rep_k)
                return c_re * p_re - c_im * p_im, c_re * p_im + c_im * p_re

            l_re, l_im = times_power(lagp_ref)
            lag_part.append(exact_dot(bbt_ref[a2, dr, 0], l_re) - exact_dot(bbt_ref[a2, dr, 1], l_im))
            w_re, w_im = times_power(outp_ref)
            for ri, val in ((0, w_re), (1, -w_im)):
                rows = pl.ds((dr * 2 + ri) * LANES + a2 * S5_STATE, S5_STATE)
                wo_ref[rows, a2 * GROUP_K:(a2 + 1) * GROUP_K] = val.astype(BF16)
                wo_ref[rows, (1 - a2) * GROUP_K:(2 - a2) * GROUP_K] = jnp.zeros((S5_STATE, GROUP_K), BF16)
            b_re = bbo_ref[a2, dr, 0]
            b_im = bbo_ref[a2, dr, 1]
            for t in range(CHUNK):
                e_re = inp_ref[a2, dr, 0, t:t + 1, :]
                e_im = inp_ref[a2, dr, 1, t:t + 1, :]
                rows = pl.ds(a2 * GROUP_K + t * S5_GROUP, S5_GROUP)
                wi_ref[rows, (2 * dr) * LANES:(2 * dr + 1) * LANES] = (b_re * e_re - b_im * e_im).astype(BF16)
                wi_ref[rows, (2 * dr + 1) * LANES:(2 * dr + 2) * LANES] = (b_re * e_im + b_im * e_re).astype(BF16)
        zero = jnp.zeros((S5_GROUP, GROUP_K), F32)
        lags = (jnp.concatenate([lag_part[1] + skip_ref[a2], zero], axis=1)
                + pltpu.roll(jnp.concatenate([lag_part[0], zero], axis=1), (CHUNK - 1) * S5_GROUP, 1))
        for t in range(CHUNK):
            shift = (CHUNK - 1 - t) * S5_GROUP
            window = lags if shift == 0 else pltpu.roll(lags, wide - shift, 1)
            wt_ref[a2, t * S5_GROUP:(t + 1) * S5_GROUP, :] = window[:, :GROUP_K].astype(BF16)


def _s5_pack(ct, bbt, bb_own, lag_pw, out_pw, in_pw, skip, rep_h, rep_k):
    n_pairs = ct.shape[0]
    per_pair = lambda a: pl.BlockSpec((None,) + a.shape[1:], lambda i: (i,) + (0,) * (a.ndim - 1))
    tabs = (ct, bbt, bb_own, lag_pw, out_pw, in_pw, skip)
    wt, wi, wo = pl.pallas_call(
        _s5_pack_kernel,
        grid=(n_pairs,),
        in_specs=[per_pair(a) for a in tabs] + [_const_spec(rep_h.shape), _const_spec(rep_k.shape)],
        out_specs=[pl.BlockSpec((2, GROUP_K, GROUP_K), lambda i: (i, 0, 0)),
                   pl.BlockSpec((None, 2 * GROUP_K, 4 * LANES), lambda i: (i, 0, 0)),
                   pl.BlockSpec((None, 4 * LANES, 2 * GROUP_K), lambda i: (i, 0, 0))],
        out_shape=[jax.ShapeDtypeStruct((2 * n_pairs, GROUP_K, GROUP_K), BF16),
                   jax.ShapeDtypeStruct((n_pairs, 2 * GROUP_K, 4 * LANES), BF16),
                   jax.ShapeDtypeStruct((n_pairs, 4 * LANES, 2 * GROUP_K), BF16)],
        compiler_params=_params(("parallel",)),
        name="s5_pack",
    )(*tabs, rep_h, rep_k)
    return (wt.reshape(N_SLABS, GROUPS_PER_SLAB, GROUP_K, GROUP_K),
            wi.reshape(N_SLABS, PAIRS_PER_SLAB, 2 * GROUP_K, 4 * LANES),
            wo.reshape(N_SLABS, PAIRS_PER_SLAB, 4 * LANES, 2 * GROUP_K))


def _attn_kernel(sink_ref, q_ref, kl_ref, kc_ref, kr_ref, bias_ref, o_ref):
    j = pl.program_id(1)
    nj = pl.num_programs(1)
    nk = 3 * BLOCK
    centre = kc_ref[...]
    windows = (jnp.concatenate([kl_ref[...], centre], axis=0), jnp.concatenate([centre, kr_ref[...]], axis=0))
    col = lax.broadcasted_iota(jnp.int32, (1, 2 * nk), 1) % nk
    valids = ((col >= BLOCK) | (j > 0), (col < 2 * BLOCK) | (j < nj - 1))
    lane = lax.broadcasted_iota(jnp.int32, (nk, LANES), 1)
    low = lane < HEAD_DIM
    zero = jnp.zeros((nk, LANES), BF16)

    def block_diag(tile, kvh):
        other = pltpu.roll(tile, HEAD_DIM, 1)
        first, second = (tile, other) if kvh == 0 else (other, tile)
        return jnp.concatenate([jnp.where(low, first, zero), jnp.where(low, zero, second)], axis=0)

    top = lax.broadcasted_iota(jnp.int32, (2 * nk, LANES), 0) < nk
    ones_bd = (top == (lax.broadcasted_iota(jnp.int32, (2 * nk, LANES), 1) < HEAD_DIM)).astype(BF16)

    upper = lax.broadcasted_iota(jnp.int32, (2 * BLOCK, 1), 0) < BLOCK
    for qb, (kv, valid) in enumerate(zip(windows, valids)):
        rows = slice(qb * BLOCK, (qb + 1) * BLOCK)
        for kvh in range(N_KV_HEADS):
            kbd = block_diag(kv[:, :LANES], kvh)
            vbd = jnp.concatenate([block_diag(kv[:, LANES:], kvh), ones_bd], axis=1)
            tiles = [(kvh * 2 + pair) * LANES for pair in range(2)]
            q2 = jnp.concatenate([q_ref[rows, tl:tl + LANES] for tl in tiles], axis=0)
            s = lax.dot_general(q2, kbd, (((1,), (1,)), ((), ())), preferred_element_type=F32)
            s = jnp.where(valid, s + bias_ref[kvh], NEG_INF)
            ps, sinks = [], []
            for e in range(2):
                se = s[:, e * nk:(e + 1) * nk]
                sk = jnp.where(upper, sink_ref[4 * kvh + e], sink_ref[4 * kvh + 2 + e])
                mx = jnp.maximum(jnp.max(se, axis=-1, keepdims=True), sk)
                ps.append(jnp.exp2(se - mx).astype(BF16))
                sinks.append(jnp.broadcast_to(jnp.exp2(sk - mx), (2 * BLOCK, HEAD_DIM)))
            nd = _dot(jnp.concatenate(ps, axis=1), vbd)
            o = (nd[:, :LANES] / (nd[:, LANES:] + jnp.concatenate(sinks, axis=1))).astype(BF16)
            for pair, tl in enumerate(tiles):
                o_ref[rows, tl:tl + LANES] = o[pair * BLOCK:(pair + 1) * BLOCK, :]


def _attention(q, kv, sink, bias, bsz, seq):
    nb = seq // BLOCK
    nj = nb // 2
    two = lambda b, j: (b * nj + j, 0)
    left = lambda b, j: (b * nb + jnp.maximum(2 * j - 1, 0), 0)
    right = lambda b, j: (b * nb + jnp.minimum(2 * j + 2, nb - 1), 0)
    return pl.pallas_call(
        _attn_kernel,
        grid=(bsz, nj),
        in_specs=[pl.BlockSpec(memory_space=pltpu.SMEM),
                  pl.BlockSpec((2 * BLOCK, ATT_WIDTH), two),
                  pl.BlockSpec((BLOCK, 2 * KV_WIDTH), left),
                  pl.BlockSpec((2 * BLOCK, 2 * KV_WIDTH), two),
                  pl.BlockSpec((BLOCK, 2 * KV_WIDTH), right),
                  _const_spec((N_KV_HEADS, 2 * BLOCK, 6 * BLOCK))],
        out_specs=pl.BlockSpec((2 * BLOCK, ATT_WIDTH), two),
        out_shape=jax.ShapeDtypeStruct((bsz * seq, ATT_WIDTH), BF16),
        compiler_params=_params(("parallel", "parallel")),
        name="attention",
    )(sink, q, kv, kv, kv, bias)


def _t5_bucket(rel):
    half = NUM_BUCKETS // 2
    max_exact = half // 2
    ret = jnp.where(rel > 0, half, 0)
    n = jnp.abs(rel)
    nf = jnp.maximum(n, 1).astype(jnp.float32)
    large = max_exact + (jnp.log(nf / max_exact) / math.log(MAX_DISTANCE / max_exact)
                         * (half - max_exact)).astype(jnp.int32)
    large = jnp.minimum(large, half - 1)
    return ret + jnp.where(n < max_exact, n, large)


def _band_bias(rel_bias):
    q_loc = jnp.arange(BLOCK, dtype=jnp.int32)
    k_loc = jnp.arange(3 * BLOCK, dtype=jnp.int32)
    rel = (k_loc[None, :] - BLOCK) - q_loc[:, None]
    onehot = (_t5_bucket(rel)[None] == jnp.arange(NUM_BUCKETS, dtype=jnp.int32)[:, None, None]).astype(F32)
    bias = jnp.einsum('bh,bqk->hqk', rel_bias.astype(F32), onehot, precision=lax.Precision.HIGHEST)
    bias = jnp.where((jnp.abs(rel) <= WINDOW)[None], bias * LOG2E, NEG_INF)
    bias = bias.reshape(N_Q_HEADS // 2, 2, BLOCK, 3 * BLOCK).transpose(0, 2, 1, 3)
    return bias.reshape(N_KV_HEADS, 2 * BLOCK, 6 * BLOCK)


def _merge_kernel(h_ref, y_ref, yb_ref, g1_ref, unperm_ref, wg_ref, wglu_ref, wa_ref, wb_ref, wo_ref, o_ref):
    bsz = h_ref.shape[0]
    h = h_ref[...].reshape(TILE_ROWS, D_MODEL)
    hn = _rms(h, g1_ref[...]).astype(BF16)
    gates = _sigmoid(_dot(hn, wg_ref[...]))
    z = _gelu_tanh(_dot(unperm_ref[...], y_ref[...]))
    za = (z * _sigmoid(_dot(z.astype(BF16), wglu_ref[...]))).astype(BF16)
    yb = yb_ref[...].reshape(TILE_ROWS, ATT_WIDTH)
    merged = gates[:, :D_MODEL] * _dot(za, wa_ref[...]) + gates[:, D_MODEL:] * _dot(yb, wb_ref[...])
    o_ref[...] = (h + _dot(merged.astype(BF16), wo_ref[...])).reshape(bsz, TOK_TILE, D_MODEL)


def _merge(h3, y, yb3, g1, unperm, w_gates, w_glu, w_a, w_b, w_o):
    bsz, seq, _ = h3.shape
    tok = lambda width: pl.BlockSpec((bsz, TOK_TILE, width), lambda i: (0, i, 0))
    return pl.pallas_call(
        _merge_kernel,
        grid=(seq // TOK_TILE,),
        in_specs=[tok(D_MODEL), pl.BlockSpec((TILE_ROWS, S5_WIDTH), lambda i: (i, 0)), tok(ATT_WIDTH),
                  _const_spec((1, D_MODEL)), _const_spec(unperm.shape),
                  _const_spec(w_gates.shape), _const_spec(w_glu.shape), _const_spec(w_a.shape),
                  _const_spec(w_b.shape), _const_spec(w_o.shape)],
        out_specs=tok(D_MODEL),
        out_shape=jax.ShapeDtypeStruct((bsz, seq, D_MODEL), F32),
        compiler_params=_params(("parallel",)),
        name="merge",
    )(h3, y, yb3, g1, unperm, w_gates, w_glu, w_a, w_b, w_o)


def _ffn_kernel(h_ref, g2_ref, gf_ref, wg_ref, wu_ref, wd_ref, o_ref, acc_ref, *, final_norm):
    h = h_ref[...]
    hn = _rms(h, g2_ref[...]).astype(BF16)
    acc_ref[...] = h
    for j in range(D_FF // FF_TILE):
        cols = slice(j * FF_TILE, (j + 1) * FF_TILE)
        gate = _dot(hn, wg_ref[:, cols])
        up = _dot(hn, wu_ref[:, cols])
        act = (gate * _sigmoid(gate) * up).astype(BF16)
        acc_ref[...] += _dot(act, wd_ref[cols, :])
    out = acc_ref[...]
    o_ref[...] = _rms(out, gf_ref[...]) if final_norm else out


def _ffn(h, g2, gf, w_gate, w_up, w_down, final_norm, tm=512):
    t = h.shape[0]
    return pl.pallas_call(
        functools.partial(_ffn_kernel, final_norm=final_norm),
        grid=(t // tm,),
        in_specs=[pl.BlockSpec((tm, D_MODEL), lambda i: (i, 0)),
                  _const_spec((1, D_MODEL)), _const_spec((1, D_MODEL)),
                  _const_spec(w_gate.shape), _const_spec(w_up.shape), _const_spec(w_down.shape)],
        out_specs=pl.BlockSpec((tm, D_MODEL), lambda i: (i, 0)),
        out_shape=jax.ShapeDtypeStruct((t, D_MODEL), F32),
        scratch_shapes=[pltpu.VMEM((tm, D_MODEL), F32)],
        compiler_params=_params(("parallel",)),
        name="ffn",
    )(h, g2, gf, w_gate, w_up, w_down)


def kernel(x, norm1_g, norm2_g, final_g, w_in, s5_lambda_re, s5_lambda_im, s5_log_dt, s5_b_re, s5_b_im,
           s5_c_re, s5_c_im, s5_d, s5_w_glu, attn_sink, rel_bias, w_branch_a, w_branch_b, w_out,
           ffn_w_gate, ffn_w_up, ffn_w_down):
    bsz, seq, _ = x.shape
    depth = w_in.shape[0]
    assert seq % BLOCK == 0 and seq % CHUNK == 0 and bsz == SUBLANES
    n_chunks = seq // CHUNK
    t = bsz * seq
    bias = _band_bias(rel_bias)
    o_k = S5_WIDTH + ATT_WIDTH
    o_g = o_k + 2 * KV_WIDTH
    col_scale = jnp.concatenate([jnp.ones((S5_WIDTH,), F32), jnp.full((ATT_WIDTH,), LOG2E * HEAD_DIM ** -0.5, F32),
                                 jnp.ones((2 * KV_WIDTH,), F32)])
    gf = final_g.reshape(1, D_MODEL).astype(F32)
    perm = _to_chunk_order()
    unperm = perm.T
    h = x
    for layer in range(depth):
        g1 = norm1_g[layer].reshape(1, D_MODEL).astype(F32)
        g2 = norm2_g[layer].reshape(1, D_MODEL).astype(F32)
        w_uqkv = (w_in[layer][:, :o_g] * col_scale).astype(BF16)
        w_gates = w_in[layer][:, o_g:].astype(BF16)
        u, q, kv = _inproj(h, g1, w_uqkv, perm)
        w_in_state, w_toep, w_out_state, a16 = _s5_weights(
            s5_lambda_re[layer], s5_lambda_im[layer], s5_log_dt[layer], s5_b_re[layer], s5_b_im[layer],
            s5_c_re[layer], s5_c_im[layer], s5_d[layer])
        sin = _s5_state_in(u, w_in_state)
        carry = _s5_scan(sin, a16, n_chunks)
        y = _s5_out(u, carry, w_toep, w_out_state)
        yb = _attention(q.reshape(t, ATT_WIDTH), kv.reshape(t, 2 * KV_WIDTH),
                        attn_sink[layer].astype(F32) * LOG2E, bias, bsz, seq)
        h = _merge(h, y, yb.reshape(bsz, seq, ATT_WIDTH), g1, unperm, w_gates, s5_w_glu[layer].astype(BF16),
                   w_branch_a[layer].astype(BF16), w_branch_b[layer].astype(BF16), w_out[layer].astype(BF16))
        h = _ffn(h.reshape(t, D_MODEL), g2, gf, ffn_w_gate[layer].astype(BF16), ffn_w_up[layer].astype(BF16),
                 ffn_w_down[layer].astype(BF16), final_norm=(layer == depth - 1)).reshape(bsz, seq, D_MODEL)
    return h
```

```python
import functools
import math

import jax
import jax.numpy as jnp
from jax import lax
from jax.experimental import pallas as pl
from jax.experimental.pallas import tpu as pltpu

F32 = jnp.float32
BF16 = jnp.bfloat16

D_MODEL = 1024
S5_WIDTH = 512
S5_GROUP = 16
S5_GROUPS = 32
S5_STATE = 64
HEAD_DIM = 64
N_Q_HEADS = 8
N_KV_HEADS = 2
Q_PER_KV = N_Q_HEADS // N_KV_HEADS
ATT_WIDTH = N_Q_HEADS * HEAD_DIM
KV_WIDTH = N_KV_HEADS * HEAD_DIM
WINDOW = 128
BLOCK = 128
NUM_BUCKETS = 32
MAX_DISTANCE = 128
D_FF = 2816
RMS_EPS = 1e-6
NEG_INF = -1e30
LOG2E = math.log2(math.e)

LANES = 128
SUBLANES = 8
CHUNK = 16
GROUPS_PER_SLAB = LANES // S5_GROUP
N_SLABS = S5_WIDTH // LANES
PAIRS_PER_SLAB = GROUPS_PER_SLAB // 2
GROUP_K = CHUNK * S5_GROUP
STATE_COLS = 2 * 2 * GROUPS_PER_SLAB * S5_STATE
SCAN_TILE = STATE_COLS // 2
TOK_TILE = 64
TILE_ROWS = SUBLANES * TOK_TILE
S5_ROWS = 8 * TILE_ROWS
FF_TILE = 256
ATT_QB = 8
VMEM_LIMIT = 56 * 1024 * 1024


def _rms(x, g):
    return x * lax.rsqrt(jnp.mean(x * x, axis=-1, keepdims=True) + RMS_EPS) * g


def _gelu_tanh(x):
    return 0.5 * x * (1.0 + jnp.tanh(math.sqrt(2.0 / math.pi) * (x + 0.044715 * (x * x * x))))


def _sigmoid(x):
    return 1.0 / (1.0 + jnp.exp(-x))


def _dot(a, b):
    return jnp.dot(a, b, preferred_element_type=F32)


def _const_spec(shape):
    nd = len(shape)
    return pl.BlockSpec(shape, lambda *_: (0,) * nd, pipeline_mode=pl.Buffered(1))


def _params(sem):
    return pltpu.CompilerParams(dimension_semantics=sem, vmem_limit_bytes=VMEM_LIMIT)


def _to_chunk_order():
    cpt = TOK_TILE // CHUNK
    dst = jnp.arange(TILE_ROWS)
    t, c, b = dst // (cpt * SUBLANES), (dst // SUBLANES) % cpt, dst % SUBLANES
    src = b * TOK_TILE + c * CHUNK + t
    return (src[:, None] == jnp.arange(TILE_ROWS)[None, :]).astype(BF16)


def _inproj_kernel(x_ref, g_ref, w_ref, perm_ref, u_ref, q_ref, kv_ref):
    bsz = x_ref.shape[0]
    hn = _rms(x_ref[...].reshape(TILE_ROWS, D_MODEL), g_ref[...]).astype(BF16)
    r = _dot(hn, w_ref[...])
    u_ref[...] = _dot(perm_ref[...], r[:, :S5_WIDTH].astype(BF16)).astype(BF16)
    q_ref[...] = r[:, S5_WIDTH:S5_WIDTH + ATT_WIDTH].astype(BF16).reshape(bsz, TOK_TILE, ATT_WIDTH)
    kv_ref[...] = r[:, S5_WIDTH + ATT_WIDTH:].astype(BF16).reshape(bsz, TOK_TILE, 2 * KV_WIDTH)


def _inproj(h3, g, w, perm):
    bsz, seq, _ = h3.shape
    assert bsz * TOK_TILE == TILE_ROWS
    n = w.shape[1]
    tok = lambda width: pl.BlockSpec((bsz, TOK_TILE, width), lambda i: (0, i, 0))
    return pl.pallas_call(
        _inproj_kernel,
        grid=(seq // TOK_TILE,),
        in_specs=[tok(D_MODEL), _const_spec((1, D_MODEL)), _const_spec((D_MODEL, n)),
                  _const_spec((TILE_ROWS, TILE_ROWS))],
        out_specs=[pl.BlockSpec((TILE_ROWS, S5_WIDTH), lambda i: (i, 0)), tok(ATT_WIDTH), tok(2 * KV_WIDTH)],
        out_shape=[jax.ShapeDtypeStruct((bsz * seq, S5_WIDTH), BF16),
                   jax.ShapeDtypeStruct((bsz, seq, ATT_WIDTH), BF16),
                   jax.ShapeDtypeStruct((bsz, seq, 2 * KV_WIDTH), BF16)],
        compiler_params=_params(("parallel",)),
        name="inproj",
    )(h3, g, w, perm)


def _block_transpose(x):
    x = list(x)
    blk = lax.broadcasted_iota(jnp.int32, x[0].shape, 1) // S5_GROUP
    for d in (4, 2, 1):
        keep = (blk & d) == 0
        for i in range(GROUPS_PER_SLAB):
            if i & d:
                continue
            xi, xj = x[i], x[i + d]
            x[i] = jnp.where(keep, xi, pltpu.roll(xj, d * S5_GROUP, 1))
            x[i + d] = jnp.where(keep, pltpu.roll(xi, LANES - d * S5_GROUP, 1), xj)
    return x


def _token_rows(t, k):
    return pl.ds(k * TILE_ROWS + t * (TILE_ROWS // CHUNK), TILE_ROWS // CHUNK)


def _group_rows(u_ref):
    tiles = u_ref.shape[0] // TILE_ROWS
    tok = [jnp.concatenate([u_ref[_token_rows(t, k), :] for k in range(tiles)], axis=0) for t in range(CHUNK)]
    lo = _block_transpose(tok[:CHUNK // 2])
    hi = _block_transpose(tok[CHUNK // 2:])
    return [jnp.concatenate([lo[a], hi[a]], axis=1) for a in range(GROUPS_PER_SLAB)]


def _state_tiles(pq):
    return [pl.ds((dr * 2 + ri) * (STATE_COLS // 4) + pq * LANES, LANES) for dr in range(2) for ri in range(2)]


def _s5_in_kernel(u_ref, w_ref, o_ref):
    g = _group_rows(u_ref)
    for pq in range(PAIRS_PER_SLAB):
        r = _dot(jnp.concatenate([g[2 * pq], g[2 * pq + 1]], axis=1), w_ref[pq])
        for k, tile in enumerate(_state_tiles(pq)):
            o_ref[:, tile] = r[:, k * LANES:(k + 1) * LANES].astype(BF16)


def _s5_state_in(u, w_in_state, rows=S5_ROWS):
    t = u.shape[0]
    return pl.pallas_call(
        _s5_in_kernel,
        grid=(N_SLABS, t // rows),
        in_specs=[pl.BlockSpec((rows, LANES), lambda s, i: (i, s)),
                  pl.BlockSpec((None,) + w_in_state.shape[1:], lambda s, i: (s, 0, 0, 0))],
        out_specs=pl.BlockSpec((rows // CHUNK, STATE_COLS), lambda s, i: (i, s)),
        out_shape=jax.ShapeDtypeStruct((t // CHUNK, N_SLABS * STATE_COLS), BF16),
        compiler_params=_params(("parallel", "parallel")),
        name="s5_state_in",
    )(u, w_in_state)


def _s5_scan_kernel(sin_ref, a_ref, o_ref, st_ref, *, cpt):
    half = SCAN_TILE // 2
    pair_rows = 2 * SUBLANES

    @pl.when(pl.program_id(2) == 0)
    def _():
        st_ref[...] = jnp.zeros_like(st_ref)

    ar = a_ref[:, :half]
    ai = a_ref[:, half:]

    def advance(sr, si, xr, xi):
        return ar * sr - ai * si + xr, ar * si + ai * sr + xi

    def run(reverse):
        lower, upper = slice(0, SUBLANES), slice(SUBLANES, pair_rows)
        first, second = (upper, lower) if reverse else (lower, upper)

        def body(i, carry):
            s0r, s0i = carry
            c2 = (cpt // 2 - 1 - i) if reverse else i
            rows = pl.ds(pl.multiple_of(c2 * pair_rows, pair_rows), pair_rows)
            x = sin_ref[rows, :].astype(F32)
            s1r, s1i = advance(s0r, s0i, x[first, :half], x[first, half:])
            s2r, s2i = advance(s1r, s1i, x[second, :half], x[second, half:])
            enter_r, enter_i = ((s1r, s0r), (s1i, s0i)) if reverse else ((s0r, s1r), (s0i, s1i))
            o_ref[rows, :half] = jnp.concatenate(enter_r, axis=0).astype(BF16)
            o_ref[rows, half:] = jnp.concatenate(enter_i, axis=0).astype(BF16)
            return s2r, s2i

        sr, si = lax.fori_loop(0, cpt // 2, body, (st_ref[:, :half], st_ref[:, half:]), unroll=2)
        st_ref[:, :half] = sr
        st_ref[:, half:] = si

    backward = pl.program_id(1) == 1
    pl.when(backward)(lambda: run(True))
    pl.when(jnp.logical_not(backward))(lambda: run(False))


def _s5_scan(sin, a16, n_chunks, cpt=64):
    nct = n_chunks // cpt
    blk = lambda s, d, k: (k + d * (nct - 1 - 2 * k), s * 2 + d)
    return pl.pallas_call(
        functools.partial(_s5_scan_kernel, cpt=cpt),
        grid=(N_SLABS, 2, nct),
        in_specs=[pl.BlockSpec((cpt * SUBLANES, SCAN_TILE), blk),
                  pl.BlockSpec((SUBLANES, SCAN_TILE), lambda s, d, k: (0, s * 2 + d))],
        out_specs=pl.BlockSpec((cpt * SUBLANES, SCAN_TILE), blk),
        out_shape=jax.ShapeDtypeStruct(sin.shape, BF16),
        scratch_shapes=[pltpu.VMEM((SUBLANES, SCAN_TILE), F32)],
        compiler_params=_params(("parallel", "parallel", "arbitrary")),
        name="s5_scan",
    )(sin, a16)


def _s5_out_kernel(u_ref, c_ref, wt_ref, wo_ref, y_ref):
    g = _group_rows(u_ref)
    ys = []
    for pq in range(PAIRS_PER_SLAB):
        carry = jnp.concatenate([c_ref[:, tile] for tile in _state_tiles(pq)], axis=1)
        from_state = _dot(carry, wo_ref[pq])
        for a2 in range(2):
            a = 2 * pq + a2
            ys.append((_dot(g[a], wt_ref[a]) + from_state[:, a2 * GROUP_K:(a2 + 1) * GROUP_K]).astype(BF16))
    halves = (_block_transpose([y[:, :LANES] for y in ys]), _block_transpose([y[:, LANES:] for y in ys]))
    per_tile = TILE_ROWS // CHUNK
    for t in range(CHUNK):
        tok = halves[t // (CHUNK // 2)][t % (CHUNK // 2)]
        for k in range(y_ref.shape[0] // TILE_ROWS):
            y_ref[_token_rows(t, k), :] = tok[k * per_tile:(k + 1) * per_tile, :]


def _s5_out(u, carry, w_toep, w_out_state, rows=S5_ROWS):
    t = u.shape[0]
    return pl.pallas_call(
        _s5_out_kernel,
        grid=(N_SLABS, t // rows),
        in_specs=[pl.BlockSpec((rows, LANES), lambda s, i: (i, s)),
                  pl.BlockSpec((rows // CHUNK, STATE_COLS), lambda s, i: (i, s)),
                  pl.BlockSpec((None,) + w_toep.shape[1:], lambda s, i: (s, 0, 0, 0)),
                  pl.BlockSpec((None,) + w_out_state.shape[1:], lambda s, i: (s, 0, 0, 0))],
        out_specs=pl.BlockSpec((rows, LANES), lambda s, i: (i, s)),
        out_shape=jax.ShapeDtypeStruct((t, S5_WIDTH), BF16),
        compiler_params=_params(("parallel", "parallel")),
        name="s5_out",
    )(u, carry, w_toep, w_out_state)


def _s5_weights(lam_re, lam_im, log_dt, b_re, b_im, c_re, c_im, d):
    G, P, H, C, S = S5_GROUPS, S5_STATE, S5_GROUP, CHUNK, GROUPS_PER_SLAB
    lr = lam_re.astype(F32)
    li = lam_im.astype(F32)
    dt = jnp.exp(log_dt.astype(F32))[..., None]
    mag = jnp.exp(lr * dt)
    ab_re = mag * jnp.cos(li * dt)
    ab_im = mag * jnp.sin(li * dt)
    nr = ab_re - 1.0
    den = lr * lr + li * li
    coef_re = (nr * lr + ab_im * li) / den
    coef_im = (ab_im * lr - nr * li) / den
    br = b_re.astype(F32)
    bi = b_im.astype(F32)
    bb_re = coef_re[..., None] * br - coef_im[..., None] * bi
    bb_im = coef_re[..., None] * bi + coef_im[..., None] * br
    cr = c_re.astype(F32)
    ci = c_im.astype(F32)

    k = jnp.arange(C + 1, dtype=F32)[:, None, None, None]
    pmag = jnp.exp(k * (lr * dt))
    pw_re = pmag * jnp.cos(k * (li * dt))
    pw_im = pmag * jnp.sin(k * (li * dt))

    n_pairs = G // 2
    pw = jnp.stack([pw_re, pw_im], axis=0)
    up = jnp.arange(C)

    def table(k_fwd, k_bwd):
        return jnp.stack([pw[:, k_fwd, 0], pw[:, k_bwd, 1]], axis=0).transpose(3, 0, 1, 2, 4)

    by_state = lambda tab: tab.transpose(0, 1, 2, 4, 3).reshape(n_pairs, 2, 2, 2, P, C)
    lag_pw = by_state(table(up, C - 1 - up))
    out_pw = by_state(table(up + 1, C - up))
    in_pw = table(C - 1 - up, up).reshape(n_pairs, 2, 2, 2, C, P)
    in_pw = jnp.tile(in_pw, (1, 1, 1, 1, 1, 2))
    ct = jnp.stack([cr, ci], axis=0).transpose(2, 1, 0, 4, 3).reshape(n_pairs, 2, 2, 2, P, H)
    bbt = jnp.stack([bb_re, bb_im], axis=0).transpose(2, 1, 0, 4, 3).reshape(n_pairs, 2, 2, 2, H, P)
    own = jnp.eye(2, dtype=F32)[None, :, None, None, None, :, None]
    bb_own = (bbt[:, :, :, :, :, None, :] * own).reshape(n_pairs, 2, 2, 2, H, 2 * P)
    skip = jnp.eye(H, dtype=F32)[None] * d.astype(F32).reshape(G, 1, H)
    skip = jnp.pad(skip, ((0, 0), (0, 0), ((C - 1) * H, 0))).reshape(n_pairs, 2, H, GROUP_K)
    rep_h = jnp.tile(jnp.eye(H, dtype=F32), (1, C))
    rep_k = jnp.repeat(jnp.eye(C, dtype=F32), H, axis=1)
    w_toep, w_in_state, w_out_state = _s5_pack(ct, bbt, bb_own, lag_pw, out_pw, in_pw, skip, rep_h, rep_k)

    a16 = jnp.stack([pw_re[C], pw_im[C]], axis=0)
    a16 = a16.reshape(2, 2, N_SLABS, S, P).transpose(2, 1, 0, 3, 4)
    a16 = jnp.broadcast_to(a16.reshape(1, N_SLABS * STATE_COLS), (SUBLANES, N_SLABS * STATE_COLS))
    return w_in_state, w_toep, w_out_state, a16


def _s5_pack_kernel(ct_ref, bbt_ref, bbo_ref, lagp_ref, outp_ref, inp_ref, skip_ref, reph_ref, repk_ref,
                    wt_ref, wi_ref, wo_ref):
    hp = lax.Precision.HIGHEST
    exact_dot = lambda a, b: jnp.dot(a, b, precision=hp, preferred_element_type=F32)
    rep_h = reph_ref[...]
    rep_k = repk_ref[...]
    wide = 2 * GROUP_K
    for a2 in range(2):
        lag_part = []
        for dr in range(2):
            c_re = exact_dot(ct_ref[a2, dr, 0], rep_h)
            c_im = exact_dot(ct_ref[a2, dr, 1], rep_h)

            def times_power(ref):
                p_re = exact_dot(ref[a2, dr, 0], rep_k)
                p_im = exact_dot(ref[a2, dr, 1], rep_k)
                return c_re * p_re - c_im * p_im, c_re * p_im + c_im * p_re

            l_re, l_im = times_power(lagp_ref)
            lag_part.append(exact_dot(bbt_ref[a2, dr, 0], l_re) - exact_dot(bbt_ref[a2, dr, 1], l_im))
            w_re, w_im = times_power(outp_ref)
            for ri, val in ((0, w_re), (1, -w_im)):
                rows = pl.ds((dr * 2 + ri) * LANES + a2 * S5_STATE, S5_STATE)
                wo_ref[rows, a2 * GROUP_K:(a2 + 1) * GROUP_K] = val.astype(BF16)
                wo_ref[rows, (1 - a2) * GROUP_K:(2 - a2) * GROUP_K] = jnp.zeros((S5_STATE, GROUP_K), BF16)
            b_re = bbo_ref[a2, dr, 0]
            b_im = bbo_ref[a2, dr, 1]
            for t in range(CHUNK):
                e_re = inp_ref[a2, dr, 0, t:t + 1, :]
                e_im = inp_ref[a2, dr, 1, t:t + 1, :]
                rows = pl.ds(a2 * GROUP_K + t * S5_GROUP, S5_GROUP)
                wi_ref[rows, (2 * dr) * LANES:(2 * dr + 1) * LANES] = (b_re * e_re - b_im * e_im).astype(BF16)
                wi_ref[rows, (2 * dr + 1) * LANES:(2 * dr + 2) * LANES] = (b_re * e_im + b_im * e_re).astype(BF16)
        zero = jnp.zeros((S5_GROUP, GROUP_K), F32)
        lags = (jnp.concatenate([lag_part[1] + skip_ref[a2], zero], axis=1)
                + pltpu.roll(jnp.concatenate([lag_part[0], zero], axis=1), (CHUNK - 1) * S5_GROUP, 1))
        for t in range(CHUNK):
            shift = (CHUNK - 1 - t) * S5_GROUP
            window = lags if shift == 0 else pltpu.roll(lags, wide - shift, 1)
            wt_ref[a2, t * S5_GROUP:(t + 1) * S5_GROUP, :] = window[:, :GROUP_K].astype(BF16)


def _s5_pack(ct, bbt, bb_own, lag_pw, out_pw, in_pw, skip, rep_h, rep_k):
    n_pairs = ct.shape[0]
    per_pair = lambda a: pl.BlockSpec((None,) + a.shape[1:], lambda i: (i,) + (0,) * (a.ndim - 1))
    tabs = (ct, bbt, bb_own, lag_pw, out_pw, in_pw, skip)
    wt, wi, wo = pl.pallas_call(
        _s5_pack_kernel,
        grid=(n_pairs,),
        in_specs=[per_pair(a) for a in tabs] + [_const_spec(rep_h.shape), _const_spec(rep_k.shape)],
        out_specs=[pl.BlockSpec((2, GROUP_K, GROUP_K), lambda i: (i, 0, 0)),
                   pl.BlockSpec((None, 2 * GROUP_K, 4 * LANES), lambda i: (i, 0, 0)),
                   pl.BlockSpec((None, 4 * LANES, 2 * GROUP_K), lambda i: (i, 0, 0))],
        out_shape=[jax.ShapeDtypeStruct((2 * n_pairs, GROUP_K, GROUP_K), BF16),
                   jax.ShapeDtypeStruct((n_pairs, 2 * GROUP_K, 4 * LANES), BF16),
                   jax.ShapeDtypeStruct((n_pairs, 4 * LANES, 2 * GROUP_K), BF16)],
        compiler_params=_params(("parallel",)),
        name="s5_pack",
    )(*tabs, rep_h, rep_k)
    return (wt.reshape(N_SLABS, GROUPS_PER_SLAB, GROUP_K, GROUP_K),
            wi.reshape(N_SLABS, PAIRS_PER_SLAB, 2 * GROUP_K, 4 * LANES),
            wo.reshape(N_SLABS, PAIRS_PER_SLAB, 4 * LANES, 2 * GROUP_K))


def _attn_kernel(sink_ref, q_ref, kl_ref, kc_ref, kr_ref, bias_first_ref, bias_mid_ref, bias_last_ref, o_ref):
    nk = 3 * BLOCK
    kv_all = jnp.concatenate([kl_ref[...], kc_ref[...], kr_ref[...]], axis=0)
    low = lax.broadcasted_iota(jnp.int32, (kv_all.shape[0], LANES), 1) < HEAD_DIM
    zero = jnp.zeros((kv_all.shape[0], LANES), BF16)

    def diag_parts(tile, kvh):
        other = pltpu.roll(tile, HEAD_DIM, 1)
        first, second = (tile, other) if kvh == 0 else (other, tile)
        return jnp.where(low, first, zero), jnp.where(low, zero, second)

    def window(parts, qb):
        return jnp.concatenate([p[qb * BLOCK:qb * BLOCK + nk] for p in parts], axis=0)

    top = lax.broadcasted_iota(jnp.int32, (2 * nk, LANES), 0) < nk
    ones_bd = (top == (lax.broadcasted_iota(jnp.int32, (2 * nk, LANES), 1) < HEAD_DIM)).astype(BF16)

    upper = lax.broadcasted_iota(jnp.int32, (2 * BLOCK, 1), 0) < BLOCK
    bias_refs = [bias_first_ref] + [bias_mid_ref] * (ATT_QB - 2) + [bias_last_ref]
    for kvh in range(N_KV_HEADS):
        k_parts = diag_parts(kv_all[:, :LANES], kvh)
        v_parts = diag_parts(kv_all[:, LANES:], kvh)
        for qb, bias_ref in enumerate(bias_refs):
            rows = slice(qb * BLOCK, (qb + 1) * BLOCK)
            kbd = window(k_parts, qb)
            vbd = jnp.concatenate([window(v_parts, qb), ones_bd], axis=1)
            tiles = [(kvh * 2 + pair) * LANES for pair in range(2)]
            q2 = jnp.concatenate([q_ref[rows, tl:tl + LANES] for tl in tiles], axis=0)
            s = lax.dot_general(q2, kbd, (((1,), (1,)), ((), ())), preferred_element_type=F32)
            s = s + bias_ref[kvh]
            ps, sinks = [], []
            for e in range(2):
                se = s[:, e * nk:(e + 1) * nk]
                sk = jnp.where(upper, sink_ref[4 * kvh + e], sink_ref[4 * kvh + 2 + e])
                mx = jnp.maximum(jnp.max(se, axis=-1, keepdims=True), sk)
                ps.append(jnp.exp2((se - mx).astype(BF16)))
                sinks.append(jnp.broadcast_to(jnp.exp2(sk - mx), (2 * BLOCK, HEAD_DIM)))
            nd = _dot(jnp.concatenate(ps, axis=1), vbd)
            o = (nd[:, :LANES] / (nd[:, LANES:] + jnp.concatenate(sinks, axis=1))).astype(BF16)
            for pair, tl in enumerate(tiles):
                o_ref[rows, tl:tl + LANES] = o[pair * BLOCK:(pair + 1) * BLOCK, :]


def _attention(q, kv, sink, bias, bsz, seq):
    nb = seq // BLOCK
    assert nb % ATT_QB == 0 and ATT_QB >= 2
    nj = nb // ATT_QB
    rows = ATT_QB * BLOCK
    centre = lambda b, j: (b * nj + j, 0)
    left = lambda b, j: (b * nb + jnp.maximum(ATT_QB * j - 1, 0), 0)
    right = lambda b, j: (b * nb + jnp.minimum(ATT_QB * (j + 1), nb - 1), 0)
    variant = (None,) + bias.shape[1:]
    bias_first = pl.BlockSpec(variant, lambda b, j: (jnp.where(j == 0, 0, 1), 0, 0, 0))
    bias_mid = pl.BlockSpec(variant, lambda b, j: (1, 0, 0, 0), pipeline_mode=pl.Buffered(1))
    bias_last = pl.BlockSpec(variant, lambda b, j: (jnp.where(j == nj - 1, 2, 1), 0, 0, 0))
    return pl.pallas_call(
        _attn_kernel,
        grid=(bsz, nj),
        in_specs=[pl.BlockSpec(memory_space=pltpu.SMEM),
                  pl.BlockSpec((rows, ATT_WIDTH), centre),
                  pl.BlockSpec((BLOCK, 2 * KV_WIDTH), left),
                  pl.BlockSpec((rows, 2 * KV_WIDTH), centre),
                  pl.BlockSpec((BLOCK, 2 * KV_WIDTH), right),
                  bias_first, bias_mid, bias_last],
        out_specs=pl.BlockSpec((rows, ATT_WIDTH), centre),
        out_shape=jax.ShapeDtypeStruct((bsz * seq, ATT_WIDTH), BF16),
        compiler_params=_params(("parallel", "parallel")),
        name="attention",
    )(sink, q, kv, kv, kv, bias, bias, bias)


def _t5_bucket(rel):
    half = NUM_BUCKETS // 2
    max_exact = half // 2
    ret = jnp.where(rel > 0, half, 0)
    n = jnp.abs(rel)
    nf = jnp.maximum(n, 1).astype(jnp.float32)
    large = max_exact + (jnp.log(nf / max_exact) / math.log(MAX_DISTANCE / max_exact)
                         * (half - max_exact)).astype(jnp.int32)
    large = jnp.minimum(large, half - 1)
    return ret + jnp.where(n < max_exact, n, large)


def _band_bias(rel_bias):
    q_loc = jnp.arange(BLOCK, dtype=jnp.int32)
    k_loc = jnp.arange(3 * BLOCK, dtype=jnp.int32)
    rel = (k_loc[None, :] - BLOCK) - q_loc[:, None]
    onehot = (_t5_bucket(rel)[None] == jnp.arange(NUM_BUCKETS, dtype=jnp.int32)[:, None, None]).astype(F32)
    bias = jnp.einsum('bh,bqk->hqk', rel_bias.astype(F32), onehot, precision=lax.Precision.HIGHEST)
    bias = jnp.where((jnp.abs(rel) <= WINDOW)[None], bias * LOG2E, NEG_INF)
    keep = jnp.stack([k_loc >= BLOCK, k_loc >= 0, k_loc < 2 * BLOCK], axis=0)
    bias = jnp.where(keep[:, None, None, :], bias[None], NEG_INF)
    bias = bias.reshape(3, N_Q_HEADS // 2, 2, BLOCK, 3 * BLOCK).transpose(0, 1, 3, 2, 4)
    return bias.reshape(3, N_KV_HEADS, 2 * BLOCK, 6 * BLOCK)


def _merge_kernel(h_ref, y_ref, yb_ref, g1_ref, unperm_ref, wg_ref, wglu_ref, wa_ref, wb_ref, wo_ref, o_ref):
    bsz = h_ref.shape[0]
    h = h_ref[...].reshape(TILE_ROWS, D_MODEL)
    hn = _rms(h, g1_ref[...]).astype(BF16)
    gates = _sigmoid(_dot(hn, wg_ref[...]))
    z = _gelu_tanh(_dot(unperm_ref[...], y_ref[...]))
    za = (z * _sigmoid(_dot(z.astype(BF16), wglu_ref[...]))).astype(BF16)
    yb = yb_ref[...].reshape(TILE_ROWS, ATT_WIDTH)
    merged = gates[:, :D_MODEL] * _dot(za, wa_ref[...]) + gates[:, D_MODEL:] * _dot(yb, wb_ref[...])
    o_ref[...] = (h + _dot(merged.astype(BF16), wo_ref[...])).reshape(bsz, TOK_TILE, D_MODEL)


def _merge(h3, y, yb3, g1, unperm, w_gates, w_glu, w_a, w_b, w_o):
    bsz, seq, _ = h3.shape
    tok = lambda width: pl.BlockSpec((bsz, TOK_TILE, width), lambda i: (0, i, 0))
    return pl.pallas_call(
        _merge_kernel,
        grid=(seq // TOK_TILE,),
        in_specs=[tok(D_MODEL), pl.BlockSpec((TILE_ROWS, S5_WIDTH), lambda i: (i, 0)), tok(ATT_WIDTH),
                  _const_spec((1, D_MODEL)), _const_spec(unperm.shape),
                  _const_spec(w_gates.shape), _const_spec(w_glu.shape), _const_spec(w_a.shape),
                  _const_spec(w_b.shape), _const_spec(w_o.shape)],
        out_specs=tok(D_MODEL),
        out_shape=jax.ShapeDtypeStruct((bsz, seq, D_MODEL), F32),
        compiler_params=_params(("parallel",)),
        name="merge",
    )(h3, y, yb3, g1, unperm, w_gates, w_glu, w_a, w_b, w_o)


def _ffn_kernel(h_ref, g2_ref, gf_ref, wg_ref, wu_ref, wd_ref, o_ref, acc_ref, *, final_norm):
    h = h_ref[...]
    hn = _rms(h, g2_ref[...]).astype(BF16)
    acc_ref[...] = h
    for j in range(D_FF // FF_TILE):
        cols = slice(j * FF_TILE, (j + 1) * FF_TILE)
        gate = _dot(hn, wg_ref[:, cols])
        up = _dot(hn, wu_ref[:, cols])
        act = (gate * _sigmoid(gate) * up).astype(BF16)
        acc_ref[...] += _dot(act, wd_ref[cols, :])
    out = acc_ref[...]
    o_ref[...] = _rms(out, gf_ref[...]) if final_norm else out


def _ffn(h, g2, gf, w_gate, w_up, w_down, final_norm, tm=512):
    t = h.shape[0]
    return pl.pallas_call(
        functools.partial(_ffn_kernel, final_norm=final_norm),
        grid=(t // tm,),
        in_specs=[pl.BlockSpec((tm, D_MODEL), lambda i: (i, 0)),
                  _const_spec((1, D_MODEL)), _const_spec((1, D_MODEL)),
                  _const_spec(w_gate.shape), _const_spec(w_up.shape), _const_spec(w_down.shape)],
        out_specs=pl.BlockSpec((tm, D_MODEL), lambda i: (i, 0)),
        out_shape=jax.ShapeDtypeStruct((t, D_MODEL), F32),
        scratch_shapes=[pltpu.VMEM((tm, D_MODEL), F32)],
        compiler_params=_params(("parallel",)),
        name="ffn",
    )(h, g2, gf, w_gate, w_up, w_down)


def kernel(x, norm1_g, norm2_g, final_g, w_in, s5_lambda_re, s5_lambda_im, s5_log_dt, s5_b_re, s5_b_im,
           s5_c_re, s5_c_im, s5_d, s5_w_glu, attn_sink, rel_bias, w_branch_a, w_branch_b, w_out,
           ffn_w_gate, ffn_w_up, ffn_w_down):
    bsz, seq, _ = x.shape
    depth = w_in.shape[0]
    assert seq % BLOCK == 0 and seq % CHUNK == 0 and bsz == SUBLANES
    n_chunks = seq // CHUNK
    t = bsz * seq
    bias = _band_bias(rel_bias)
    o_k = S5_WIDTH + ATT_WIDTH
    o_g = o_k + 2 * KV_WIDTH
    col_scale = jnp.concatenate([jnp.ones((S5_WIDTH,), F32), jnp.full((ATT_WIDTH,), LOG2E * HEAD_DIM ** -0.5, F32),
                                 jnp.ones((2 * KV_WIDTH,), F32)])
    gf = final_g.reshape(1, D_MODEL).astype(F32)
    perm = _to_chunk_order()
    unperm = perm.T
    h = x
    for layer in range(depth):
        g1 = norm1_g[layer].reshape(1, D_MODEL).astype(F32)
        g2 = norm2_g[layer].reshape(1, D_MODEL).astype(F32)
        w_uqkv = (w_in[layer][:, :o_g] * col_scale).astype(BF16)
        w_gates = w_in[layer][:, o_g:].astype(BF16)
        u, q, kv = _inproj(h, g1, w_uqkv, perm)
        w_in_state, w_toep, w_out_state, a16 = _s5_weights(
            s5_lambda_re[layer], s5_lambda_im[layer], s5_log_dt[layer], s5_b_re[layer], s5_b_im[layer],
            s5_c_re[layer], s5_c_im[layer], s5_d[layer])
        sin = _s5_state_in(u, w_in_state)
        carry = _s5_scan(sin, a16, n_chunks)
        y = _s5_out(u, carry, w_toep, w_out_state)
        yb = _attention(q.reshape(t, ATT_WIDTH), kv.reshape(t, 2 * KV_WIDTH),
                        attn_sink[layer].astype(F32) * LOG2E, bias, bsz, seq)
        h = _merge(h, y, yb.reshape(bsz, seq, ATT_WIDTH), g1, unperm, w_gates, s5_w_glu[layer].astype(BF16),
                   w_branch_a[layer].astype(BF16), w_branch_b[layer].astype(BF16), w_out[layer].astype(BF16))
        h = _ffn(h.reshape(t, D_MODEL), g2, gf, ffn_w_gate[layer].astype(BF16), ffn_w_up[layer].astype(BF16),
                 ffn_w_down[layer].astype(BF16), final_norm=(layer == depth - 1)).reshape(bsz, seq, D_MODEL)
    return h
```

```python
import functools
import math

import jax
import jax.numpy as jnp
from jax import lax
from jax.experimental import pallas as pl
from jax.experimental.pallas import tpu as pltpu

F32 = jnp.float32
BF16 = jnp.bfloat16

D_MODEL = 1024
S5_WIDTH = 512
S5_GROUP = 16
S5_GROUPS = 32
S5_STATE = 64
HEAD_DIM = 64
N_Q_HEADS = 8
N_KV_HEADS = 2
Q_PER_KV = N_Q_HEADS // N_KV_HEADS
ATT_WIDTH = N_Q_HEADS * HEAD_DIM
KV_WIDTH = N_KV_HEADS * HEAD_DIM
WINDOW = 128
BLOCK = 128
NUM_BUCKETS = 32
MAX_DISTANCE = 128
D_FF = 2816
RMS_EPS = 1e-6
NEG_INF = -1e30
LOG2E = math.log2(math.e)

LANES = 128
SUBLANES = 8
CHUNK = 16
GROUPS_PER_SLAB = LANES // S5_GROUP
N_SLABS = S5_WIDTH // LANES
PAIRS_PER_SLAB = GROUPS_PER_SLAB // 2
GROUP_K = CHUNK * S5_GROUP
STATE_COLS = 2 * 2 * GROUPS_PER_SLAB * S5_STATE
SCAN_TILE = STATE_COLS // 2
TOK_TILE = 64
TILE_ROWS = SUBLANES * TOK_TILE
S5_ROWS = 8 * TILE_ROWS
FF_TILE = 256
ATT_QB = 8
VMEM_LIMIT = 56 * 1024 * 1024


def _rms(x, g):
    return x * lax.rsqrt(jnp.mean(x * x, axis=-1, keepdims=True) + RMS_EPS) * g


def _gelu_tanh(x):
    return 0.5 * x * (1.0 + jnp.tanh(math.sqrt(2.0 / math.pi) * (x + 0.044715 * (x * x * x))))


def _sigmoid(x):
    return 1.0 / (1.0 + jnp.exp(-x))


def _dot(a, b):
    return jnp.dot(a, b, preferred_element_type=F32)


def _const_spec(shape):
    nd = len(shape)
    return pl.BlockSpec(shape, lambda *_: (0,) * nd, pipeline_mode=pl.Buffered(1))


def _params(sem):
    return pltpu.CompilerParams(dimension_semantics=sem, vmem_limit_bytes=VMEM_LIMIT)


def _to_chunk_order():
    cpt = TOK_TILE // CHUNK
    dst = jnp.arange(TILE_ROWS)
    t, c, b = dst // (cpt * SUBLANES), (dst // SUBLANES) % cpt, dst % SUBLANES
    src = b * TOK_TILE + c * CHUNK + t
    return (src[:, None] == jnp.arange(TILE_ROWS)[None, :]).astype(BF16)


def _inproj_kernel(x_ref, g_ref, w_ref, perm_ref, u_ref, q_ref, kv_ref):
    bsz = x_ref.shape[0]
    hn = _rms(x_ref[...].reshape(TILE_ROWS, D_MODEL), g_ref[...]).astype(BF16)
    r = _dot(hn, w_ref[...])
    u_ref[...] = _dot(perm_ref[...], r[:, :S5_WIDTH].astype(BF16)).astype(BF16)
    q_ref[...] = r[:, S5_WIDTH:S5_WIDTH + ATT_WIDTH].astype(BF16).reshape(bsz, TOK_TILE, ATT_WIDTH)
    kv_ref[...] = r[:, S5_WIDTH + ATT_WIDTH:].astype(BF16).reshape(bsz, TOK_TILE, 2 * KV_WIDTH)


def _inproj(h3, g, w, perm):
    bsz, seq, _ = h3.shape
    assert bsz * TOK_TILE == TILE_ROWS
    n = w.shape[1]
    tok = lambda width: pl.BlockSpec((bsz, TOK_TILE, width), lambda i: (0, i, 0))
    return pl.pallas_call(
        _inproj_kernel,
        grid=(seq // TOK_TILE,),
        in_specs=[tok(D_MODEL), _const_spec((1, D_MODEL)), _const_spec((D_MODEL, n)),
                  _const_spec((TILE_ROWS, TILE_ROWS))],
        out_specs=[pl.BlockSpec((TILE_ROWS, S5_WIDTH), lambda i: (i, 0)), tok(ATT_WIDTH), tok(2 * KV_WIDTH)],
        out_shape=[jax.ShapeDtypeStruct((bsz * seq, S5_WIDTH), BF16),
                   jax.ShapeDtypeStruct((bsz, seq, ATT_WIDTH), BF16),
                   jax.ShapeDtypeStruct((bsz, seq, 2 * KV_WIDTH), BF16)],
        compiler_params=_params(("parallel",)),
        name="inproj",
    )(h3, g, w, perm)


def _block_transpose(x):
    x = list(x)
    blk = lax.broadcasted_iota(jnp.int32, x[0].shape, 1) // S5_GROUP
    for d in (4, 2, 1):
        keep = (blk & d) == 0
        for i in range(GROUPS_PER_SLAB):
            if i & d:
                continue
            xi, xj = x[i], x[i + d]
            x[i] = jnp.where(keep, xi, pltpu.roll(xj, d * S5_GROUP, 1))
            x[i + d] = jnp.where(keep, pltpu.roll(xi, LANES - d * S5_GROUP, 1), xj)
    return x


def _token_rows(t, k):
    return pl.ds(k * TILE_ROWS + t * (TILE_ROWS // CHUNK), TILE_ROWS // CHUNK)


def _group_rows(u_ref):
    tiles = u_ref.shape[0] // TILE_ROWS
    tok = [jnp.concatenate([u_ref[_token_rows(t, k), :] for k in range(tiles)], axis=0) for t in range(CHUNK)]
    lo = _block_transpose(tok[:CHUNK // 2])
    hi = _block_transpose(tok[CHUNK // 2:])
    return [jnp.concatenate([lo[a], hi[a]], axis=1) for a in range(GROUPS_PER_SLAB)]


def _state_tiles(pq):
    return [pl.ds((dr * 2 + ri) * (STATE_COLS // 4) + pq * LANES, LANES) for dr in range(2) for ri in range(2)]


def _s5_in_kernel(u_ref, w_ref, o_ref):
    g = _group_rows(u_ref)
    for pq in range(PAIRS_PER_SLAB):
        r = _dot(jnp.concatenate([g[2 * pq], g[2 * pq + 1]], axis=1), w_ref[pq])
        for k, tile in enumerate(_state_tiles(pq)):
            o_ref[:, tile] = r[:, k * LANES:(k + 1) * LANES].astype(BF16)


def _s5_state_in(u, w_in_state, rows=S5_ROWS):
    t = u.shape[0]
    return pl.pallas_call(
        _s5_in_kernel,
        grid=(N_SLABS, t // rows),
        in_specs=[pl.BlockSpec((rows, LANES), lambda s, i: (i, s)),
                  pl.BlockSpec((None,) + w_in_state.shape[1:], lambda s, i: (s, 0, 0, 0))],
        out_specs=pl.BlockSpec((rows // CHUNK, STATE_COLS), lambda s, i: (i, s)),
        out_shape=jax.ShapeDtypeStruct((t // CHUNK, N_SLABS * STATE_COLS), BF16),
        compiler_params=_params(("parallel", "parallel")),
        name="s5_state_in",
    )(u, w_in_state)


def _s5_scan_kernel(sin_ref, a_ref, o_ref, st_ref, *, cpt):
    half = SCAN_TILE // 2
    pair_rows = 2 * SUBLANES

    @pl.when(pl.program_id(2) == 0)
    def _():
        st_ref[...] = jnp.zeros_like(st_ref)

    ar = a_ref[:, :half]
    ai = a_ref[:, half:]

    def advance(sr, si, xr, xi):
        return ar * sr - ai * si + xr, ar * si + ai * sr + xi

    def run(reverse):
        lower, upper = slice(0, SUBLANES), slice(SUBLANES, pair_rows)
        first, second = (upper, lower) if reverse else (lower, upper)

        def body(i, carry):
            s0r, s0i = carry
            c2 = (cpt // 2 - 1 - i) if reverse else i
            rows = pl.ds(pl.multiple_of(c2 * pair_rows, pair_rows), pair_rows)
            x = sin_ref[rows, :].astype(F32)
            s1r, s1i = advance(s0r, s0i, x[first, :half], x[first, half:])
            s2r, s2i = advance(s1r, s1i, x[second, :half], x[second, half:])
            enter_r, enter_i = ((s1r, s0r), (s1i, s0i)) if reverse else ((s0r, s1r), (s0i, s1i))
            o_ref[rows, :half] = jnp.concatenate(enter_r, axis=0).astype(BF16)
            o_ref[rows, half:] = jnp.concatenate(enter_i, axis=0).astype(BF16)
            return s2r, s2i

        sr, si = lax.fori_loop(0, cpt // 2, body, (st_ref[:, :half], st_ref[:, half:]), unroll=2)
        st_ref[:, :half] = sr
        st_ref[:, half:] = si

    backward = pl.program_id(1) == 1
    pl.when(backward)(lambda: run(True))
    pl.when(jnp.logical_not(backward))(lambda: run(False))


def _s5_scan(sin, a16, n_chunks, cpt=64):
    nct = n_chunks // cpt
    blk = lambda s, d, k: (k + d * (nct - 1 - 2 * k), s * 2 + d)
    return pl.pallas_call(
        functools.partial(_s5_scan_kernel, cpt=cpt),
        grid=(N_SLABS, 2, nct),
        in_specs=[pl.BlockSpec((cpt * SUBLANES, SCAN_TILE), blk),
                  pl.BlockSpec((SUBLANES, SCAN_TILE), lambda s, d, k: (0, s * 2 + d))],
        out_specs=pl.BlockSpec((cpt * SUBLANES, SCAN_TILE), blk),
        out_shape=jax.ShapeDtypeStruct(sin.shape, BF16),
        scratch_shapes=[pltpu.VMEM((SUBLANES, SCAN_TILE), F32)],
        compiler_params=_params(("parallel", "parallel", "arbitrary")),
        name="s5_scan",
    )(sin, a16)


def _s5_out_kernel(u_ref, c_ref, wt_ref, wo_ref, y_ref):
    g = _group_rows(u_ref)
    ys = []
    for pq in range(PAIRS_PER_SLAB):
        carry = jnp.concatenate([c_ref[:, tile] for tile in _state_tiles(pq)], axis=1)
        from_state = _dot(carry, wo_ref[pq])
        for a2 in range(2):
            a = 2 * pq + a2
            ys.append((_dot(g[a], wt_ref[a]) + from_state[:, a2 * GROUP_K:(a2 + 1) * GROUP_K]).astype(BF16))
    halves = (_block_transpose([y[:, :LANES] for y in ys]), _block_transpose([y[:, LANES:] for y in ys]))
    per_tile = TILE_ROWS // CHUNK
    for t in range(CHUNK):
        tok = halves[t // (CHUNK // 2)][t % (CHUNK // 2)]
        for k in range(y_ref.shape[0] // TILE_ROWS):
            y_ref[_token_rows(t, k), :] = tok[k * per_tile:(k + 1) * per_tile, :]


def _s5_out(u, carry, w_toep, w_out_state, rows=S5_ROWS):
    t = u.shape[0]
    return pl.pallas_call(
        _s5_out_kernel,
        grid=(N_SLABS, t // rows),
        in_specs=[pl.BlockSpec((rows, LANES), lambda s, i: (i, s)),
                  pl.BlockSpec((rows // CHUNK, STATE_COLS), lambda s, i: (i, s)),
                  pl.BlockSpec((None,) + w_toep.shape[1:], lambda s, i: (s, 0, 0, 0)),
                  pl.BlockSpec((None,) + w_out_state.shape[1:], lambda s, i: (s, 0, 0, 0))],
        out_specs=pl.BlockSpec((rows, LANES), lambda s, i: (i, s)),
        out_shape=jax.ShapeDtypeStruct((t, S5_WIDTH), BF16),
        compiler_params=_params(("parallel", "parallel")),
        name="s5_out",
    )(u, carry, w_toep, w_out_state)


def _s5_weights(lam_re, lam_im, log_dt, b_re, b_im, c_re, c_im, d):
    G, P, H, C, S = S5_GROUPS, S5_STATE, S5_GROUP, CHUNK, GROUPS_PER_SLAB
    lr = lam_re.astype(F32)
    li = lam_im.astype(F32)
    dt = jnp.exp(log_dt.astype(F32))[..., None]
    mag = jnp.exp(lr * dt)
    ab_re = mag * jnp.cos(li * dt)
    ab_im = mag * jnp.sin(li * dt)
    nr = ab_re - 1.0
    den = lr * lr + li * li
    coef_re = (nr * lr + ab_im * li) / den
    coef_im = (ab_im * lr - nr * li) / den
    br = b_re.astype(F32)
    bi = b_im.astype(F32)
    bb_re = coef_re[..., None] * br - coef_im[..., None] * bi
    bb_im = coef_re[..., None] * bi + coef_im[..., None] * br
    cr = c_re.astype(F32)
    ci = c_im.astype(F32)

    k = jnp.arange(C + 1, dtype=F32)[:, None, None, None]
    pmag = jnp.exp(k * (lr * dt))
    pw_re = pmag * jnp.cos(k * (li * dt))
    pw_im = pmag * jnp.sin(k * (li * dt))

    n_pairs = G // 2
    pw = jnp.stack([pw_re, pw_im], axis=0)
    up = jnp.arange(C)

    def table(k_fwd, k_bwd):
        return jnp.stack([pw[:, k_fwd, 0], pw[:, k_bwd, 1]], axis=0).transpose(3, 0, 1, 2, 4)

    by_state = lambda tab: tab.transpose(0, 1, 2, 4, 3).reshape(n_pairs, 2, 2, 2, P, C)
    lag_pw = by_state(table(up, C - 1 - up))
    out_pw = by_state(table(up + 1, C - up))
    in_pw = table(C - 1 - up, up).reshape(n_pairs, 2, 2, 2, C, P)
    in_pw = jnp.tile(in_pw, (1, 1, 1, 1, 1, 2))
    ct = jnp.stack([cr, ci], axis=0).transpose(2, 1, 0, 4, 3).reshape(n_pairs, 2, 2, 2, P, H)
    bbt = jnp.stack([bb_re, bb_im], axis=0).transpose(2, 1, 0, 4, 3).reshape(n_pairs, 2, 2, 2, H, P)
    own = jnp.eye(2, dtype=F32)[None, :, None, None, None, :, None]
    bb_own = (bbt[:, :, :, :, :, None, :] * own).reshape(n_pairs, 2, 2, 2, H, 2 * P)
    skip = jnp.eye(H, dtype=F32)[None] * d.astype(F32).reshape(G, 1, H)
    skip = jnp.pad(skip, ((0, 0), (0, 0), ((C - 1) * H, 0))).reshape(n_pairs, 2, H, GROUP_K)
    rep_h = jnp.tile(jnp.eye(H, dtype=F32), (1, C))
    rep_k = jnp.repeat(jnp.eye(C, dtype=F32), H, axis=1)
    w_toep, w_in_state, w_out_state = _s5_pack(ct, bbt, bb_own, lag_pw, out_pw, in_pw, skip, rep_h, rep_k)

    a16 = jnp.stack([pw_re[C], pw_im[C]], axis=0)
    a16 = a16.reshape(2, 2, N_SLABS, S, P).transpose(2, 1, 0, 3, 4)
    a16 = jnp.broadcast_to(a16.reshape(1, N_SLABS * STATE_COLS), (SUBLANES, N_SLABS * STATE_COLS))
    return w_in_state, w_toep, w_out_state, a16


def _s5_pack_kernel(ct_ref, bbt_ref, bbo_ref, lagp_ref, outp_ref, inp_ref, skip_ref, reph_ref, repk_ref,
                    wt_ref, wi_ref, wo_ref):
    hp = lax.Precision.HIGHEST
    exact_dot = lambda a, b: jnp.dot(a, b, precision=hp, preferred_element_type=F32)
    rep_h = reph_ref[...]
    rep_k = repk_ref[...]
    wide = 2 * GROUP_K
    for a2 in range(2):
        lag_part = []
        for dr in range(2):
            c_re = exact_dot(ct_ref[a2, dr, 0], rep_h)
            c_im = exact_dot(ct_ref[a2, dr, 1], rep_h)

            def times_power(ref):
                p_re = exact_dot(ref[a2, dr, 0], rep_k)
                p_im = exact_dot(ref[a2, dr, 1], rep_k)
                return c_re * p_re - c_im * p_im, c_re * p_im + c_im * p_re

            l_re, l_im = times_power(lagp_ref)
            lag_part.append(exact_dot(bbt_ref[a2, dr, 0], l_re) - exact_dot(bbt_ref[a2, dr, 1], l_im))
            w_re, w_im = times_power(outp_ref)
            for ri, val in ((0, w_re), (1, -w_im)):
                rows = pl.ds((dr * 2 + ri) * LANES + a2 * S5_STATE, S5_STATE)
                wo_ref[rows, a2 * GROUP_K:(a2 + 1) * GROUP_K] = val.astype(BF16)
                wo_ref[rows, (1 - a2) * GROUP_K:(2 - a2) * GROUP_K] = jnp.zeros((S5_STATE, GROUP_K), BF16)
            b_re = bbo_ref[a2, dr, 0]
            b_im = bbo_ref[a2, dr, 1]
            for t in range(CHUNK):
                e_re = inp_ref[a2, dr, 0, t:t + 1, :]
                e_im = inp_ref[a2, dr, 1, t:t + 1, :]
                rows = pl.ds(a2 * GROUP_K + t * S5_GROUP, S5_GROUP)
                wi_ref[rows, (2 * dr) * LANES:(2 * dr + 1) * LANES] = (b_re * e_re - b_im * e_im).astype(BF16)
                wi_ref[rows, (2 * dr + 1) * LANES:(2 * dr + 2) * LANES] = (b_re * e_im + b_im * e_re).astype(BF16)
        zero = jnp.zeros((S5_GROUP, GROUP_K), F32)
        lags = (jnp.concatenate([lag_part[1] + skip_ref[a2], zero], axis=1)
                + pltpu.roll(jnp.concatenate([lag_part[0], zero], axis=1), (CHUNK - 1) * S5_GROUP, 1))
        for t in range(CHUNK):
            shift = (CHUNK - 1 - t) * S5_GROUP
            window = lags if shift == 0 else pltpu.roll(lags, wide - shift, 1)
            wt_ref[a2, t * S5_GROUP:(t + 1) * S5_GROUP, :] = window[:, :GROUP_K].astype(BF16)


def _s5_pack(ct, bbt, bb_own, lag_pw, out_pw, in_pw, skip, rep_h, rep_k):
    n_pairs = ct.shape[0]
    per_pair = lambda a: pl.BlockSpec((None,) + a.shape[1:], lambda i: (i,) + (0,) * (a.ndim - 1))
    tabs = (ct, bbt, bb_own, lag_pw, out_pw, in_pw, skip)
    wt, wi, wo = pl.pallas_call(
        _s5_pack_kernel,
        grid=(n_pairs,),
        in_specs=[per_pair(a) for a in tabs] + [_const_spec(rep_h.shape), _const_spec(rep_k.shape)],
        out_specs=[pl.BlockSpec((2, GROUP_K, GROUP_K), lambda i: (i, 0, 0)),
                   pl.BlockSpec((None, 2 * GROUP_K, 4 * LANES), lambda i: (i, 0, 0)),
                   pl.BlockSpec((None, 4 * LANES, 2 * GROUP_K), lambda i: (i, 0, 0))],
        out_shape=[jax.ShapeDtypeStruct((2 * n_pairs, GROUP_K, GROUP_K), BF16),
                   jax.ShapeDtypeStruct((n_pairs, 2 * GROUP_K, 4 * LANES), BF16),
                   jax.ShapeDtypeStruct((n_pairs, 4 * LANES, 2 * GROUP_K), BF16)],
        compiler_params=_params(("parallel",)),
        name="s5_pack",
    )(*tabs, rep_h, rep_k)
    return (wt.reshape(N_SLABS, GROUPS_PER_SLAB, GROUP_K, GROUP_K),
            wi.reshape(N_SLABS, PAIRS_PER_SLAB, 2 * GROUP_K, 4 * LANES),
            wo.reshape(N_SLABS, PAIRS_PER_SLAB, 4 * LANES, 2 * GROUP_K))


def _attn_kernel(sink_ref, q_ref, kl_ref, kc_ref, kr_ref, bias_first_ref, bias_mid_ref, bias_last_ref, o_ref):
    nk = 3 * BLOCK
    kv_all = jnp.concatenate([kl_ref[...], kc_ref[...], kr_ref[...]], axis=0)
    low = lax.broadcasted_iota(jnp.int32, (kv_all.shape[0], LANES), 1) < HEAD_DIM
    zero = jnp.zeros((kv_all.shape[0], LANES), BF16)

    def diag_parts(tile, kvh):
        other = pltpu.roll(tile, HEAD_DIM, 1)
        first, second = (tile, other) if kvh == 0 else (other, tile)
        return jnp.where(low, first, zero), jnp.where(low, zero, second)

    def window(parts, qb):
        return jnp.concatenate([p[qb * BLOCK:qb * BLOCK + nk] for p in parts], axis=0)

    top = lax.broadcasted_iota(jnp.int32, (2 * nk, LANES), 0) < nk
    ones_bd = (top == (lax.broadcasted_iota(jnp.int32, (2 * nk, LANES), 1) < HEAD_DIM)).astype(BF16)

    upper = lax.broadcasted_iota(jnp.int32, (2 * BLOCK, 1), 0) < BLOCK
    bias_refs = [bias_first_ref] + [bias_mid_ref] * (ATT_QB - 2) + [bias_last_ref]
    for kvh in range(N_KV_HEADS):
        k_parts = diag_parts(kv_all[:, :LANES], kvh)
        v_parts = diag_parts(kv_all[:, LANES:], kvh)
        for qb, bias_ref in enumerate(bias_refs):
            rows = slice(qb * BLOCK, (qb + 1) * BLOCK)
            kbd = window(k_parts, qb)
            vbd = jnp.concatenate([window(v_parts, qb), ones_bd], axis=1)
            tiles = [(kvh * 2 + pair) * LANES for pair in range(2)]
            q2 = jnp.concatenate([q_ref[rows, tl:tl + LANES] for tl in tiles], axis=0)
            s = lax.dot_general(q2, kbd, (((1,), (1,)), ((), ())), preferred_element_type=F32)
            s = s + bias_ref[kvh]
            ps, sinks = [], []
            for e in range(2):
                se = s[:, e * nk:(e + 1) * nk]
                sk = jnp.where(upper, sink_ref[4 * kvh + e], sink_ref[4 * kvh + 2 + e])
                mx = jnp.maximum(jnp.max(se, axis=-1, keepdims=True), sk)
                ps.append(jnp.exp2((se - mx).astype(BF16)))
                sinks.append(jnp.broadcast_to(jnp.exp2(sk - mx), (2 * BLOCK, HEAD_DIM)))
            nd = _dot(jnp.concatenate(ps, axis=1), vbd)
            o = (nd[:, :LANES] / (nd[:, LANES:] + jnp.concatenate(sinks, axis=1))).astype(BF16)
            for pair, tl in enumerate(tiles):
                o_ref[rows, tl:tl + LANES] = o[pair * BLOCK:(pair + 1) * BLOCK, :]


def _attention(q, kv, sink, bias, bsz, seq):
    nb = seq // BLOCK
    assert nb % ATT_QB == 0 and ATT_QB >= 2
    nj = nb // ATT_QB
    rows = ATT_QB * BLOCK
    centre = lambda b, j: (b * nj + j, 0)
    left = lambda b, j: (b * nb + jnp.maximum(ATT_QB * j - 1, 0), 0)
    right = lambda b, j: (b * nb + jnp.minimum(ATT_QB * (j + 1), nb - 1), 0)
    variant = (None,) + bias.shape[1:]
    bias_first = pl.BlockSpec(variant, lambda b, j: (jnp.where(j == 0, 0, 1), 0, 0, 0))
    bias_mid = pl.BlockSpec(variant, lambda b, j: (1, 0, 0, 0), pipeline_mode=pl.Buffered(1))
    bias_last = pl.BlockSpec(variant, lambda b, j: (jnp.where(j == nj - 1, 2, 1), 0, 0, 0))
    return pl.pallas_call(
        _attn_kernel,
        grid=(bsz, nj),
        in_specs=[pl.BlockSpec(memory_space=pltpu.SMEM),
                  pl.BlockSpec((rows, ATT_WIDTH), centre),
                  pl.BlockSpec((BLOCK, 2 * KV_WIDTH), left),
                  pl.BlockSpec((rows, 2 * KV_WIDTH), centre),
                  pl.BlockSpec((BLOCK, 2 * KV_WIDTH), right),
                  bias_first, bias_mid, bias_last],
        out_specs=pl.BlockSpec((rows, ATT_WIDTH), centre),
        out_shape=jax.ShapeDtypeStruct((bsz * seq, ATT_WIDTH), BF16),
        compiler_params=_params(("parallel", "parallel")),
        name="attention",
    )(sink, q, kv, kv, kv, bias, bias, bias)


def _t5_bucket(rel):
    half = NUM_BUCKETS // 2
    max_exact = half // 2
    ret = jnp.where(rel > 0, half, 0)
    n = jnp.abs(rel)
    nf = jnp.maximum(n, 1).astype(jnp.float32)
    large = max_exact + (jnp.log(nf / max_exact) / math.log(MAX_DISTANCE / max_exact)
                         * (half - max_exact)).astype(jnp.int32)
    large = jnp.minimum(large, half - 1)
    return ret + jnp.where(n < max_exact, n, large)


def _band_bias(rel_bias):
    q_loc = jnp.arange(BLOCK, dtype=jnp.int32)
    k_loc = jnp.arange(3 * BLOCK, dtype=jnp.int32)
    rel = (k_loc[None, :] - BLOCK) - q_loc[:, None]
    onehot = (_t5_bucket(rel)[None] == jnp.arange(NUM_BUCKETS, dtype=jnp.int32)[:, None, None]).astype(F32)
    bias = jnp.einsum('bh,bqk->hqk', rel_bias.astype(F32), onehot, precision=lax.Precision.HIGHEST)
    bias = jnp.where((jnp.abs(rel) <= WINDOW)[None], bias * LOG2E, NEG_INF)
    keep = jnp.stack([k_loc >= BLOCK, k_loc >= 0, k_loc < 2 * BLOCK], axis=0)
    bias = jnp.where(keep[:, None, None, :], bias[None], NEG_INF)
    bias = bias.reshape(3, N_Q_HEADS // 2, 2, BLOCK, 3 * BLOCK).transpose(0, 1, 3, 2, 4)
    return bias.reshape(3, N_KV_HEADS, 2 * BLOCK, 6 * BLOCK)


def _merge_ffn_kernel(h_ref, y_ref, yb_ref, g1_ref, g2_ref, gf_ref, unperm_ref, wg_ref, wglu_ref, wa_ref, wb_ref,
                      wo_ref, fg_ref, fu_ref, fd_ref, o_ref, acc_ref, *, final_norm):
    bsz = h_ref.shape[0]
    h = h_ref[...].reshape(TILE_ROWS, D_MODEL)
    hn = _rms(h, g1_ref[...]).astype(BF16)
    gates = _sigmoid(_dot(hn, wg_ref[...]))
    z = _gelu_tanh(_dot(unperm_ref[...], y_ref[...]))
    za = (z * _sigmoid(_dot(z.astype(BF16), wglu_ref[...]))).astype(BF16)
    yb = yb_ref[...].reshape(TILE_ROWS, ATT_WIDTH)
    merged = gates[:, :D_MODEL] * _dot(za, wa_ref[...]) + gates[:, D_MODEL:] * _dot(yb, wb_ref[...])
    h = h + _dot(merged.astype(BF16), wo_ref[...])

    hn = _rms(h, g2_ref[...]).astype(BF16)
    acc_ref[...] = h
    for j in range(D_FF // FF_TILE):
        cols = slice(j * FF_TILE, (j + 1) * FF_TILE)
        gate = _dot(hn, fg_ref[:, cols])
        up = _dot(hn, fu_ref[:, cols])
        act = (gate * _sigmoid(gate) * up).astype(BF16)
        acc_ref[...] += _dot(act, fd_ref[cols, :])
    out = acc_ref[...]
    out = _rms(out, gf_ref[...]) if final_norm else out
    o_ref[...] = out.reshape(bsz, TOK_TILE, D_MODEL)


def _merge_ffn(h3, y, yb3, g1, g2, gf, unperm, weights, final_norm):
    bsz, seq, _ = h3.shape
    tok = lambda width: pl.BlockSpec((bsz, TOK_TILE, width), lambda i: (0, i, 0))
    vec = _const_spec((1, D_MODEL))
    return pl.pallas_call(
        functools.partial(_merge_ffn_kernel, final_norm=final_norm),
        grid=(seq // TOK_TILE,),
        in_specs=[tok(D_MODEL), pl.BlockSpec((TILE_ROWS, S5_WIDTH), lambda i: (i, 0)), tok(ATT_WIDTH),
                  vec, vec, vec, _const_spec(unperm.shape)] + [_const_spec(w.shape) for w in weights],
        out_specs=tok(D_MODEL),
        out_shape=jax.ShapeDtypeStruct((bsz, seq, D_MODEL), F32),
        scratch_shapes=[pltpu.VMEM((TILE_ROWS, D_MODEL), F32)],
        compiler_params=_params(("parallel",)),
        name="merge_ffn",
    )(h3, y, yb3, g1, g2, gf, unperm, *weights)


def kernel(x, norm1_g, norm2_g, final_g, w_in, s5_lambda_re, s5_lambda_im, s5_log_dt, s5_b_re, s5_b_im,
           s5_c_re, s5_c_im, s5_d, s5_w_glu, attn_sink, rel_bias, w_branch_a, w_branch_b, w_out,
           ffn_w_gate, ffn_w_up, ffn_w_down):
    bsz, seq, _ = x.shape
    depth = w_in.shape[0]
    assert seq % BLOCK == 0 and seq % CHUNK == 0 and bsz == SUBLANES
    n_chunks = seq // CHUNK
    t = bsz * seq
    bias = _band_bias(rel_bias)
    o_k = S5_WIDTH + ATT_WIDTH
    o_g = o_k + 2 * KV_WIDTH
    col_scale = jnp.concatenate([jnp.ones((S5_WIDTH,), F32), jnp.full((ATT_WIDTH,), LOG2E * HEAD_DIM ** -0.5, F32),
                                 jnp.ones((2 * KV_WIDTH,), F32)])
    gf = final_g.reshape(1, D_MODEL).astype(F32)
    perm = _to_chunk_order()
    unperm = perm.T
    h = x
    for layer in range(depth):
        g1 = norm1_g[layer].reshape(1, D_MODEL).astype(F32)
        g2 = norm2_g[layer].reshape(1, D_MODEL).astype(F32)
        w_uqkv = (w_in[layer][:, :o_g] * col_scale).astype(BF16)
        w_gates = w_in[layer][:, o_g:].astype(BF16)
        u, q, kv = _inproj(h, g1, w_uqkv, perm)
        w_in_state, w_toep, w_out_state, a16 = _s5_weights(
            s5_lambda_re[layer], s5_lambda_im[layer], s5_log_dt[layer], s5_b_re[layer], s5_b_im[layer],
            s5_c_re[layer], s5_c_im[layer], s5_d[layer])
        sin = _s5_state_in(u, w_in_state)
        carry = _s5_scan(sin, a16, n_chunks)
        y = _s5_out(u, carry, w_toep, w_out_state)
        yb = _attention(q.reshape(t, ATT_WIDTH), kv.reshape(t, 2 * KV_WIDTH),
                        attn_sink[layer].astype(F32) * LOG2E, bias, bsz, seq)
        weights = [w.astype(BF16) for w in (
            w_gates, s5_w_glu[layer], w_branch_a[layer], w_branch_b[layer], w_out[layer],
            ffn_w_gate[layer], ffn_w_up[layer], ffn_w_down[layer])]
        h = _merge_ffn(h, y, yb.reshape(bsz, seq, ATT_WIDTH), g1, g2, gf, unperm, weights,
                       final_norm=(layer == depth - 1))
    return h
```

```python
import functools
import math

import jax
import jax.numpy as jnp
from jax import lax
from jax.experimental import pallas as pl
from jax.experimental.pallas import tpu as pltpu

F32 = jnp.float32
BF16 = jnp.bfloat16

D_MODEL = 1024
S5_WIDTH = 512
S5_GROUP = 16
S5_GROUPS = 32
S5_STATE = 64
HEAD_DIM = 64
N_Q_HEADS = 8
N_KV_HEADS = 2
Q_PER_KV = N_Q_HEADS // N_KV_HEADS
ATT_WIDTH = N_Q_HEADS * HEAD_DIM
KV_WIDTH = N_KV_HEADS * HEAD_DIM
WINDOW = 128
BLOCK = 128
NUM_BUCKETS = 32
MAX_DISTANCE = 128
D_FF = 2816
RMS_EPS = 1e-6
NEG_INF = -1e30
LOG2E = math.log2(math.e)

LANES = 128
SUBLANES = 8
CHUNK = 16
GROUPS_PER_SLAB = LANES // S5_GROUP
N_SLABS = S5_WIDTH // LANES
PAIRS_PER_SLAB = GROUPS_PER_SLAB // 2
GROUP_K = CHUNK * S5_GROUP
DIR_COLS = 2 * GROUPS_PER_SLAB * S5_STATE
TOK_TILE = 64
TILE_ROWS = SUBLANES * TOK_TILE
S5_ROWS = 8 * TILE_ROWS
FF_TILE = 256
ATT_QB = 8
VMEM_LIMIT = 56 * 1024 * 1024


def _rms(x, g):
    return x * lax.rsqrt(jnp.mean(x * x, axis=-1, keepdims=True) + RMS_EPS) * g


def _gelu_tanh(x):
    return 0.5 * x * (1.0 + jnp.tanh(math.sqrt(2.0 / math.pi) * (x + 0.044715 * (x * x * x))))


def _sigmoid(x):
    return 1.0 / (1.0 + jnp.exp(-x))


def _dot(a, b):
    return jnp.dot(a, b, preferred_element_type=F32)


def _const_spec(shape):
    nd = len(shape)
    return pl.BlockSpec(shape, lambda *_: (0,) * nd, pipeline_mode=pl.Buffered(1))


def _params(sem):
    return pltpu.CompilerParams(dimension_semantics=sem, vmem_limit_bytes=VMEM_LIMIT)


def _to_chunk_order():
    cpt = TOK_TILE // CHUNK
    dst = jnp.arange(TILE_ROWS)
    t, c, b = dst // (cpt * SUBLANES), (dst // SUBLANES) % cpt, dst % SUBLANES
    src = b * TOK_TILE + c * CHUNK + t
    return (src[:, None] == jnp.arange(TILE_ROWS)[None, :]).astype(BF16)


def _inproj_kernel(x_ref, g_ref, w_ref, perm_ref, u_ref, q_ref, kv_ref):
    bsz = x_ref.shape[0]
    hn = _rms(x_ref[...].reshape(TILE_ROWS, D_MODEL), g_ref[...]).astype(BF16)
    r = _dot(hn, w_ref[...])
    u_ref[...] = _dot(perm_ref[...], r[:, :S5_WIDTH].astype(BF16)).astype(BF16)
    q_ref[...] = r[:, S5_WIDTH:S5_WIDTH + ATT_WIDTH].astype(BF16).reshape(bsz, TOK_TILE, ATT_WIDTH)
    kv_ref[...] = r[:, S5_WIDTH + ATT_WIDTH:].astype(BF16).reshape(bsz, TOK_TILE, 2 * KV_WIDTH)


def _inproj(h3, g, w, perm):
    bsz, seq, _ = h3.shape
    assert bsz * TOK_TILE == TILE_ROWS
    n = w.shape[1]
    tok = lambda width: pl.BlockSpec((bsz, TOK_TILE, width), lambda i: (0, i, 0))
    return pl.pallas_call(
        _inproj_kernel,
        grid=(seq // TOK_TILE,),
        in_specs=[tok(D_MODEL), _const_spec((1, D_MODEL)), _const_spec((D_MODEL, n)),
                  _const_spec((TILE_ROWS, TILE_ROWS))],
        out_specs=[pl.BlockSpec((TILE_ROWS, S5_WIDTH), lambda i: (i, 0)), tok(ATT_WIDTH), tok(2 * KV_WIDTH)],
        out_shape=[jax.ShapeDtypeStruct((bsz * seq, S5_WIDTH), BF16),
                   jax.ShapeDtypeStruct((bsz, seq, ATT_WIDTH), BF16),
                   jax.ShapeDtypeStruct((bsz, seq, 2 * KV_WIDTH), BF16)],
        compiler_params=_params(("parallel",)),
        name="inproj",
    )(h3, g, w, perm)


def _block_transpose(x):
    x = list(x)
    blk = lax.broadcasted_iota(jnp.int32, x[0].shape, 1) // S5_GROUP
    for d in (4, 2, 1):
        keep = (blk & d) == 0
        for i in range(GROUPS_PER_SLAB):
            if i & d:
                continue
            xi, xj = x[i], x[i + d]
            x[i] = jnp.where(keep, xi, pltpu.roll(xj, d * S5_GROUP, 1))
            x[i + d] = jnp.where(keep, pltpu.roll(xi, LANES - d * S5_GROUP, 1), xj)
    return x


def _token_rows(t, k):
    return pl.ds(k * TILE_ROWS + t * (TILE_ROWS // CHUNK), TILE_ROWS // CHUNK)


def _group_rows(u_ref):
    tiles = u_ref.shape[0] // TILE_ROWS
    tok = [jnp.concatenate([u_ref[_token_rows(t, k), :] for k in range(tiles)], axis=0) for t in range(CHUNK)]
    lo = _block_transpose(tok[:CHUNK // 2])
    hi = _block_transpose(tok[CHUNK // 2:])
    return [jnp.concatenate([lo[a], hi[a]], axis=1) for a in range(GROUPS_PER_SLAB)]


def _state_tile(ri, pq):
    return pl.ds(ri * (DIR_COLS // 2) + pq * LANES, LANES)


def _s5_in_kernel(u_ref, w_ref, of_ref, ob_ref):
    g = _group_rows(u_ref)
    for pq in range(PAIRS_PER_SLAB):
        r = _dot(jnp.concatenate([g[2 * pq], g[2 * pq + 1]], axis=1), w_ref[pq])
        for dr, o_ref in enumerate((of_ref, ob_ref)):
            for ri in range(2):
                k = dr * 2 + ri
                o_ref[:, _state_tile(ri, pq)] = r[:, k * LANES:(k + 1) * LANES].astype(BF16)


def _s5_state_in(u, w_in_state, layer, rows=S5_ROWS):
    t = u.shape[0]
    state = jax.ShapeDtypeStruct((t // CHUNK, N_SLABS * DIR_COLS), BF16)
    spec = pl.BlockSpec((rows // CHUNK, DIR_COLS), lambda s, i: (i, s))
    return pl.pallas_call(
        _s5_in_kernel,
        grid=(N_SLABS, t // rows),
        in_specs=[pl.BlockSpec((rows, LANES), lambda s, i: (i, s)),
                  pl.BlockSpec((None,) + w_in_state.shape[1:], lambda s, i: (layer * N_SLABS + s, 0, 0, 0))],
        out_specs=[spec, spec],
        out_shape=[state, state],
        compiler_params=_params(("parallel", "parallel")),
        name="s5_state_in",
    )(u, w_in_state)


def _s5_scan_kernel(xf_ref, xb_ref, af_ref, ab_ref, of_ref, ob_ref, st_ref, *, cpt):
    half = DIR_COLS // 2
    pair_rows = 2 * SUBLANES
    units = xf_ref.shape[1] // DIR_COLS
    lower, upper = slice(0, SUBLANES), slice(SUBLANES, pair_rows)

    @pl.when(pl.program_id(1) == 0)
    def _():
        st_ref[...] = jnp.zeros_like(st_ref)

    def two_chunks(x_ref, a_ref, o_ref, c2, state, reverse):
        rows = pl.ds(pl.multiple_of(c2 * pair_rows, pair_rows), pair_rows)
        first, second = (upper, lower) if reverse else (lower, upper)
        new_state = []
        for un in range(units):
            re = slice(un * DIR_COLS, un * DIR_COLS + half)
            im = slice(un * DIR_COLS + half, (un + 1) * DIR_COLS)
            ar, ai = a_ref[:, re], a_ref[:, im]
            xr = x_ref[rows, re].astype(F32)
            xi = x_ref[rows, im].astype(F32)
            s0r, s0i = state[un]
            s1r, s1i = ar * s0r - ai * s0i + xr[first], ar * s0i + ai * s0r + xi[first]
            s2r, s2i = ar * s1r - ai * s1i + xr[second], ar * s1i + ai * s1r + xi[second]
            enter_r, enter_i = ((s1r, s0r), (s1i, s0i)) if reverse else ((s0r, s1r), (s0i, s1i))
            o_ref[rows, re] = jnp.concatenate(enter_r, axis=0).astype(BF16)
            o_ref[rows, im] = jnp.concatenate(enter_i, axis=0).astype(BF16)
            new_state.append((s2r, s2i))
        return new_state

    def body(i, state):
        fwd, bwd = state
        return (two_chunks(xf_ref, af_ref, of_ref, i, fwd, False),
                two_chunks(xb_ref, ab_ref, ob_ref, cpt // 2 - 1 - i, bwd, True))

    def load(dr):
        return [(st_ref[dr, :, pl.ds(un * DIR_COLS, half)], st_ref[dr, :, pl.ds(un * DIR_COLS + half, half)])
                for un in range(units)]

    fwd, bwd = lax.fori_loop(0, cpt // 2, body, (load(0), load(1)))
    for dr, state in enumerate((fwd, bwd)):
        for un, (sr, si) in enumerate(state):
            st_ref[dr, :, pl.ds(un * DIR_COLS, half)] = sr
            st_ref[dr, :, pl.ds(un * DIR_COLS + half, half)] = si


def _s5_scan(sin_f, sin_b, a16, n_chunks, cpt=64, width=2 * DIR_COLS):
    nct = n_chunks // cpt
    rows = cpt * SUBLANES
    up = pl.BlockSpec((rows, width), lambda c, k: (k, c))
    down = pl.BlockSpec((rows, width), lambda c, k: (nct - 1 - k, c))
    coef = lambda dr: pl.BlockSpec((None, SUBLANES, width), lambda c, k: (dr, 0, c))
    return pl.pallas_call(
        functools.partial(_s5_scan_kernel, cpt=cpt),
        grid=(sin_f.shape[1] // width, nct),
        in_specs=[up, down, coef(0), coef(1)],
        out_specs=[up, down],
        out_shape=[jax.ShapeDtypeStruct(sin_f.shape, BF16), jax.ShapeDtypeStruct(sin_b.shape, BF16)],
        scratch_shapes=[pltpu.VMEM((2, SUBLANES, width), F32)],
        compiler_params=_params(("parallel", "arbitrary")),
        name="s5_scan",
    )(sin_f, sin_b, a16, a16)


def _s5_out_kernel(u_ref, cf_ref, cb_ref, wt_ref, wo_ref, y_ref):
    g = _group_rows(u_ref)
    ys = []
    for pq in range(PAIRS_PER_SLAB):
        carry = jnp.concatenate([c_ref[:, _state_tile(ri, pq)] for c_ref in (cf_ref, cb_ref) for ri in range(2)],
                                axis=1)
        from_state = _dot(carry, wo_ref[pq])
        for a2 in range(2):
            a = 2 * pq + a2
            ys.append((_dot(g[a], wt_ref[a]) + from_state[:, a2 * GROUP_K:(a2 + 1) * GROUP_K]).astype(BF16))
    halves = (_block_transpose([y[:, :LANES] for y in ys]), _block_transpose([y[:, LANES:] for y in ys]))
    per_tile = TILE_ROWS // CHUNK
    for t in range(CHUNK):
        tok = halves[t // (CHUNK // 2)][t % (CHUNK // 2)]
        for k in range(y_ref.shape[0] // TILE_ROWS):
            y_ref[_token_rows(t, k), :] = tok[k * per_tile:(k + 1) * per_tile, :]


def _s5_out(u, carry_f, carry_b, w_toep, w_out_state, layer, rows=S5_ROWS):
    t = u.shape[0]
    state = pl.BlockSpec((rows // CHUNK, DIR_COLS), lambda s, i: (i, s))
    return pl.pallas_call(
        _s5_out_kernel,
        grid=(N_SLABS, t // rows),
        in_specs=[pl.BlockSpec((rows, LANES), lambda s, i: (i, s)), state, state,
                  pl.BlockSpec((None,) + w_toep.shape[1:], lambda s, i: (layer * N_SLABS + s, 0, 0, 0)),
                  pl.BlockSpec((None,) + w_out_state.shape[1:], lambda s, i: (layer * N_SLABS + s, 0, 0, 0))],
        out_specs=pl.BlockSpec((rows, LANES), lambda s, i: (i, s)),
        out_shape=jax.ShapeDtypeStruct((t, S5_WIDTH), BF16),
        compiler_params=_params(("parallel", "parallel")),
        name="s5_out",
    )(u, carry_f, carry_b, w_toep, w_out_state)


def _s5_tables(lam_re, lam_im, log_dt, b_re, b_im, c_re, c_im, d):
    G, P, H, C, S = S5_GROUPS, S5_STATE, S5_GROUP, CHUNK, GROUPS_PER_SLAB
    lr = lam_re.astype(F32)
    li = lam_im.astype(F32)
    dt = jnp.exp(log_dt.astype(F32))[..., None]
    mag = jnp.exp(lr * dt)
    ab_re = mag * jnp.cos(li * dt)
    ab_im = mag * jnp.sin(li * dt)
    nr = ab_re - 1.0
    den = lr * lr + li * li
    coef_re = (nr * lr + ab_im * li) / den
    coef_im = (ab_im * lr - nr * li) / den
    br = b_re.astype(F32)
    bi = b_im.astype(F32)
    bb_re = coef_re[..., None] * br - coef_im[..., None] * bi
    bb_im = coef_re[..., None] * bi + coef_im[..., None] * br
    cr = c_re.astype(F32)
    ci = c_im.astype(F32)

    k = jnp.arange(C + 1, dtype=F32)[:, None, None, None]
    pmag = jnp.exp(k * (lr * dt))
    pw_re = pmag * jnp.cos(k * (li * dt))
    pw_im = pmag * jnp.sin(k * (li * dt))

    n_pairs = G // 2
    pw = jnp.stack([pw_re, pw_im], axis=0)
    up = jnp.arange(C)

    def table(k_fwd, k_bwd):
        return jnp.stack([pw[:, k_fwd, 0], pw[:, k_bwd, 1]], axis=0).transpose(3, 0, 1, 2, 4)

    by_state = lambda tab: tab.transpose(0, 1, 2, 4, 3).reshape(n_pairs, 2, 2, 2, P, C)
    lag_pw = by_state(table(up, C - 1 - up))
    out_pw = by_state(table(up + 1, C - up))
    in_pw = table(C - 1 - up, up).reshape(n_pairs, 2, 2, 2, C, P)
    in_pw = jnp.tile(in_pw, (1, 1, 1, 1, 1, 2))
    ct = jnp.stack([cr, ci], axis=0).transpose(2, 1, 0, 4, 3).reshape(n_pairs, 2, 2, 2, P, H)
    bbt = jnp.stack([bb_re, bb_im], axis=0).transpose(2, 1, 0, 4, 3).reshape(n_pairs, 2, 2, 2, H, P)
    own = jnp.eye(2, dtype=F32)[None, :, None, None, None, :, None]
    bb_own = (bbt[:, :, :, :, :, None, :] * own).reshape(n_pairs, 2, 2, 2, H, 2 * P)
    skip = jnp.eye(H, dtype=F32)[None] * d.astype(F32).reshape(G, 1, H)
    skip = jnp.pad(skip, ((0, 0), (0, 0), ((C - 1) * H, 0))).reshape(n_pairs, 2, H, GROUP_K)

    a16 = jnp.stack([pw_re[C], pw_im[C]], axis=0)
    a16 = a16.reshape(2, 2, N_SLABS, S, P).transpose(1, 2, 0, 3, 4)
    a16 = jnp.broadcast_to(a16.reshape(2, 1, N_SLABS * DIR_COLS), (2, SUBLANES, N_SLABS * DIR_COLS))
    return (ct, bbt, bb_own, lag_pw, out_pw, in_pw, skip), a16


def _s5_weights(*stacked_params):
    depth = stacked_params[0].shape[0]
    tabs, a16 = jax.vmap(_s5_tables)(*stacked_params)
    tabs = [a.reshape((depth * a.shape[1],) + a.shape[2:]) for a in tabs]
    rep_h = jnp.tile(jnp.eye(S5_GROUP, dtype=F32), (1, CHUNK))
    rep_k = jnp.repeat(jnp.eye(CHUNK, dtype=F32), S5_GROUP, axis=1)
    wt, wi, wo = _s5_pack(*tabs, rep_h, rep_k)
    wt = wt.reshape(depth * N_SLABS, GROUPS_PER_SLAB, GROUP_K, GROUP_K)
    wi = wi.reshape(depth * N_SLABS, PAIRS_PER_SLAB, 2 * GROUP_K, 4 * LANES)
    wo = wo.reshape(depth * N_SLABS, PAIRS_PER_SLAB, 4 * LANES, 2 * GROUP_K)
    return wi, wt, wo, a16


def _s5_pack_kernel(ct_ref, bbt_ref, bbo_ref, lagp_ref, outp_ref, inp_ref, skip_ref, reph_ref, repk_ref,
                    wt_ref, wi_ref, wo_ref):
    hp = lax.Precision.HIGHEST
    exact_dot = lambda a, b: jnp.dot(a, b, precision=hp, preferred_element_type=F32)
    rep_h = reph_ref[...]
    rep_k = repk_ref[...]
    wide = 2 * GROUP_K
    for a2 in range(2):
        lag_part = []
        for dr in range(2):
            c_re = exact_dot(ct_ref[a2, dr, 0], rep_h)
            c_im = exact_dot(ct_ref[a2, dr, 1], rep_h)

            def times_power(ref):
                p_re = exact_dot(ref[a2, dr, 0], rep_k)
                p_im = exact_dot(ref[a2, dr, 1], rep_k)
                return c_re * p_re - c_im * p_im, c_re * p_im + c_im * p_re

            l_re, l_im = times_power(lagp_ref)
            lag_part.append(exact_dot(bbt_ref[a2, dr, 0], l_re) - exact_dot(bbt_ref[a2, dr, 1], l_im))
            w_re, w_im = times_power(outp_ref)
            for ri, val in ((0, w_re), (1, -w_im)):
                rows = pl.ds((dr * 2 + ri) * LANES + a2 * S5_STATE, S5_STATE)
                wo_ref[rows, a2 * GROUP_K:(a2 + 1) * GROUP_K] = val.astype(BF16)
                wo_ref[rows, (1 - a2) * GROUP_K:(2 - a2) * GROUP_K] = jnp.zeros((S5_STATE, GROUP_K), BF16)
            b_re = bbo_ref[a2, dr, 0]
            b_im = bbo_ref[a2, dr, 1]
            for t in range(CHUNK):
                e_re = inp_ref[a2, dr, 0, t:t + 1, :]
                e_im = inp_ref[a2, dr, 1, t:t + 1, :]
                rows = pl.ds(a2 * GROUP_K + t * S5_GROUP, S5_GROUP)
                wi_ref[rows, (2 * dr) * LANES:(2 * dr + 1) * LANES] = (b_re * e_re - b_im * e_im).astype(BF16)
                wi_ref[rows, (2 * dr + 1) * LANES:(2 * dr + 2) * LANES] = (b_re * e_im + b_im * e_re).astype(BF16)
        zero = jnp.zeros((S5_GROUP, GROUP_K), F32)
        lags = (jnp.concatenate([lag_part[1] + skip_ref[a2], zero], axis=1)
                + pltpu.roll(jnp.concatenate([lag_part[0], zero], axis=1), (CHUNK - 1) * S5_GROUP, 1))
        for t in range(CHUNK):
            shift = (CHUNK - 1 - t) * S5_GROUP
            window = lags if shift == 0 else pltpu.roll(lags, wide - shift, 1)
            wt_ref[a2, t * S5_GROUP:(t + 1) * S5_GROUP, :] = window[:, :GROUP_K].astype(BF16)


def _s5_pack(ct, bbt, bb_own, lag_pw, out_pw, in_pw, skip, rep_h, rep_k):
    n_pairs = ct.shape[0]
    per_pair = lambda a: pl.BlockSpec((None,) + a.shape[1:], lambda i: (i,) + (0,) * (a.ndim - 1))
    tabs = (ct, bbt, bb_own, lag_pw, out_pw, in_pw, skip)
    return pl.pallas_call(
        _s5_pack_kernel,
        grid=(n_pairs,),
        in_specs=[per_pair(a) for a in tabs] + [_const_spec(rep_h.shape), _const_spec(rep_k.shape)],
        out_specs=[pl.BlockSpec((2, GROUP_K, GROUP_K), lambda i: (i, 0, 0)),
                   pl.BlockSpec((None, 2 * GROUP_K, 4 * LANES), lambda i: (i, 0, 0)),
                   pl.BlockSpec((None, 4 * LANES, 2 * GROUP_K), lambda i: (i, 0, 0))],
        out_shape=[jax.ShapeDtypeStruct((2 * n_pairs, GROUP_K, GROUP_K), BF16),
                   jax.ShapeDtypeStruct((n_pairs, 2 * GROUP_K, 4 * LANES), BF16),
                   jax.ShapeDtypeStruct((n_pairs, 4 * LANES, 2 * GROUP_K), BF16)],
        compiler_params=_params(("parallel",)),
        name="s5_pack",
    )(*tabs, rep_h, rep_k)


def _attn_kernel(sink_ref, q_ref, kl_ref, kc_ref, kr_ref, bias_first_ref, bias_mid_ref, bias_last_ref, o_ref):
    nk = 3 * BLOCK
    kv_all = jnp.concatenate([kl_ref[...], kc_ref[...], kr_ref[...]], axis=0)
    low = lax.broadcasted_iota(jnp.int32, (kv_all.shape[0], LANES), 1) < HEAD_DIM
    zero = jnp.zeros((kv_all.shape[0], LANES), BF16)

    def diag_parts(tile, kvh):
        other = pltpu.roll(tile, HEAD_DIM, 1)
        first, second = (tile, other) if kvh == 0 else (other, tile)
        return jnp.where(low, first, zero), jnp.where(low, zero, second)

    def window(parts, qb):
        return jnp.concatenate([p[qb * BLOCK:qb * BLOCK + nk] for p in parts], axis=0)

    top = lax.broadcasted_iota(jnp.int32, (2 * nk, LANES), 0) < nk
    ones_bd = (top == (lax.broadcasted_iota(jnp.int32, (2 * nk, LANES), 1) < HEAD_DIM)).astype(BF16)

    upper = lax.broadcasted_iota(jnp.int32, (2 * BLOCK, 1), 0) < BLOCK
    bias_refs = [bias_first_ref] + [bias_mid_ref] * (ATT_QB - 2) + [bias_last_ref]
    for kvh in range(N_KV_HEADS):
        k_parts = diag_parts(kv_all[:, :LANES], kvh)
        v_parts = diag_parts(kv_all[:, LANES:], kvh)
        for qb, bias_ref in enumerate(bias_refs):
            rows = slice(qb * BLOCK, (qb + 1) * BLOCK)
            kbd = window(k_parts, qb)
            vbd = jnp.concatenate([window(v_parts, qb), ones_bd], axis=1)
            tiles = [(kvh * 2 + pair) * LANES for pair in range(2)]
            q2 = jnp.concatenate([q_ref[rows, tl:tl + LANES] for tl in tiles], axis=0)
            s = lax.dot_general(q2, kbd, (((1,), (1,)), ((), ())), preferred_element_type=F32)
            s = s + bias_ref[kvh]
            ps, sinks = [], []
            for e in range(2):
                se = s[:, e * nk:(e + 1) * nk]
                sk = jnp.where(upper, sink_ref[4 * kvh + e], sink_ref[4 * kvh + 2 + e])
                mx = jnp.maximum(jnp.max(se, axis=-1, keepdims=True), sk)
                ps.append(jnp.exp2((se - mx).astype(BF16)))
                sinks.append(jnp.broadcast_to(jnp.exp2(sk - mx), (2 * BLOCK, HEAD_DIM)))
            nd = _dot(jnp.concatenate(ps, axis=1), vbd)
            o = (nd[:, :LANES] / (nd[:, LANES:] + jnp.concatenate(sinks, axis=1))).astype(BF16)
            for pair, tl in enumerate(tiles):
                o_ref[rows, tl:tl + LANES] = o[pair * BLOCK:(pair + 1) * BLOCK, :]


def _attention(q, kv, sink, bias, bsz, seq):
    nb = seq // BLOCK
    assert nb % ATT_QB == 0 and ATT_QB >= 2
    nj = nb // ATT_QB
    rows = ATT_QB * BLOCK
    centre = lambda b, j: (b * nj + j, 0)
    left = lambda b, j: (b * nb + jnp.maximum(ATT_QB * j - 1, 0), 0)
    right = lambda b, j: (b * nb + jnp.minimum(ATT_QB * (j + 1), nb - 1), 0)
    variant = (None,) + bias.shape[1:]
    bias_first = pl.BlockSpec(variant, lambda b, j: (jnp.where(j == 0, 0, 1), 0, 0, 0))
    bias_mid = pl.BlockSpec(variant, lambda b, j: (1, 0, 0, 0), pipeline_mode=pl.Buffered(1))
    bias_last = pl.BlockSpec(variant, lambda b, j: (jnp.where(j == nj - 1, 2, 1), 0, 0, 0))
    return pl.pallas_call(
        _attn_kernel,
        grid=(bsz, nj),
        in_specs=[pl.BlockSpec(memory_space=pltpu.SMEM),
                  pl.BlockSpec((rows, ATT_WIDTH), centre),
                  pl.BlockSpec((BLOCK, 2 * KV_WIDTH), left),
                  pl.BlockSpec((rows, 2 * KV_WIDTH), centre),
                  pl.BlockSpec((BLOCK, 2 * KV_WIDTH), right),
                  bias_first, bias_mid, bias_last],
        out_specs=pl.BlockSpec((rows, ATT_WIDTH), centre),
        out_shape=jax.ShapeDtypeStruct((bsz * seq, ATT_WIDTH), BF16),
        compiler_params=_params(("parallel", "parallel")),
        name="attention",
    )(sink, q, kv, kv, kv, bias, bias, bias)


def _t5_bucket(rel):
    half = NUM_BUCKETS // 2
    max_exact = half // 2
    ret = jnp.where(rel > 0, half, 0)
    n = jnp.abs(rel)
    nf = jnp.maximum(n, 1).astype(jnp.float32)
    large = max_exact + (jnp.log(nf / max_exact) / math.log(MAX_DISTANCE / max_exact)
                         * (half - max_exact)).astype(jnp.int32)
    large = jnp.minimum(large, half - 1)
    return ret + jnp.where(n < max_exact, n, large)


def _band_bias(rel_bias):
    q_loc = jnp.arange(BLOCK, dtype=jnp.int32)
    k_loc = jnp.arange(3 * BLOCK, dtype=jnp.int32)
    rel = (k_loc[None, :] - BLOCK) - q_loc[:, None]
    onehot = (_t5_bucket(rel)[None] == jnp.arange(NUM_BUCKETS, dtype=jnp.int32)[:, None, None]).astype(F32)
    bias = jnp.einsum('bh,bqk->hqk', rel_bias.astype(F32), onehot, precision=lax.Precision.HIGHEST)
    bias = jnp.where((jnp.abs(rel) <= WINDOW)[None], bias * LOG2E, NEG_INF)
    keep = jnp.stack([k_loc >= BLOCK, k_loc >= 0, k_loc < 2 * BLOCK], axis=0)
    bias = jnp.where(keep[:, None, None, :], bias[None], NEG_INF)
    bias = bias.reshape(3, N_Q_HEADS // 2, 2, BLOCK, 3 * BLOCK).transpose(0, 1, 3, 2, 4)
    return bias.reshape(3, N_KV_HEADS, 2 * BLOCK, 6 * BLOCK)


def _merge_ffn_kernel(h_ref, y_ref, yb_ref, g1_ref, g2_ref, gf_ref, unperm_ref, wg_ref, wglu_ref, wa_ref, wb_ref,
                      wo_ref, fg_ref, fu_ref, fd_ref, o_ref, acc_ref, *, final_norm):
    bsz = h_ref.shape[0]
    h = h_ref[...].reshape(TILE_ROWS, D_MODEL)
    hn = _rms(h, g1_ref[...]).astype(BF16)
    gates = _sigmoid(_dot(hn, wg_ref[...]))
    z = _gelu_tanh(_dot(unperm_ref[...], y_ref[...]))
    za = (z * _sigmoid(_dot(z.astype(BF16), wglu_ref[...]))).astype(BF16)
    yb = yb_ref[...].reshape(TILE_ROWS, ATT_WIDTH)
    merged = gates[:, :D_MODEL] * _dot(za, wa_ref[...]) + gates[:, D_MODEL:] * _dot(yb, wb_ref[...])
    h = h + _dot(merged.astype(BF16), wo_ref[...])

    hn = _rms(h, g2_ref[...]).astype(BF16)
    acc_ref[...] = h
    for j in range(D_FF // FF_TILE):
        cols = slice(j * FF_TILE, (j + 1) * FF_TILE)
        gate = _dot(hn, fg_ref[:, cols])
        up = _dot(hn, fu_ref[:, cols])
        act = (gate * _sigmoid(gate) * up).astype(BF16)
        acc_ref[...] += _dot(act, fd_ref[cols, :])
    out = acc_ref[...]
    out = _rms(out, gf_ref[...]) if final_norm else out
    o_ref[...] = out.reshape(bsz, TOK_TILE, D_MODEL)


def _merge_ffn(h3, y, yb3, g1, g2, gf, unperm, weights, final_norm):
    bsz, seq, _ = h3.shape
    tok = lambda width: pl.BlockSpec((bsz, TOK_TILE, width), lambda i: (0, i, 0))
    vec = _const_spec((1, D_MODEL))
    return pl.pallas_call(
        functools.partial(_merge_ffn_kernel, final_norm=final_norm),
        grid=(seq // TOK_TILE,),
        in_specs=[tok(D_MODEL), pl.BlockSpec((TILE_ROWS, S5_WIDTH), lambda i: (i, 0)), tok(ATT_WIDTH),
                  vec, vec, vec, _const_spec(unperm.shape)] + [_const_spec(w.shape) for w in weights],
        out_specs=tok(D_MODEL),
        out_shape=jax.ShapeDtypeStruct((bsz, seq, D_MODEL), F32),
        scratch_shapes=[pltpu.VMEM((TILE_ROWS, D_MODEL), F32)],
        compiler_params=_params(("parallel",)),
        name="merge_ffn",
    )(h3, y, yb3, g1, g2, gf, unperm, *weights)


def kernel(x, norm1_g, norm2_g, final_g, w_in, s5_lambda_re, s5_lambda_im, s5_log_dt, s5_b_re, s5_b_im,
           s5_c_re, s5_c_im, s5_d, s5_w_glu, attn_sink, rel_bias, w_branch_a, w_branch_b, w_out,
           ffn_w_gate, ffn_w_up, ffn_w_down):
    bsz, seq, _ = x.shape
    depth = w_in.shape[0]
    assert seq % BLOCK == 0 and seq % CHUNK == 0 and bsz == SUBLANES
    n_chunks = seq // CHUNK
    t = bsz * seq
    bias = _band_bias(rel_bias)
    o_k = S5_WIDTH + ATT_WIDTH
    o_g = o_k + 2 * KV_WIDTH
    col_scale = jnp.concatenate([jnp.ones((S5_WIDTH,), F32), jnp.full((ATT_WIDTH,), LOG2E * HEAD_DIM ** -0.5, F32),
                                 jnp.ones((2 * KV_WIDTH,), F32)])
    gf = final_g.reshape(1, D_MODEL).astype(F32)
    perm = _to_chunk_order()
    unperm = perm.T
    w_in_state, w_toep, w_out_state, a16 = _s5_weights(
        s5_lambda_re, s5_lambda_im, s5_log_dt, s5_b_re, s5_b_im, s5_c_re, s5_c_im, s5_d)
    h = x
    for layer in range(depth):
        g1 = norm1_g[layer].reshape(1, D_MODEL).astype(F32)
        g2 = norm2_g[layer].reshape(1, D_MODEL).astype(F32)
        w_uqkv = (w_in[layer][:, :o_g] * col_scale).astype(BF16)
        w_gates = w_in[layer][:, o_g:].astype(BF16)
        u, q, kv = _inproj(h, g1, w_uqkv, perm)
        sin_f, sin_b = _s5_state_in(u, w_in_state, layer)
        carry_f, carry_b = _s5_scan(sin_f, sin_b, a16[layer], n_chunks)
        y = _s5_out(u, carry_f, carry_b, w_toep, w_out_state, layer)
        yb = _attention(q.reshape(t, ATT_WIDTH), kv.reshape(t, 2 * KV_WIDTH),
                        attn_sink[layer].astype(F32) * LOG2E, bias, bsz, seq)
        weights = [w.astype(BF16) for w in (
            w_gates, s5_w_glu[layer], w_branch_a[layer], w_branch_b[layer], w_out[layer],
            ffn_w_gate[layer], ffn_w_up[layer], ffn_w_down[layer])]
        h = _merge_ffn(h, y, yb.reshape(bsz, seq, ATT_WIDTH), g1, g2, gf, unperm, weights,
                       final_norm=(layer == depth - 1))
    return h
```

```python
import functools
import math

import jax
import jax.numpy as jnp
from jax import lax
from jax.experimental import pallas as pl
from jax.experimental.pallas import tpu as pltpu

F32 = jnp.float32
BF16 = jnp.bfloat16

D_MODEL = 1024
S5_WIDTH = 512
S5_GROUP = 16
S5_GROUPS = 32
S5_STATE = 64
HEAD_DIM = 64
N_Q_HEADS = 8
N_KV_HEADS = 2
Q_PER_KV = N_Q_HEADS // N_KV_HEADS
ATT_WIDTH = N_Q_HEADS * HEAD_DIM
KV_WIDTH = N_KV_HEADS * HEAD_DIM
WINDOW = 128
BLOCK = 128
NUM_BUCKETS = 32
MAX_DISTANCE = 128
D_FF = 2816
RMS_EPS = 1e-6
NEG_INF = -1e30
LOG2E = math.log2(math.e)

LANES = 128
SUBLANES = 8
CHUNK = 16
GROUPS_PER_SLAB = LANES // S5_GROUP
N_SLABS = S5_WIDTH // LANES
PAIRS_PER_SLAB = GROUPS_PER_SLAB // 2
GROUP_K = CHUNK * S5_GROUP
DIR_COLS = 2 * GROUPS_PER_SLAB * S5_STATE
TOK_TILE = 64
TILE_ROWS = SUBLANES * TOK_TILE
LAYOUT_TOK = 32
LAYOUT_ROWS = SUBLANES * LAYOUT_TOK
S5_ROWS = 16 * TILE_ROWS
FF_TILE = 256
ATT_QB = 8
VMEM_LIMIT = 56 * 1024 * 1024


def _rms(x, g):
    return x * lax.rsqrt(jnp.mean(x * x, axis=-1, keepdims=True) + RMS_EPS) * g


def _gelu_tanh(x):
    return 0.5 * x * (1.0 + jnp.tanh(math.sqrt(2.0 / math.pi) * (x + 0.044715 * (x * x * x))))


def _sigmoid(x):
    return 1.0 / (1.0 + jnp.exp(-x))


def _dot(a, b):
    return jnp.dot(a, b, preferred_element_type=F32)


def _const_spec(shape):
    nd = len(shape)
    return pl.BlockSpec(shape, lambda *_: (0,) * nd, pipeline_mode=pl.Buffered(1))


def _params(sem):
    return pltpu.CompilerParams(dimension_semantics=sem, vmem_limit_bytes=VMEM_LIMIT)


def _to_chunk_order():
    cpt = LAYOUT_TOK // CHUNK
    dst = jnp.arange(LAYOUT_ROWS)
    t, c, b = dst // (cpt * SUBLANES), (dst // SUBLANES) % cpt, dst % SUBLANES
    src = b * LAYOUT_TOK + c * CHUNK + t
    return (src[:, None] == jnp.arange(LAYOUT_ROWS)[None, :]).astype(BF16)


def _layout_tiles():
    return [(slice(s * LAYOUT_ROWS, (s + 1) * LAYOUT_ROWS), slice(s * LAYOUT_TOK, (s + 1) * LAYOUT_TOK))
            for s in range(TOK_TILE // LAYOUT_TOK)]


def _tile_rows(ref):
    return jnp.concatenate([ref[:, tok, :].reshape(LAYOUT_ROWS, ref.shape[2]) for _, tok in _layout_tiles()], axis=0)


def _inproj_kernel(x_ref, g_ref, w_ref, perm_ref, u_ref, q_ref, kv_ref):
    bsz = x_ref.shape[0]
    hn = _rms(_tile_rows(x_ref), g_ref[...]).astype(BF16)
    r = _dot(hn, w_ref[...])
    for rows, tok in _layout_tiles():
        u_ref[rows, :] = _dot(perm_ref[...], r[rows, :S5_WIDTH].astype(BF16)).astype(BF16)
        q_ref[:, tok, :] = r[rows, S5_WIDTH:S5_WIDTH + ATT_WIDTH].astype(BF16).reshape(bsz, LAYOUT_TOK, ATT_WIDTH)
        kv_ref[:, tok, :] = r[rows, S5_WIDTH + ATT_WIDTH:].astype(BF16).reshape(bsz, LAYOUT_TOK, 2 * KV_WIDTH)


def _inproj(h3, g, w, perm):
    bsz, seq, _ = h3.shape
    assert bsz * TOK_TILE == TILE_ROWS
    n = w.shape[1]
    tok = lambda width: pl.BlockSpec((bsz, TOK_TILE, width), lambda i: (0, i, 0))
    return pl.pallas_call(
        _inproj_kernel,
        grid=(seq // TOK_TILE,),
        in_specs=[tok(D_MODEL), _const_spec((1, D_MODEL)), _const_spec((D_MODEL, n)),
                  _const_spec((LAYOUT_ROWS, LAYOUT_ROWS))],
        out_specs=[pl.BlockSpec((TILE_ROWS, S5_WIDTH), lambda i: (i, 0)), tok(ATT_WIDTH), tok(2 * KV_WIDTH)],
        out_shape=[jax.ShapeDtypeStruct((bsz * seq, S5_WIDTH), BF16),
                   jax.ShapeDtypeStruct((bsz, seq, ATT_WIDTH), BF16),
                   jax.ShapeDtypeStruct((bsz, seq, 2 * KV_WIDTH), BF16)],
        compiler_params=_params(("parallel",)),
        name="inproj",
    )(h3, g, w, perm)


def _block_transpose(x):
    x = list(x)
    blk = lax.broadcasted_iota(jnp.int32, x[0].shape, 1) // S5_GROUP
    for d in (4, 2, 1):
        keep = (blk & d) == 0
        for i in range(GROUPS_PER_SLAB):
            if i & d:
                continue
            xi, xj = x[i], x[i + d]
            x[i] = jnp.where(keep, xi, pltpu.roll(xj, d * S5_GROUP, 1))
            x[i + d] = jnp.where(keep, pltpu.roll(xi, LANES - d * S5_GROUP, 1), xj)
    return x


def _token_rows(t, k):
    return pl.ds(k * LAYOUT_ROWS + t * (LAYOUT_ROWS // CHUNK), LAYOUT_ROWS // CHUNK)


def _group_rows(u_ref):
    tiles = u_ref.shape[0] // LAYOUT_ROWS
    tok = [jnp.concatenate([u_ref[_token_rows(t, k), :] for k in range(tiles)], axis=0) for t in range(CHUNK)]
    lo = _block_transpose(tok[:CHUNK // 2])
    hi = _block_transpose(tok[CHUNK // 2:])
    return [jnp.concatenate([lo[a], hi[a]], axis=1) for a in range(GROUPS_PER_SLAB)]


def _state_tile(ri, pq):
    return pl.ds(ri * (DIR_COLS // 2) + pq * LANES, LANES)


def _s5_in_kernel(u_ref, w_ref, of_ref, ob_ref):
    g = _group_rows(u_ref)
    for pq in range(PAIRS_PER_SLAB):
        r = _dot(jnp.concatenate([g[2 * pq], g[2 * pq + 1]], axis=1), w_ref[pq])
        for dr, o_ref in enumerate((of_ref, ob_ref)):
            for ri in range(2):
                k = dr * 2 + ri
                o_ref[:, _state_tile(ri, pq)] = r[:, k * LANES:(k + 1) * LANES].astype(BF16)


def _s5_state_in(u, w_in_state, layer, rows=S5_ROWS):
    t = u.shape[0]
    state = jax.ShapeDtypeStruct((t // CHUNK, N_SLABS * DIR_COLS), BF16)
    spec = pl.BlockSpec((rows // CHUNK, DIR_COLS), lambda s, i: (i, s))
    return pl.pallas_call(
        _s5_in_kernel,
        grid=(N_SLABS, t // rows),
        in_specs=[pl.BlockSpec((rows, LANES), lambda s, i: (i, s)),
                  pl.BlockSpec((None,) + w_in_state.shape[1:], lambda s, i: (layer * N_SLABS + s, 0, 0, 0))],
        out_specs=[spec, spec],
        out_shape=[state, state],
        compiler_params=_params(("parallel", "parallel")),
        name="s5_state_in",
    )(u, w_in_state)


def _s5_scan_kernel(xf_ref, xb_ref, af_ref, ab_ref, of_ref, ob_ref, st_ref, *, cpt):
    half = DIR_COLS // 2
    pair_rows = 2 * SUBLANES
    units = xf_ref.shape[1] // DIR_COLS
    lower, upper = slice(0, SUBLANES), slice(SUBLANES, pair_rows)

    @pl.when(pl.program_id(1) == 0)
    def _():
        st_ref[...] = jnp.zeros_like(st_ref)

    def two_chunks(x_ref, a_ref, o_ref, c2, state, reverse):
        rows = pl.ds(pl.multiple_of(c2 * pair_rows, pair_rows), pair_rows)
        first, second = (upper, lower) if reverse else (lower, upper)
        new_state = []
        for un in range(units):
            re = slice(un * DIR_COLS, un * DIR_COLS + half)
            im = slice(un * DIR_COLS + half, (un + 1) * DIR_COLS)
            ar, ai = a_ref[:, re], a_ref[:, im]
            xr = x_ref[rows, re].astype(F32)
            xi = x_ref[rows, im].astype(F32)
            s0r, s0i = state[un]
            s1r, s1i = ar * s0r - ai * s0i + xr[first], ar * s0i + ai * s0r + xi[first]
            s2r, s2i = ar * s1r - ai * s1i + xr[second], ar * s1i + ai * s1r + xi[second]
            enter_r, enter_i = ((s1r, s0r), (s1i, s0i)) if reverse else ((s0r, s1r), (s0i, s1i))
            o_ref[rows, re] = jnp.concatenate(enter_r, axis=0).astype(BF16)
            o_ref[rows, im] = jnp.concatenate(enter_i, axis=0).astype(BF16)
            new_state.append((s2r, s2i))
        return new_state

    def body(i, state):
        fwd, bwd = state
        return (two_chunks(xf_ref, af_ref, of_ref, i, fwd, False),
                two_chunks(xb_ref, ab_ref, ob_ref, cpt // 2 - 1 - i, bwd, True))

    def load(dr):
        return [(st_ref[dr, :, pl.ds(un * DIR_COLS, half)], st_ref[dr, :, pl.ds(un * DIR_COLS + half, half)])
                for un in range(units)]

    fwd, bwd = lax.fori_loop(0, cpt // 2, body, (load(0), load(1)))
    for dr, state in enumerate((fwd, bwd)):
        for un, (sr, si) in enumerate(state):
            st_ref[dr, :, pl.ds(un * DIR_COLS, half)] = sr
            st_ref[dr, :, pl.ds(un * DIR_COLS + half, half)] = si


def _s5_scan(sin_f, sin_b, a16, n_chunks, cpt=64, width=2 * DIR_COLS):
    nct = n_chunks // cpt
    rows = cpt * SUBLANES
    up = pl.BlockSpec((rows, width), lambda c, k: (k, c))
    down = pl.BlockSpec((rows, width), lambda c, k: (nct - 1 - k, c))
    coef = lambda dr: pl.BlockSpec((None, SUBLANES, width), lambda c, k: (dr, 0, c))
    return pl.pallas_call(
        functools.partial(_s5_scan_kernel, cpt=cpt),
        grid=(sin_f.shape[1] // width, nct),
        in_specs=[up, down, coef(0), coef(1)],
        out_specs=[up, down],
        out_shape=[jax.ShapeDtypeStruct(sin_f.shape, BF16), jax.ShapeDtypeStruct(sin_b.shape, BF16)],
        scratch_shapes=[pltpu.VMEM((2, SUBLANES, width), F32)],
        compiler_params=_params(("parallel", "arbitrary")),
        name="s5_scan",
    )(sin_f, sin_b, a16, a16)


def _s5_out_kernel(u_ref, cf_ref, cb_ref, wt_ref, wo_ref, y_ref):
    g = _group_rows(u_ref)
    ys = []
    for pq in range(PAIRS_PER_SLAB):
        carry = jnp.concatenate([c_ref[:, _state_tile(ri, pq)] for c_ref in (cf_ref, cb_ref) for ri in range(2)],
                                axis=1)
        from_state = _dot(carry, wo_ref[pq])
        for a2 in range(2):
            a = 2 * pq + a2
            ys.append((_dot(g[a], wt_ref[a]) + from_state[:, a2 * GROUP_K:(a2 + 1) * GROUP_K]).astype(BF16))
    halves = (_block_transpose([y[:, :LANES] for y in ys]), _block_transpose([y[:, LANES:] for y in ys]))
    per_tile = LAYOUT_ROWS // CHUNK
    for t in range(CHUNK):
        tok = halves[t // (CHUNK // 2)][t % (CHUNK // 2)]
        for k in range(y_ref.shape[0] // LAYOUT_ROWS):
            y_ref[_token_rows(t, k), :] = tok[k * per_tile:(k + 1) * per_tile, :]


def _s5_out(u, carry_f, carry_b, w_toep, w_out_state, layer, rows=S5_ROWS):
    t = u.shape[0]
    state = pl.BlockSpec((rows // CHUNK, DIR_COLS), lambda s, i: (i, s))
    return pl.pallas_call(
        _s5_out_kernel,
        grid=(N_SLABS, t // rows),
        in_specs=[pl.BlockSpec((rows, LANES), lambda s, i: (i, s)), state, state,
                  pl.BlockSpec((None,) + w_toep.shape[1:], lambda s, i: (layer * N_SLABS + s, 0, 0, 0)),
                  pl.BlockSpec((None,) + w_out_state.shape[1:], lambda s, i: (layer * N_SLABS + s, 0, 0, 0))],
        out_specs=pl.BlockSpec((rows, LANES), lambda s, i: (i, s)),
        out_shape=jax.ShapeDtypeStruct((t, S5_WIDTH), BF16),
        compiler_params=_params(("parallel", "parallel")),
        name="s5_out",
    )(u, carry_f, carry_b, w_toep, w_out_state)


def _s5_tables(lam_re, lam_im, log_dt, b_re, b_im, c_re, c_im, d):
    G, P, H, C, S = S5_GROUPS, S5_STATE, S5_GROUP, CHUNK, GROUPS_PER_SLAB
    lr = lam_re.astype(F32)
    li = lam_im.astype(F32)
    dt = jnp.exp(log_dt.astype(F32))[..., None]
    mag = jnp.exp(lr * dt)
    ab_re = mag * jnp.cos(li * dt)
    ab_im = mag * jnp.sin(li * dt)
    nr = ab_re - 1.0
    den = lr * lr + li * li
    coef_re = (nr * lr + ab_im * li) / den
    coef_im = (ab_im * lr - nr * li) / den
    br = b_re.astype(F32)
    bi = b_im.astype(F32)
    bb_re = coef_re[..., None] * br - coef_im[..., None] * bi
    bb_im = coef_re[..., None] * bi + coef_im[..., None] * br
    cr = c_re.astype(F32)
    ci = c_im.astype(F32)

    k = jnp.arange(C + 1, dtype=F32)[:, None, None, None]
    pmag = jnp.exp(k * (lr * dt))
    pw_re = pmag * jnp.cos(k * (li * dt))
    pw_im = pmag * jnp.sin(k * (li * dt))

    n_pairs = G // 2
    pw = jnp.stack([pw_re, pw_im], axis=0)
    up = jnp.arange(C)

    def table(k_fwd, k_bwd):
        return jnp.stack([pw[:, k_fwd, 0], pw[:, k_bwd, 1]], axis=0).transpose(3, 0, 1, 2, 4)

    by_state = lambda tab: tab.transpose(0, 1, 2, 4, 3).reshape(n_pairs, 2, 2, 2, P, C)
    lag_pw = by_state(table(up, C - 1 - up))
    out_pw = by_state(table(up + 1, C - up))
    in_pw = table(C - 1 - up, up).reshape(n_pairs, 2, 2, 2, C, P)
    in_pw = jnp.tile(in_pw, (1, 1, 1, 1, 1, 2))
    ct = jnp.stack([cr, ci], axis=0).transpose(2, 1, 0, 4, 3).reshape(n_pairs, 2, 2, 2, P, H)
    bbt = jnp.stack([bb_re, bb_im], axis=0).transpose(2, 1, 0, 4, 3).reshape(n_pairs, 2, 2, 2, H, P)
    own = jnp.eye(2, dtype=F32)[None, :, None, None, None, :, None]
    bb_own = (bbt[:, :, :, :, :, None, :] * own).reshape(n_pairs, 2, 2, 2, H, 2 * P)
    skip = jnp.eye(H, dtype=F32)[None] * d.astype(F32).reshape(G, 1, H)
    skip = jnp.pad(skip, ((0, 0), (0, 0), ((C - 1) * H, 0))).reshape(n_pairs, 2, H, GROUP_K)

    a16 = jnp.stack([pw_re[C], pw_im[C]], axis=0)
    a16 = a16.reshape(2, 2, N_SLABS, S, P).transpose(1, 2, 0, 3, 4)
    a16 = jnp.broadcast_to(a16.reshape(2, 1, N_SLABS * DIR_COLS), (2, SUBLANES, N_SLABS * DIR_COLS))
    return (ct, bbt, bb_own, lag_pw, out_pw, in_pw, skip), a16


def _s5_weights(*stacked_params):
    depth = stacked_params[0].shape[0]
    tabs, a16 = jax.vmap(_s5_tables)(*stacked_params)
    tabs = [a.reshape((depth * a.shape[1],) + a.shape[2:]) for a in tabs]
    rep_h = jnp.tile(jnp.eye(S5_GROUP, dtype=BF16), (1, CHUNK))
    rep_k = jnp.repeat(jnp.eye(CHUNK, dtype=BF16), S5_GROUP, axis=1)
    wt, wi, wo = _s5_pack(*tabs, rep_h, rep_k)
    wt = wt.reshape(depth * N_SLABS, GROUPS_PER_SLAB, GROUP_K, GROUP_K)
    wi = wi.reshape(depth * N_SLABS, PAIRS_PER_SLAB, 2 * GROUP_K, 4 * LANES)
    wo = wo.reshape(depth * N_SLABS, PAIRS_PER_SLAB, 4 * LANES, 2 * GROUP_K)
    return wi, wt, wo, a16


def _s5_pack_kernel(ct_ref, bbt_ref, bbo_ref, lagp_ref, outp_ref, inp_ref, skip_ref, reph_ref, repk_ref,
                    wt_ref, wi_ref, wo_ref):
    def split(x):
        hi = x.astype(BF16)
        return hi, (x - hi.astype(F32)).astype(BF16)

    def exact_dot(a, b):
        a_hi, a_lo = split(a)
        if b.dtype == BF16:
            return _dot(a_hi, b) + _dot(a_lo, b)
        b_hi, b_lo = split(b)
        return _dot(a_hi, b_hi) + (_dot(a_hi, b_lo) + _dot(a_lo, b_hi))

    rep_h = reph_ref[...]
    rep_k = repk_ref[...]
    wide = 2 * GROUP_K
    for a2 in range(2):
        lag_part = []
        for dr in range(2):
            c_re = exact_dot(ct_ref[a2, dr, 0], rep_h)
            c_im = exact_dot(ct_ref[a2, dr, 1], rep_h)

            def times_power(ref):
                p_re = exact_dot(ref[a2, dr, 0], rep_k)
                p_im = exact_dot(ref[a2, dr, 1], rep_k)
                return c_re * p_re - c_im * p_im, c_re * p_im + c_im * p_re

            l_re, l_im = times_power(lagp_ref)
            lag_part.append(exact_dot(bbt_ref[a2, dr, 0], l_re) - exact_dot(bbt_ref[a2, dr, 1], l_im))
            w_re, w_im = times_power(outp_ref)
            for ri, val in ((0, w_re), (1, -w_im)):
                rows = pl.ds((dr * 2 + ri) * LANES + a2 * S5_STATE, S5_STATE)
                wo_ref[rows, a2 * GROUP_K:(a2 + 1) * GROUP_K] = val.astype(BF16)
                wo_ref[rows, (1 - a2) * GROUP_K:(2 - a2) * GROUP_K] = jnp.zeros((S5_STATE, GROUP_K), BF16)
            b_re = bbo_ref[a2, dr, 0]
            b_im = bbo_ref[a2, dr, 1]
            for t in range(CHUNK):
                e_re = inp_ref[a2, dr, 0, t:t + 1, :]
                e_im = inp_ref[a2, dr, 1, t:t + 1, :]
                rows = pl.ds(a2 * GROUP_K + t * S5_GROUP, S5_GROUP)
                wi_ref[rows, (2 * dr) * LANES:(2 * dr + 1) * LANES] = (b_re * e_re - b_im * e_im).astype(BF16)
                wi_ref[rows, (2 * dr + 1) * LANES:(2 * dr + 2) * LANES] = (b_re * e_im + b_im * e_re).astype(BF16)
        zero = jnp.zeros((S5_GROUP, GROUP_K), F32)
        lags = (jnp.concatenate([lag_part[1] + skip_ref[a2], zero], axis=1)
                + pltpu.roll(jnp.concatenate([lag_part[0], zero], axis=1), (CHUNK - 1) * S5_GROUP, 1))
        for t in range(CHUNK):
            shift = (CHUNK - 1 - t) * S5_GROUP
            window = lags if shift == 0 else pltpu.roll(lags, wide - shift, 1)
            wt_ref[a2, t * S5_GROUP:(t + 1) * S5_GROUP, :] = window[:, :GROUP_K].astype(BF16)


def _s5_pack(ct, bbt, bb_own, lag_pw, out_pw, in_pw, skip, rep_h, rep_k):
    n_pairs = ct.shape[0]
    per_pair = lambda a: pl.BlockSpec((None,) + a.shape[1:], lambda i: (i,) + (0,) * (a.ndim - 1))
    tabs = (ct, bbt, bb_own, lag_pw, out_pw, in_pw, skip)
    return pl.pallas_call(
        _s5_pack_kernel,
        grid=(n_pairs,),
        in_specs=[per_pair(a) for a in tabs] + [_const_spec(rep_h.shape), _const_spec(rep_k.shape)],
        out_specs=[pl.BlockSpec((2, GROUP_K, GROUP_K), lambda i: (i, 0, 0)),
                   pl.BlockSpec((None, 2 * GROUP_K, 4 * LANES), lambda i: (i, 0, 0)),
                   pl.BlockSpec((None, 4 * LANES, 2 * GROUP_K), lambda i: (i, 0, 0))],
        out_shape=[jax.ShapeDtypeStruct((2 * n_pairs, GROUP_K, GROUP_K), BF16),
                   jax.ShapeDtypeStruct((n_pairs, 2 * GROUP_K, 4 * LANES), BF16),
                   jax.ShapeDtypeStruct((n_pairs, 4 * LANES, 2 * GROUP_K), BF16)],
        compiler_params=_params(("parallel",)),
        name="s5_pack",
    )(*tabs, rep_h, rep_k)


def _attn_kernel(sink_ref, q_ref, kl_ref, kc_ref, kr_ref, bias_first_ref, bias_mid_ref, bias_last_ref, o_ref):
    nk = 3 * BLOCK
    kv_all = jnp.concatenate([kl_ref[...], kc_ref[...], kr_ref[...]], axis=0)
    low = lax.broadcasted_iota(jnp.int32, (kv_all.shape[0], LANES), 1) < HEAD_DIM
    zero = jnp.zeros((kv_all.shape[0], LANES), BF16)

    def diag_parts(tile, kvh):
        other = pltpu.roll(tile, HEAD_DIM, 1)
        first, second = (tile, other) if kvh == 0 else (other, tile)
        return jnp.where(low, first, zero), jnp.where(low, zero, second)

    def window(parts, qb):
        return jnp.concatenate([p[qb * BLOCK:qb * BLOCK + nk] for p in parts], axis=0)

    top = lax.broadcasted_iota(jnp.int32, (2 * nk, LANES), 0) < nk
    ones_bd = (top == (lax.broadcasted_iota(jnp.int32, (2 * nk, LANES), 1) < HEAD_DIM)).astype(BF16)

    upper = lax.broadcasted_iota(jnp.int32, (2 * BLOCK, 1), 0) < BLOCK
    bias_refs = [bias_first_ref] + [bias_mid_ref] * (ATT_QB - 2) + [bias_last_ref]
    for kvh in range(N_KV_HEADS):
        k_parts = diag_parts(kv_all[:, :LANES], kvh)
        v_parts = diag_parts(kv_all[:, LANES:], kvh)
        for qb, bias_ref in enumerate(bias_refs):
            rows = slice(qb * BLOCK, (qb + 1) * BLOCK)
            kbd = window(k_parts, qb)
            vbd = jnp.concatenate([window(v_parts, qb), ones_bd], axis=1)
            tiles = [(kvh * 2 + pair) * LANES for pair in range(2)]
            q2 = jnp.concatenate([q_ref[rows, tl:tl + LANES] for tl in tiles], axis=0)
            s = lax.dot_general(q2, kbd, (((1,), (1,)), ((), ())), preferred_element_type=F32)
            s = s + bias_ref[kvh]
            ps, sinks = [], []
            for e in range(2):
                se = s[:, e * nk:(e + 1) * nk]
                sk = jnp.where(upper, sink_ref[4 * kvh + e], sink_ref[4 * kvh + 2 + e])
                mx = jnp.maximum(jnp.max(se, axis=-1, keepdims=True), sk)
                ps.append(jnp.exp2((se - mx).astype(BF16)))
                sinks.append(jnp.broadcast_to(jnp.exp2(sk - mx), (2 * BLOCK, HEAD_DIM)))
            nd = _dot(jnp.concatenate(ps, axis=1), vbd)
            o = (nd[:, :LANES] / (nd[:, LANES:] + jnp.concatenate(sinks, axis=1))).astype(BF16)
            for pair, tl in enumerate(tiles):
                o_ref[rows, tl:tl + LANES] = o[pair * BLOCK:(pair + 1) * BLOCK, :]


def _attention(q, kv, sink, bias, bsz, seq):
    nb = seq // BLOCK
    assert nb % ATT_QB == 0 and ATT_QB >= 2
    nj = nb // ATT_QB
    rows = ATT_QB * BLOCK
    centre = lambda b, j: (b * nj + j, 0)
    left = lambda b, j: (b * nb + jnp.maximum(ATT_QB * j - 1, 0), 0)
    right = lambda b, j: (b * nb + jnp.minimum(ATT_QB * (j + 1), nb - 1), 0)
    variant = (None,) + bias.shape[1:]
    bias_first = pl.BlockSpec(variant, lambda b, j: (jnp.where(j == 0, 0, 1), 0, 0, 0))
    bias_mid = pl.BlockSpec(variant, lambda b, j: (1, 0, 0, 0), pipeline_mode=pl.Buffered(1))
    bias_last = pl.BlockSpec(variant, lambda b, j: (jnp.where(j == nj - 1, 2, 1), 0, 0, 0))
    return pl.pallas_call(
        _attn_kernel,
        grid=(bsz, nj),
        in_specs=[pl.BlockSpec(memory_space=pltpu.SMEM),
                  pl.BlockSpec((rows, ATT_WIDTH), centre),
                  pl.BlockSpec((BLOCK, 2 * KV_WIDTH), left),
                  pl.BlockSpec((rows, 2 * KV_WIDTH), centre),
                  pl.BlockSpec((BLOCK, 2 * KV_WIDTH), right),
                  bias_first, bias_mid, bias_last],
        out_specs=pl.BlockSpec((rows, ATT_WIDTH), centre),
        out_shape=jax.ShapeDtypeStruct((bsz * seq, ATT_WIDTH), BF16),
        compiler_params=_params(("parallel", "parallel")),
        name="attention",
    )(sink, q, kv, kv, kv, bias, bias, bias)


def _t5_bucket(rel):
    half = NUM_BUCKETS // 2
    max_exact = half // 2
    ret = jnp.where(rel > 0, half, 0)
    n = jnp.abs(rel)
    nf = jnp.maximum(n, 1).astype(jnp.float32)
    large = max_exact + (jnp.log(nf / max_exact) / math.log(MAX_DISTANCE / max_exact)
                         * (half - max_exact)).astype(jnp.int32)
    large = jnp.minimum(large, half - 1)
    return ret + jnp.where(n < max_exact, n, large)


def _band_bias(rel_bias):
    q_loc = jnp.arange(BLOCK, dtype=jnp.int32)
    k_loc = jnp.arange(3 * BLOCK, dtype=jnp.int32)
    rel = (k_loc[None, :] - BLOCK) - q_loc[:, None]
    onehot = (_t5_bucket(rel)[None] == jnp.arange(NUM_BUCKETS, dtype=jnp.int32)[:, None, None]).astype(F32)
    bias = jnp.einsum('bh,bqk->hqk', rel_bias.astype(F32), onehot, precision=lax.Precision.HIGHEST)
    bias = jnp.where((jnp.abs(rel) <= WINDOW)[None], bias * LOG2E, NEG_INF)
    keep = jnp.stack([k_loc >= BLOCK, k_loc >= 0, k_loc < 2 * BLOCK], axis=0)
    bias = jnp.where(keep[:, None, None, :], bias[None], NEG_INF)
    bias = bias.reshape(3, N_Q_HEADS // 2, 2, BLOCK, 3 * BLOCK).transpose(0, 1, 3, 2, 4)
    return bias.reshape(3, N_KV_HEADS, 2 * BLOCK, 6 * BLOCK)


def _merge_ffn_kernel(h_ref, y_ref, yb_ref, g1_ref, g2_ref, gf_ref, unperm_ref, wg_ref, wglu_ref, wa_ref, wb_ref,
                      wo_ref, fg_ref, fu_ref, fd_ref, o_ref, acc_ref, *, final_norm):
    bsz = h_ref.shape[0]
    h = _tile_rows(h_ref)
    hn = _rms(h, g1_ref[...]).astype(BF16)
    gates = _sigmoid(_dot(hn, wg_ref[...]))
    y = jnp.concatenate([_dot(unperm_ref[...], y_ref[rows, :]) for rows, _ in _layout_tiles()], axis=0)
    z = _gelu_tanh(y)
    za = (z * _sigmoid(_dot(z.astype(BF16), wglu_ref[...]))).astype(BF16)
    yb = _tile_rows(yb_ref)
    merged = gates[:, :D_MODEL] * _dot(za, wa_ref[...]) + gates[:, D_MODEL:] * _dot(yb, wb_ref[...])
    h = h + _dot(merged.astype(BF16), wo_ref[...])

    hn = _rms(h, g2_ref[...]).astype(BF16)
    acc_ref[...] = h
    for j in range(D_FF // FF_TILE):
        cols = slice(j * FF_TILE, (j + 1) * FF_TILE)
        gate = _dot(hn, fg_ref[:, cols])
        up = _dot(hn, fu_ref[:, cols])
        act = (gate * _sigmoid(gate) * up).astype(BF16)
        acc_ref[...] += _dot(act, fd_ref[cols, :])
    out = acc_ref[...]
    out = _rms(out, gf_ref[...]) if final_norm else out
    for rows, tok in _layout_tiles():
        o_ref[:, tok, :] = out[rows, :].reshape(bsz, LAYOUT_TOK, D_MODEL)


def _merge_ffn(h3, y, yb3, g1, g2, gf, unperm, weights, final_norm):
    bsz, seq, _ = h3.shape
    tok = lambda width: pl.BlockSpec((bsz, TOK_TILE, width), lambda i: (0, i, 0))
    vec = _const_spec((1, D_MODEL))
    return pl.pallas_call(
        functools.partial(_merge_ffn_kernel, final_norm=final_norm),
        grid=(seq // TOK_TILE,),
        in_specs=[tok(D_MODEL), pl.BlockSpec((TILE_ROWS, S5_WIDTH), lambda i: (i, 0)), tok(ATT_WIDTH),
                  vec, vec, vec, _const_spec(unperm.shape)] + [_const_spec(w.shape) for w in weights],
        out_specs=tok(D_MODEL),
        out_shape=jax.ShapeDtypeStruct((bsz, seq, D_MODEL), F32),
        scratch_shapes=[pltpu.VMEM((TILE_ROWS, D_MODEL), F32)],
        compiler_params=_params(("parallel",)),
        name="merge_ffn",
    )(h3, y, yb3, g1, g2, gf, unperm, *weights)


def kernel(x, norm1_g, norm2_g, final_g, w_in, s5_lambda_re, s5_lambda_im, s5_log_dt, s5_b_re, s5_b_im,
           s5_c_re, s5_c_im, s5_d, s5_w_glu, attn_sink, rel_bias, w_branch_a, w_branch_b, w_out,
           ffn_w_gate, ffn_w_up, ffn_w_down):
    bsz, seq, _ = x.shape
    depth = w_in.shape[0]
    assert seq % BLOCK == 0 and seq % CHUNK == 0 and bsz == SUBLANES
    n_chunks = seq // CHUNK
    t = bsz * seq
    bias = _band_bias(rel_bias)
    o_k = S5_WIDTH + ATT_WIDTH
    o_g = o_k + 2 * KV_WIDTH
    col_scale = jnp.concatenate([jnp.ones((S5_WIDTH,), F32), jnp.full((ATT_WIDTH,), LOG2E * HEAD_DIM ** -0.5, F32),
                                 jnp.ones((2 * KV_WIDTH,), F32)])
    gf = final_g.reshape(1, D_MODEL).astype(F32)
    perm = _to_chunk_order()
    unperm = perm.T
    w_in_state, w_toep, w_out_state, a16 = _s5_weights(
        s5_lambda_re, s5_lambda_im, s5_log_dt, s5_b_re, s5_b_im, s5_c_re, s5_c_im, s5_d)
    h = x
    for layer in range(depth):
        g1 = norm1_g[layer].reshape(1, D_MODEL).astype(F32)
        g2 = norm2_g[layer].reshape(1, D_MODEL).astype(F32)
        w_uqkv = (w_in[layer][:, :o_g] * col_scale).astype(BF16)
        w_gates = w_in[layer][:, o_g:].astype(BF16)
        u, q, kv = _inproj(h, g1, w_uqkv, perm)
        sin_f, sin_b = _s5_state_in(u, w_in_state, layer)
        carry_f, carry_b = _s5_scan(sin_f, sin_b, a16[layer], n_chunks)
        y = _s5_out(u, carry_f, carry_b, w_toep, w_out_state, layer)
        yb = _attention(q.reshape(t, ATT_WIDTH), kv.reshape(t, 2 * KV_WIDTH),
                        attn_sink[layer].astype(F32) * LOG2E, bias, bsz, seq)
        weights = [w.astype(BF16) for w in (
            w_gates, s5_w_glu[layer], w_branch_a[layer], w_branch_b[layer], w_out[layer],
            ffn_w_gate[layer], ffn_w_up[layer], ffn_w_down[layer])]
        h = _merge_ffn(h, y, yb.reshape(bsz, seq, ATT_WIDTH), g1, g2, gf, unperm, weights,
                       final_norm=(layer == depth - 1))
    return h
```

```python
import functools
import math

import jax
import jax.numpy as jnp
from jax import lax
from jax.experimental import pallas as pl
from jax.experimental.pallas import tpu as pltpu

F32 = jnp.float32
BF16 = jnp.bfloat16

D_MODEL = 1024
S5_WIDTH = 512
S5_GROUP = 16
S5_GROUPS = 32
S5_STATE = 64
HEAD_DIM = 64
N_Q_HEADS = 8
N_KV_HEADS = 2
Q_PER_KV = N_Q_HEADS // N_KV_HEADS
ATT_WIDTH = N_Q_HEADS * HEAD_DIM
KV_WIDTH = N_KV_HEADS * HEAD_DIM
WINDOW = 128
BLOCK = 128
NUM_BUCKETS = 32
MAX_DISTANCE = 128
D_FF = 2816
RMS_EPS = 1e-6
NEG_INF = -1e30
LOG2E = math.log2(math.e)

LANES = 128
SUBLANES = 8
CHUNK = 16
GROUPS_PER_SLAB = LANES // S5_GROUP
N_SLABS = S5_WIDTH // LANES
PAIRS_PER_SLAB = GROUPS_PER_SLAB // 2
GROUP_K = CHUNK * S5_GROUP
DIR_COLS = 2 * GROUPS_PER_SLAB * S5_STATE
TOK_TILE = 64
TILE_ROWS = SUBLANES * TOK_TILE
LAYOUT_TOK = 32
LAYOUT_ROWS = SUBLANES * LAYOUT_TOK
S5_ROWS = 16 * TILE_ROWS
FF_TILE = 256
ATT_QB = 8
VMEM_LIMIT = 56 * 1024 * 1024


def _rms(x, g):
    return x * lax.rsqrt(jnp.mean(x * x, axis=-1, keepdims=True) + RMS_EPS) * g


def _gelu_tanh(x):
    return 0.5 * x * (1.0 + jnp.tanh(math.sqrt(2.0 / math.pi) * (x + 0.044715 * (x * x * x))))


def _sigmoid(x):
    return 1.0 / (1.0 + jnp.exp(-x))


def _dot(a, b):
    return jnp.dot(a, b, preferred_element_type=F32)


def _const_spec(shape):
    nd = len(shape)
    return pl.BlockSpec(shape, lambda *_: (0,) * nd, pipeline_mode=pl.Buffered(1))


def _params(sem):
    return pltpu.CompilerParams(dimension_semantics=sem, vmem_limit_bytes=VMEM_LIMIT)


def _to_chunk_order():
    cpt = LAYOUT_TOK // CHUNK
    dst = jnp.arange(LAYOUT_ROWS)
    t, c, b = dst // (cpt * SUBLANES), (dst // SUBLANES) % cpt, dst % SUBLANES
    src = b * LAYOUT_TOK + c * CHUNK + t
    return (src[:, None] == jnp.arange(LAYOUT_ROWS)[None, :]).astype(BF16)


def _layout_tiles():
    return [(slice(s * LAYOUT_ROWS, (s + 1) * LAYOUT_ROWS), slice(s * LAYOUT_TOK, (s + 1) * LAYOUT_TOK))
            for s in range(TOK_TILE // LAYOUT_TOK)]


def _tile_rows(ref):
    return jnp.concatenate([ref[:, tok, :].reshape(LAYOUT_ROWS, ref.shape[2]) for _, tok in _layout_tiles()], axis=0)


def _inproj_rows(h, g_ref, w_ref, perm_ref, u_ref, q_ref, kv_ref):
    bsz = q_ref.shape[0]
    hn = _rms(h, g_ref[...]).astype(BF16)
    r = _dot(hn, w_ref[...])
    for rows, tok in _layout_tiles():
        u_ref[rows, :] = _dot(perm_ref[...], r[rows, :S5_WIDTH].astype(BF16)).astype(BF16)
        q_ref[:, tok, :] = r[rows, S5_WIDTH:S5_WIDTH + ATT_WIDTH].astype(BF16).reshape(bsz, LAYOUT_TOK, ATT_WIDTH)
        kv_ref[:, tok, :] = r[rows, S5_WIDTH + ATT_WIDTH:].astype(BF16).reshape(bsz, LAYOUT_TOK, 2 * KV_WIDTH)


def _inproj_kernel(x_ref, g_ref, w_ref, perm_ref, u_ref, q_ref, kv_ref):
    _inproj_rows(_tile_rows(x_ref), g_ref, w_ref, perm_ref, u_ref, q_ref, kv_ref)


def _tok_spec(bsz, width):
    return pl.BlockSpec((bsz, TOK_TILE, width), lambda i: (0, i, 0))


def _inproj_out(bsz, seq):
    specs = [pl.BlockSpec((TILE_ROWS, S5_WIDTH), lambda i: (i, 0)), _tok_spec(bsz, ATT_WIDTH),
             _tok_spec(bsz, 2 * KV_WIDTH)]
    shapes = [jax.ShapeDtypeStruct((bsz * seq, S5_WIDTH), BF16),
              jax.ShapeDtypeStruct((bsz, seq, ATT_WIDTH), BF16),
              jax.ShapeDtypeStruct((bsz, seq, 2 * KV_WIDTH), BF16)]
    return specs, shapes


def _inproj(h3, g, w, perm):
    bsz, seq, _ = h3.shape
    assert bsz * TOK_TILE == TILE_ROWS
    specs, shapes = _inproj_out(bsz, seq)
    return pl.pallas_call(
        _inproj_kernel,
        grid=(seq // TOK_TILE,),
        in_specs=[_tok_spec(bsz, D_MODEL), _const_spec((1, D_MODEL)), _const_spec(w.shape), _const_spec(perm.shape)],
        out_specs=specs,
        out_shape=shapes,
        compiler_params=_params(("parallel",)),
        name="inproj",
    )(h3, g, w, perm)


def _block_transpose(x):
    x = list(x)
    blk = lax.broadcasted_iota(jnp.int32, x[0].shape, 1) // S5_GROUP
    for d in (4, 2, 1):
        keep = (blk & d) == 0
        for i in range(GROUPS_PER_SLAB):
            if i & d:
                continue
            xi, xj = x[i], x[i + d]
            x[i] = jnp.where(keep, xi, pltpu.roll(xj, d * S5_GROUP, 1))
            x[i + d] = jnp.where(keep, pltpu.roll(xi, LANES - d * S5_GROUP, 1), xj)
    return x


def _token_rows(t, k):
    return pl.ds(k * LAYOUT_ROWS + t * (LAYOUT_ROWS // CHUNK), LAYOUT_ROWS // CHUNK)


def _group_rows(u_ref):
    tiles = u_ref.shape[0] // LAYOUT_ROWS
    tok = [jnp.concatenate([u_ref[_token_rows(t, k), :] for k in range(tiles)], axis=0) for t in range(CHUNK)]
    lo = _block_transpose(tok[:CHUNK // 2])
    hi = _block_transpose(tok[CHUNK // 2:])
    return [jnp.concatenate([lo[a], hi[a]], axis=1) for a in range(GROUPS_PER_SLAB)]


def _state_tile(ri, pq):
    return pl.ds(ri * (DIR_COLS // 2) + pq * LANES, LANES)


def _s5_in_kernel(u_ref, w_ref, of_ref, ob_ref):
    g = _group_rows(u_ref)
    for pq in range(PAIRS_PER_SLAB):
        r = _dot(jnp.concatenate([g[2 * pq], g[2 * pq + 1]], axis=1), w_ref[pq])
        for dr, o_ref in enumerate((of_ref, ob_ref)):
            for ri in range(2):
                k = dr * 2 + ri
                o_ref[:, _state_tile(ri, pq)] = r[:, k * LANES:(k + 1) * LANES].astype(BF16)


def _s5_state_in(u, w_in_state, layer, rows=S5_ROWS):
    t = u.shape[0]
    state = jax.ShapeDtypeStruct((t // CHUNK, N_SLABS * DIR_COLS), BF16)
    spec = pl.BlockSpec((rows // CHUNK, DIR_COLS), lambda s, i: (i, s))
    return pl.pallas_call(
        _s5_in_kernel,
        grid=(N_SLABS, t // rows),
        in_specs=[pl.BlockSpec((rows, LANES), lambda s, i: (i, s)),
                  pl.BlockSpec((None,) + w_in_state.shape[1:], lambda s, i: (layer * N_SLABS + s, 0, 0, 0))],
        out_specs=[spec, spec],
        out_shape=[state, state],
        compiler_params=_params(("parallel", "parallel")),
        name="s5_state_in",
    )(u, w_in_state)


def _s5_scan_kernel(xf_ref, xb_ref, af_ref, ab_ref, of_ref, ob_ref, st_ref, *, cpt):
    half = DIR_COLS // 2
    pair_rows = 2 * SUBLANES
    units = xf_ref.shape[1] // DIR_COLS
    lower, upper = slice(0, SUBLANES), slice(SUBLANES, pair_rows)

    @pl.when(pl.program_id(1) == 0)
    def _():
        st_ref[...] = jnp.zeros_like(st_ref)

    def two_chunks(x_ref, a_ref, o_ref, c2, state, reverse):
        rows = pl.ds(pl.multiple_of(c2 * pair_rows, pair_rows), pair_rows)
        first, second = (upper, lower) if reverse else (lower, upper)
        new_state = []
        for un in range(units):
            re = slice(un * DIR_COLS, un * DIR_COLS + half)
            im = slice(un * DIR_COLS + half, (un + 1) * DIR_COLS)
            ar, ai = a_ref[:, re], a_ref[:, im]
            xr = x_ref[rows, re].astype(F32)
            xi = x_ref[rows, im].astype(F32)
            s0r, s0i = state[un]
            s1r, s1i = ar * s0r - ai * s0i + xr[first], ar * s0i + ai * s0r + xi[first]
            s2r, s2i = ar * s1r - ai * s1i + xr[second], ar * s1i + ai * s1r + xi[second]
            enter_r, enter_i = ((s1r, s0r), (s1i, s0i)) if reverse else ((s0r, s1r), (s0i, s1i))
            o_ref[rows, re] = jnp.concatenate(enter_r, axis=0).astype(BF16)
            o_ref[rows, im] = jnp.concatenate(enter_i, axis=0).astype(BF16)
            new_state.append((s2r, s2i))
        return new_state

    def body(i, state):
        fwd, bwd = state
        return (two_chunks(xf_ref, af_ref, of_ref, i, fwd, False),
                two_chunks(xb_ref, ab_ref, ob_ref, cpt // 2 - 1 - i, bwd, True))

    def load(dr):
        return [(st_ref[dr, :, pl.ds(un * DIR_COLS, half)], st_ref[dr, :, pl.ds(un * DIR_COLS + half, half)])
                for un in range(units)]

    fwd, bwd = lax.fori_loop(0, cpt // 2, body, (load(0), load(1)))
    for dr, state in enumerate((fwd, bwd)):
        for un, (sr, si) in enumerate(state):
            st_ref[dr, :, pl.ds(un * DIR_COLS, half)] = sr
            st_ref[dr, :, pl.ds(un * DIR_COLS + half, half)] = si


def _s5_scan(sin_f, sin_b, a16, n_chunks, cpt=64, width=2 * DIR_COLS):
    nct = n_chunks // cpt
    rows = cpt * SUBLANES
    up = pl.BlockSpec((rows, width), lambda c, k: (k, c))
    down = pl.BlockSpec((rows, width), lambda c, k: (nct - 1 - k, c))
    coef = lambda dr: pl.BlockSpec((None, SUBLANES, width), lambda c, k: (dr, 0, c))
    return pl.pallas_call(
        functools.partial(_s5_scan_kernel, cpt=cpt),
        grid=(sin_f.shape[1] // width, nct),
        in_specs=[up, down, coef(0), coef(1)],
        out_specs=[up, down],
        out_shape=[jax.ShapeDtypeStruct(sin_f.shape, BF16), jax.ShapeDtypeStruct(sin_b.shape, BF16)],
        scratch_shapes=[pltpu.VMEM((2, SUBLANES, width), F32)],
        compiler_params=_params(("parallel", "arbitrary")),
        name="s5_scan",
    )(sin_f, sin_b, a16, a16)


def _s5_out_kernel(u_ref, cf_ref, cb_ref, wt_ref, wo_ref, y_ref):
    g = _group_rows(u_ref)
    ys = []
    for pq in range(PAIRS_PER_SLAB):
        carry = jnp.concatenate([c_ref[:, _state_tile(ri, pq)] for c_ref in (cf_ref, cb_ref) for ri in range(2)],
                                axis=1)
        from_state = _dot(carry, wo_ref[pq])
        for a2 in range(2):
            a = 2 * pq + a2
            ys.append((_dot(g[a], wt_ref[a]) + from_state[:, a2 * GROUP_K:(a2 + 1) * GROUP_K]).astype(BF16))
    halves = (_block_transpose([y[:, :LANES] for y in ys]), _block_transpose([y[:, LANES:] for y in ys]))
    per_tile = LAYOUT_ROWS // CHUNK
    for t in range(CHUNK):
        tok = halves[t // (CHUNK // 2)][t % (CHUNK // 2)]
        for k in range(y_ref.shape[0] // LAYOUT_ROWS):
            y_ref[_token_rows(t, k), :] = tok[k * per_tile:(k + 1) * per_tile, :]


def _s5_out(u, carry_f, carry_b, w_toep, w_out_state, layer, rows=S5_ROWS):
    t = u.shape[0]
    state = pl.BlockSpec((rows // CHUNK, DIR_COLS), lambda s, i: (i, s))
    return pl.pallas_call(
        _s5_out_kernel,
        grid=(N_SLABS, t // rows),
        in_specs=[pl.BlockSpec((rows, LANES), lambda s, i: (i, s)), state, state,
                  pl.BlockSpec((None,) + w_toep.shape[1:], lambda s, i: (layer * N_SLABS + s, 0, 0, 0)),
                  pl.BlockSpec((None,) + w_out_state.shape[1:], lambda s, i: (layer * N_SLABS + s, 0, 0, 0))],
        out_specs=pl.BlockSpec((rows, LANES), lambda s, i: (i, s)),
        out_shape=jax.ShapeDtypeStruct((t, S5_WIDTH), BF16),
        compiler_params=_params(("parallel", "parallel")),
        name="s5_out",
    )(u, carry_f, carry_b, w_toep, w_out_state)


def _s5_tables(lam_re, lam_im, log_dt, b_re, b_im, c_re, c_im, d):
    G, P, H, C, S = S5_GROUPS, S5_STATE, S5_GROUP, CHUNK, GROUPS_PER_SLAB
    lr = lam_re.astype(F32)
    li = lam_im.astype(F32)
    dt = jnp.exp(log_dt.astype(F32))[..., None]
    mag = jnp.exp(lr * dt)
    ab_re = mag * jnp.cos(li * dt)
    ab_im = mag * jnp.sin(li * dt)
    nr = ab_re - 1.0
    den = lr * lr + li * li
    coef_re = (nr * lr + ab_im * li) / den
    coef_im = (ab_im * lr - nr * li) / den
    br = b_re.astype(F32)
    bi = b_im.astype(F32)
    bb_re = coef_re[..., None] * br - coef_im[..., None] * bi
    bb_im = coef_re[..., None] * bi + coef_im[..., None] * br
    cr = c_re.astype(F32)
    ci = c_im.astype(F32)

    k = jnp.arange(C + 1, dtype=F32)[:, None, None, None]
    pmag = jnp.exp(k * (lr * dt))
    pw_re = pmag * jnp.cos(k * (li * dt))
    pw_im = pmag * jnp.sin(k * (li * dt))

    n_pairs = G // 2
    pw = jnp.stack([pw_re, pw_im], axis=0)
    up = jnp.arange(C)

    def table(k_fwd, k_bwd):
        return jnp.stack([pw[:, k_fwd, 0], pw[:, k_bwd, 1]], axis=0).transpose(3, 0, 1, 2, 4)

    by_state = lambda tab: tab.transpose(0, 1, 2, 4, 3).reshape(n_pairs, 2, 2, 2, P, C)
    lag_pw = by_state(table(up, C - 1 - up))
    out_pw = by_state(table(up + 1, C - up))
    in_pw = table(C - 1 - up, up).reshape(n_pairs, 2, 2, 2, C, P)
    in_pw = jnp.tile(in_pw, (1, 1, 1, 1, 1, 2))
    ct = jnp.stack([cr, ci], axis=0).transpose(2, 1, 0, 4, 3).reshape(n_pairs, 2, 2, 2, P, H)
    bbt = jnp.stack([bb_re, bb_im], axis=0).transpose(2, 1, 0, 4, 3).reshape(n_pairs, 2, 2, 2, H, P)
    own = jnp.eye(2, dtype=F32)[None, :, None, None, None, :, None]
    bb_own = (bbt[:, :, :, :, :, None, :] * own).reshape(n_pairs, 2, 2, 2, H, 2 * P)
    skip = jnp.eye(H, dtype=F32)[None] * d.astype(F32).reshape(G, 1, H)
    skip = jnp.pad(skip, ((0, 0), (0, 0), ((C - 1) * H, 0))).reshape(n_pairs, 2, H, GROUP_K)

    a16 = jnp.stack([pw_re[C], pw_im[C]], axis=0)
    a16 = a16.reshape(2, 2, N_SLABS, S, P).transpose(1, 2, 0, 3, 4)
    a16 = jnp.broadcast_to(a16.reshape(2, 1, N_SLABS * DIR_COLS), (2, SUBLANES, N_SLABS * DIR_COLS))
    return (ct, bbt, bb_own, lag_pw, out_pw, in_pw, skip), a16


def _s5_weights(*stacked_params):
    depth = stacked_params[0].shape[0]
    tabs, a16 = jax.vmap(_s5_tables)(*stacked_params)
    tabs = [a.reshape((depth * a.shape[1],) + a.shape[2:]) for a in tabs]
    rep_h = jnp.tile(jnp.eye(S5_GROUP, dtype=BF16), (1, CHUNK))
    rep_k = jnp.repeat(jnp.eye(CHUNK, dtype=BF16), S5_GROUP, axis=1)
    wt, wi, wo = _s5_pack(*tabs, rep_h, rep_k)
    wt = wt.reshape(depth * N_SLABS, GROUPS_PER_SLAB, GROUP_K, GROUP_K)
    wi = wi.reshape(depth * N_SLABS, PAIRS_PER_SLAB, 2 * GROUP_K, 4 * LANES)
    wo = wo.reshape(depth * N_SLABS, PAIRS_PER_SLAB, 4 * LANES, 2 * GROUP_K)
    return wi, wt, wo, a16


def _s5_pack_kernel(ct_ref, bbt_ref, bbo_ref, lagp_ref, outp_ref, inp_ref, skip_ref, reph_ref, repk_ref,
                    wt_ref, wi_ref, wo_ref):
    def split(x):
        hi = x.astype(BF16)
        return hi, (x - hi.astype(F32)).astype(BF16)

    def exact_dot(a, b):
        a_hi, a_lo = split(a)
        if b.dtype == BF16:
            return _dot(a_hi, b) + _dot(a_lo, b)
        b_hi, b_lo = split(b)
        return _dot(a_hi, b_hi) + (_dot(a_hi, b_lo) + _dot(a_lo, b_hi))

    rep_h = reph_ref[...]
    rep_k = repk_ref[...]
    wide = 2 * GROUP_K
    for a2 in range(2):
        lag_part = []
        for dr in range(2):
            c_re = exact_dot(ct_ref[a2, dr, 0], rep_h)
            c_im = exact_dot(ct_ref[a2, dr, 1], rep_h)

            def times_power(ref):
                p_re = exact_dot(ref[a2, dr, 0], rep_k)
                p_im = exact_dot(ref[a2, dr, 1], rep_k)
                return c_re * p_re - c_im * p_im, c_re * p_im + c_im * p_re

            l_re, l_im = times_power(lagp_ref)
            lag_part.append(exact_dot(bbt_ref[a2, dr, 0], l_re) - exact_dot(bbt_ref[a2, dr, 1], l_im))
            w_re, w_im = times_power(outp_ref)
            for ri, val in ((0, w_re), (1, -w_im)):
                rows = pl.ds((dr * 2 + ri) * LANES + a2 * S5_STATE, S5_STATE)
                wo_ref[rows, a2 * GROUP_K:(a2 + 1) * GROUP_K] = val.astype(BF16)
                wo_ref[rows, (1 - a2) * GROUP_K:(2 - a2) * GROUP_K] = jnp.zeros((S5_STATE, GROUP_K), BF16)
            b_re = bbo_ref[a2, dr, 0]
            b_im = bbo_ref[a2, dr, 1]
            for t in range(CHUNK):
                e_re = inp_ref[a2, dr, 0, t:t + 1, :]
                e_im = inp_ref[a2, dr, 1, t:t + 1, :]
                rows = pl.ds(a2 * GROUP_K + t * S5_GROUP, S5_GROUP)
                wi_ref[rows, (2 * dr) * LANES:(2 * dr + 1) * LANES] = (b_re * e_re - b_im * e_im).astype(BF16)
                wi_ref[rows, (2 * dr + 1) * LANES:(2 * dr + 2) * LANES] = (b_re * e_im + b_im * e_re).astype(BF16)
        zero = jnp.zeros((S5_GROUP, GROUP_K), F32)
        lags = (jnp.concatenate([lag_part[1] + skip_ref[a2], zero], axis=1)
                + pltpu.roll(jnp.concatenate([lag_part[0], zero], axis=1), (CHUNK - 1) * S5_GROUP, 1))
        for t in range(CHUNK):
            shift = (CHUNK - 1 - t) * S5_GROUP
            window = lags if shift == 0 else pltpu.roll(lags, wide - shift, 1)
            wt_ref[a2, t * S5_GROUP:(t + 1) * S5_GROUP, :] = window[:, :GROUP_K].astype(BF16)


def _s5_pack(ct, bbt, bb_own, lag_pw, out_pw, in_pw, skip, rep_h, rep_k):
    n_pairs = ct.shape[0]
    per_pair = lambda a: pl.BlockSpec((None,) + a.shape[1:], lambda i: (i,) + (0,) * (a.ndim - 1))
    tabs = (ct, bbt, bb_own, lag_pw, out_pw, in_pw, skip)
    return pl.pallas_call(
        _s5_pack_kernel,
        grid=(n_pairs,),
        in_specs=[per_pair(a) for a in tabs] + [_const_spec(rep_h.shape), _const_spec(rep_k.shape)],
        out_specs=[pl.BlockSpec((2, GROUP_K, GROUP_K), lambda i: (i, 0, 0)),
                   pl.BlockSpec((None, 2 * GROUP_K, 4 * LANES), lambda i: (i, 0, 0)),
                   pl.BlockSpec((None, 4 * LANES, 2 * GROUP_K), lambda i: (i, 0, 0))],
        out_shape=[jax.ShapeDtypeStruct((2 * n_pairs, GROUP_K, GROUP_K), BF16),
                   jax.ShapeDtypeStruct((n_pairs, 2 * GROUP_K, 4 * LANES), BF16),
                   jax.ShapeDtypeStruct((n_pairs, 4 * LANES, 2 * GROUP_K), BF16)],
        compiler_params=_params(("parallel",)),
        name="s5_pack",
    )(*tabs, rep_h, rep_k)


def _attn_kernel(sink_ref, q_ref, kl_ref, kc_ref, kr_ref, bias_first_ref, bias_mid_ref, bias_last_ref, o_ref):
    nk = 3 * BLOCK
    kv_all = jnp.concatenate([kl_ref[...], kc_ref[...], kr_ref[...]], axis=0)
    low = lax.broadcasted_iota(jnp.int32, (kv_all.shape[0], LANES), 1) < HEAD_DIM
    zero = jnp.zeros((kv_all.shape[0], LANES), BF16)

    def diag_parts(tile, kvh):
        other = pltpu.roll(tile, HEAD_DIM, 1)
        first, second = (tile, other) if kvh == 0 else (other, tile)
        return jnp.where(low, first, zero), jnp.where(low, zero, second)

    def window(parts, qb):
        return jnp.concatenate([p[qb * BLOCK:qb * BLOCK + nk] for p in parts], axis=0)

    top = lax.broadcasted_iota(jnp.int32, (2 * nk, LANES), 0) < nk
    ones_bd = (top == (lax.broadcasted_iota(jnp.int32, (2 * nk, LANES), 1) < HEAD_DIM)).astype(BF16)

    upper = lax.broadcasted_iota(jnp.int32, (2 * BLOCK, 1), 0) < BLOCK
    bias_refs = [bias_first_ref] + [bias_mid_ref] * (ATT_QB - 2) + [bias_last_ref]
    for kvh in range(N_KV_HEADS):
        k_parts = diag_parts(kv_all[:, :LANES], kvh)
        v_parts = diag_parts(kv_all[:, LANES:], kvh)
        for qb, bias_ref in enumerate(bias_refs):
            rows = slice(qb * BLOCK, (qb + 1) * BLOCK)
            kbd = window(k_parts, qb)
            vbd = jnp.concatenate([window(v_parts, qb), ones_bd], axis=1)
            tiles = [(kvh * 2 + pair) * LANES for pair in range(2)]
            q2 = jnp.concatenate([q_ref[rows, tl:tl + LANES] for tl in tiles], axis=0)
            s = lax.dot_general(q2, kbd, (((1,), (1,)), ((), ())), preferred_element_type=F32)
            s = s + bias_ref[kvh]
            ps, sinks = [], []
            for e in range(2):
                se = s[:, e * nk:(e + 1) * nk]
                sk = jnp.where(upper, sink_ref[4 * kvh + e], sink_ref[4 * kvh + 2 + e])
                mx = jnp.maximum(jnp.max(se, axis=-1, keepdims=True), sk)
                ps.append(jnp.exp2((se - mx).astype(BF16)))
                sinks.append(jnp.broadcast_to(jnp.exp2(sk - mx), (2 * BLOCK, HEAD_DIM)))
            nd = _dot(jnp.concatenate(ps, axis=1), vbd)
            o = (nd[:, :LANES] / (nd[:, LANES:] + jnp.concatenate(sinks, axis=1))).astype(BF16)
            for pair, tl in enumerate(tiles):
                o_ref[rows, tl:tl + LANES] = o[pair * BLOCK:(pair + 1) * BLOCK, :]


def _attention(q, kv, sink, bias, bsz, seq):
    nb = seq // BLOCK
    assert nb % ATT_QB == 0 and ATT_QB >= 2
    nj = nb // ATT_QB
    rows = ATT_QB * BLOCK
    centre = lambda b, j: (b * nj + j, 0)
    left = lambda b, j: (b * nb + jnp.maximum(ATT_QB * j - 1, 0), 0)
    right = lambda b, j: (b * nb + jnp.minimum(ATT_QB * (j + 1), nb - 1), 0)
    variant = (None,) + bias.shape[1:]
    bias_first = pl.BlockSpec(variant, lambda b, j: (jnp.where(j == 0, 0, 1), 0, 0, 0))
    bias_mid = pl.BlockSpec(variant, lambda b, j: (1, 0, 0, 0), pipeline_mode=pl.Buffered(1))
    bias_last = pl.BlockSpec(variant, lambda b, j: (jnp.where(j == nj - 1, 2, 1), 0, 0, 0))
    return pl.pallas_call(
        _attn_kernel,
        grid=(bsz, nj),
        in_specs=[pl.BlockSpec(memory_space=pltpu.SMEM),
                  pl.BlockSpec((rows, ATT_WIDTH), centre),
                  pl.BlockSpec((BLOCK, 2 * KV_WIDTH), left),
                  pl.BlockSpec((rows, 2 * KV_WIDTH), centre),
                  pl.BlockSpec((BLOCK, 2 * KV_WIDTH), right),
                  bias_first, bias_mid, bias_last],
        out_specs=pl.BlockSpec((rows, ATT_WIDTH), centre),
        out_shape=jax.ShapeDtypeStruct((bsz * seq, ATT_WIDTH), BF16),
        compiler_params=_params(("parallel", "parallel")),
        name="attention",
    )(sink, q, kv, kv, kv, bias, bias, bias)


def _t5_bucket(rel):
    half = NUM_BUCKETS // 2
    max_exact = half // 2
    ret = jnp.where(rel > 0, half, 0)
    n = jnp.abs(rel)
    nf = jnp.maximum(n, 1).astype(jnp.float32)
    large = max_exact + (jnp.log(nf / max_exact) / math.log(MAX_DISTANCE / max_exact)
                         * (half - max_exact)).astype(jnp.int32)
    large = jnp.minimum(large, half - 1)
    return ret + jnp.where(n < max_exact, n, large)


def _band_bias(rel_bias):
    q_loc = jnp.arange(BLOCK, dtype=jnp.int32)
    k_loc = jnp.arange(3 * BLOCK, dtype=jnp.int32)
    rel = (k_loc[None, :] - BLOCK) - q_loc[:, None]
    onehot = (_t5_bucket(rel)[None] == jnp.arange(NUM_BUCKETS, dtype=jnp.int32)[:, None, None]).astype(F32)
    bias = jnp.einsum('bh,bqk->hqk', rel_bias.astype(F32), onehot, precision=lax.Precision.HIGHEST)
    bias = jnp.where((jnp.abs(rel) <= WINDOW)[None], bias * LOG2E, NEG_INF)
    keep = jnp.stack([k_loc >= BLOCK, k_loc >= 0, k_loc < 2 * BLOCK], axis=0)
    bias = jnp.where(keep[:, None, None, :], bias[None], NEG_INF)
    bias = bias.reshape(3, N_Q_HEADS // 2, 2, BLOCK, 3 * BLOCK).transpose(0, 1, 3, 2, 4)
    return bias.reshape(3, N_KV_HEADS, 2 * BLOCK, 6 * BLOCK)


def _merge_ffn_body(h_ref, y_ref, yb_ref, g1_ref, g2_ref, gf_ref, unperm_ref, wg_ref, wglu_ref, wa_ref, wb_ref,
                    wo_ref, fg_ref, fu_ref, fd_ref, o_ref, acc_ref, *, nxt, final_norm):
    bsz = h_ref.shape[0]
    h = _tile_rows(h_ref)
    hn = _rms(h, g1_ref[...]).astype(BF16)
    gates = _sigmoid(_dot(hn, wg_ref[...]))
    y = jnp.concatenate([_dot(unperm_ref[...], y_ref[rows, :]) for rows, _ in _layout_tiles()], axis=0)
    z = _gelu_tanh(y)
    za = (z * _sigmoid(_dot(z.astype(BF16), wglu_ref[...]))).astype(BF16)
    yb = _tile_rows(yb_ref)
    merged = gates[:, :D_MODEL] * _dot(za, wa_ref[...]) + gates[:, D_MODEL:] * _dot(yb, wb_ref[...])
    h = h + _dot(merged.astype(BF16), wo_ref[...])

    hn = _rms(h, g2_ref[...]).astype(BF16)
    acc_ref[...] = h
    for j in range(D_FF // FF_TILE):
        cols = slice(j * FF_TILE, (j + 1) * FF_TILE)
        gate = _dot(hn, fg_ref[:, cols])
        up = _dot(hn, fu_ref[:, cols])
        act = (gate * _sigmoid(gate) * up).astype(BF16)
        acc_ref[...] += _dot(act, fd_ref[cols, :])
    out = acc_ref[...]
    if final_norm:
        out = _rms(out, gf_ref[...])
    else:
        gn_ref, wn_ref, perm_ref, u_ref, q_ref, kv_ref = nxt
        _inproj_rows(out, gn_ref, wn_ref, perm_ref, u_ref, q_ref, kv_ref)
    for rows, tok in _layout_tiles():
        o_ref[:, tok, :] = out[rows, :].reshape(bsz, LAYOUT_TOK, D_MODEL)


def _merge_ffn_last_kernel(*refs):
    *ins, o_ref, acc_ref = refs
    _merge_ffn_body(*ins, o_ref, acc_ref, nxt=None, final_norm=True)


def _merge_ffn_next_kernel(*refs):
    *ins, gn_ref, wn_ref, perm_ref, o_ref, u_ref, q_ref, kv_ref, acc_ref = refs
    _merge_ffn_body(*ins, o_ref, acc_ref, nxt=(gn_ref, wn_ref, perm_ref, u_ref, q_ref, kv_ref), final_norm=False)


def _merge_ffn(h3, y, yb3, g1, g2, gf, unperm, weights, next_inproj=None):
    bsz, seq, _ = h3.shape
    vec = _const_spec((1, D_MODEL))
    in_specs = [_tok_spec(bsz, D_MODEL), pl.BlockSpec((TILE_ROWS, S5_WIDTH), lambda i: (i, 0)),
                _tok_spec(bsz, ATT_WIDTH), vec, vec, vec, _const_spec(unperm.shape)]
    in_specs += [_const_spec(w.shape) for w in weights]
    args = [h3, y, yb3, g1, g2, gf, unperm, *weights]
    out_specs, out_shape = [_tok_spec(bsz, D_MODEL)], [jax.ShapeDtypeStruct((bsz, seq, D_MODEL), F32)]
    if next_inproj is None:
        body = _merge_ffn_last_kernel
    else:
        body = _merge_ffn_next_kernel
        in_specs += [_const_spec(a.shape) for a in next_inproj]
        args += list(next_inproj)
        specs, shapes = _inproj_out(bsz, seq)
        out_specs += specs
        out_shape += shapes
    outs = pl.pallas_call(
        body,
        grid=(seq // TOK_TILE,),
        in_specs=in_specs,
        out_specs=out_specs,
        out_shape=out_shape,
        scratch_shapes=[pltpu.VMEM((TILE_ROWS, D_MODEL), F32)],
        compiler_params=_params(("parallel",)),
        name="merge_ffn",
    )(*args)
    return outs[0] if next_inproj is None else outs


def kernel(x, norm1_g, norm2_g, final_g, w_in, s5_lambda_re, s5_lambda_im, s5_log_dt, s5_b_re, s5_b_im,
           s5_c_re, s5_c_im, s5_d, s5_w_glu, attn_sink, rel_bias, w_branch_a, w_branch_b, w_out,
           ffn_w_gate, ffn_w_up, ffn_w_down):
    bsz, seq, _ = x.shape
    depth = w_in.shape[0]
    assert seq % BLOCK == 0 and seq % CHUNK == 0 and bsz == SUBLANES
    n_chunks = seq // CHUNK
    t = bsz * seq
    bias = _band_bias(rel_bias)
    o_k = S5_WIDTH + ATT_WIDTH
    o_g = o_k + 2 * KV_WIDTH
    col_scale = jnp.concatenate([jnp.ones((S5_WIDTH,), F32), jnp.full((ATT_WIDTH,), LOG2E * HEAD_DIM ** -0.5, F32),
                                 jnp.ones((2 * KV_WIDTH,), F32)])
    gf = final_g.reshape(1, D_MODEL).astype(F32)
    perm = _to_chunk_order()
    unperm = perm.T
    w_in_state, w_toep, w_out_state, a16 = _s5_weights(
        s5_lambda_re, s5_lambda_im, s5_log_dt, s5_b_re, s5_b_im, s5_c_re, s5_c_im, s5_d)
    gain1 = lambda layer: norm1_g[layer].reshape(1, D_MODEL).astype(F32)
    w_uqkv = lambda layer: (w_in[layer][:, :o_g] * col_scale).astype(BF16)
    h = x
    u, q, kv = _inproj(h, gain1(0), w_uqkv(0), perm)
    for layer in range(depth):
        g1 = gain1(layer)
        g2 = norm2_g[layer].reshape(1, D_MODEL).astype(F32)
        w_gates = w_in[layer][:, o_g:].astype(BF16)
        sin_f, sin_b = _s5_state_in(u, w_in_state, layer)
        carry_f, carry_b = _s5_scan(sin_f, sin_b, a16[layer], n_chunks)
        y = _s5_out(u, carry_f, carry_b, w_toep, w_out_state, layer)
        yb = _attention(q.reshape(t, ATT_WIDTH), kv.reshape(t, 2 * KV_WIDTH),
                        attn_sink[layer].astype(F32) * LOG2E, bias, bsz, seq)
        weights = [w.astype(BF16) for w in (
            w_gates, s5_w_glu[layer], w_branch_a[layer], w_branch_b[layer], w_out[layer],
            ffn_w_gate[layer], ffn_w_up[layer], ffn_w_down[layer])]
        yb3 = yb.reshape(bsz, seq, ATT_WIDTH)
        if layer == depth - 1:
            h = _merge_ffn(h, y, yb3, g1, g2, gf, unperm, weights)
        else:
            h, u, q, kv = _merge_ffn(h, y, yb3, g1, g2, gf, unperm, weights,
                                     next_inproj=(gain1(layer + 1), w_uqkv(layer + 1), perm))
    return h
```

```python
import functools
import math

import jax
import jax.numpy as jnp
from jax import lax
from jax.experimental import pallas as pl
from jax.experimental.pallas import tpu as pltpu

F32 = jnp.float32
BF16 = jnp.bfloat16

D_MODEL = 1024
S5_WIDTH = 512
S5_GROUP = 16
S5_GROUPS = 32
S5_STATE = 64
HEAD_DIM = 64
N_Q_HEADS = 8
N_KV_HEADS = 2
Q_PER_KV = N_Q_HEADS // N_KV_HEADS
ATT_WIDTH = N_Q_HEADS * HEAD_DIM
KV_WIDTH = N_KV_HEADS * HEAD_DIM
WINDOW = 128
BLOCK = 128
NUM_BUCKETS = 32
MAX_DISTANCE = 128
D_FF = 2816
RMS_EPS = 1e-6
NEG_INF = -1e30
LOG2E = math.log2(math.e)

LANES = 128
SUBLANES = 8
CHUNK = 16
GROUPS_PER_SLAB = LANES // S5_GROUP
N_SLABS = S5_WIDTH // LANES
PAIRS_PER_SLAB = GROUPS_PER_SLAB // 2
GROUP_K = CHUNK * S5_GROUP
DIR_COLS = 2 * GROUPS_PER_SLAB * S5_STATE
TOK_TILE = 64
TILE_ROWS = SUBLANES * TOK_TILE
LAYOUT_TOK = 32
LAYOUT_ROWS = SUBLANES * LAYOUT_TOK
S5_ROWS = 16 * TILE_ROWS
FF_TILE = 256
ATT_QB = 16
VMEM_LIMIT = 56 * 1024 * 1024


def _rms(x, g):
    return x * lax.rsqrt(jnp.mean(x * x, axis=-1, keepdims=True) + RMS_EPS) * g


def _gelu_tanh(x):
    return 0.5 * x * (1.0 + jnp.tanh(math.sqrt(2.0 / math.pi) * (x + 0.044715 * (x * x * x))))


def _sigmoid(x):
    return 1.0 / (1.0 + jnp.exp(-x))


def _dot(a, b):
    return jnp.dot(a, b, preferred_element_type=F32)


def _const_spec(shape):
    nd = len(shape)
    return pl.BlockSpec(shape, lambda *_: (0,) * nd, pipeline_mode=pl.Buffered(1))


def _params(sem):
    return pltpu.CompilerParams(dimension_semantics=sem, vmem_limit_bytes=VMEM_LIMIT)


def _to_chunk_order():
    cpt = LAYOUT_TOK // CHUNK
    dst = jnp.arange(LAYOUT_ROWS)
    t, c, b = dst // (cpt * SUBLANES), (dst // SUBLANES) % cpt, dst % SUBLANES
    src = b * LAYOUT_TOK + c * CHUNK + t
    return (src[:, None] == jnp.arange(LAYOUT_ROWS)[None, :]).astype(BF16)


def _layout_tiles():
    return [(slice(s * LAYOUT_ROWS, (s + 1) * LAYOUT_ROWS), slice(s * LAYOUT_TOK, (s + 1) * LAYOUT_TOK))
            for s in range(TOK_TILE // LAYOUT_TOK)]


def _tile_rows(ref):
    return jnp.concatenate([ref[:, tok, :].reshape(LAYOUT_ROWS, ref.shape[2]) for _, tok in _layout_tiles()], axis=0)


def _inproj_rows(h, g_ref, w_ref, perm_ref, u_ref, q_ref, kv_ref):
    bsz = q_ref.shape[0]
    hn = _rms(h, g_ref[...]).astype(BF16)
    r = _dot(hn, w_ref[...])
    for rows, tok in _layout_tiles():
        u_ref[rows, :] = _dot(perm_ref[...], r[rows, :S5_WIDTH].astype(BF16)).astype(BF16)
        q_ref[:, tok, :] = r[rows, S5_WIDTH:S5_WIDTH + ATT_WIDTH].astype(BF16).reshape(bsz, LAYOUT_TOK, ATT_WIDTH)
        kv_ref[:, tok, :] = r[rows, S5_WIDTH + ATT_WIDTH:].astype(BF16).reshape(bsz, LAYOUT_TOK, 2 * KV_WIDTH)


def _inproj_kernel(x_ref, g_ref, w_ref, perm_ref, u_ref, q_ref, kv_ref):
    _inproj_rows(_tile_rows(x_ref), g_ref, w_ref, perm_ref, u_ref, q_ref, kv_ref)


def _tok_spec(bsz, width):
    return pl.BlockSpec((bsz, TOK_TILE, width), lambda i: (0, i, 0))


def _inproj_out(bsz, seq):
    specs = [pl.BlockSpec((TILE_ROWS, S5_WIDTH), lambda i: (i, 0)), _tok_spec(bsz, ATT_WIDTH),
             _tok_spec(bsz, 2 * KV_WIDTH)]
    shapes = [jax.ShapeDtypeStruct((bsz * seq, S5_WIDTH), BF16),
              jax.ShapeDtypeStruct((bsz, seq, ATT_WIDTH), BF16),
              jax.ShapeDtypeStruct((bsz, seq, 2 * KV_WIDTH), BF16)]
    return specs, shapes


def _inproj(h3, g, w, perm):
    bsz, seq, _ = h3.shape
    assert bsz * TOK_TILE == TILE_ROWS
    specs, shapes = _inproj_out(bsz, seq)
    return pl.pallas_call(
        _inproj_kernel,
        grid=(seq // TOK_TILE,),
        in_specs=[_tok_spec(bsz, D_MODEL), _const_spec((1, D_MODEL)), _const_spec(w.shape), _const_spec(perm.shape)],
        out_specs=specs,
        out_shape=shapes,
        compiler_params=_params(("parallel",)),
        name="inproj",
    )(h3, g, w, perm)


def _block_transpose(x):
    x = list(x)
    blk = lax.broadcasted_iota(jnp.int32, x[0].shape, 1) // S5_GROUP
    for d in (4, 2, 1):
        keep = (blk & d) == 0
        for i in range(GROUPS_PER_SLAB):
            if i & d:
                continue
            xi, xj = x[i], x[i + d]
            x[i] = jnp.where(keep, xi, pltpu.roll(xj, d * S5_GROUP, 1))
            x[i + d] = jnp.where(keep, pltpu.roll(xi, LANES - d * S5_GROUP, 1), xj)
    return x


def _token_rows(t, k):
    return pl.ds(k * LAYOUT_ROWS + t * (LAYOUT_ROWS // CHUNK), LAYOUT_ROWS // CHUNK)


def _group_rows(u_ref):
    tiles = u_ref.shape[0] // LAYOUT_ROWS
    tok = [jnp.concatenate([u_ref[_token_rows(t, k), :] for k in range(tiles)], axis=0) for t in range(CHUNK)]
    lo = _block_transpose(tok[:CHUNK // 2])
    hi = _block_transpose(tok[CHUNK // 2:])
    return [jnp.concatenate([lo[a], hi[a]], axis=1) for a in range(GROUPS_PER_SLAB)]


def _state_tile(ri, pq):
    return pl.ds(ri * (DIR_COLS // 2) + pq * LANES, LANES)


def _s5_in_kernel(u_ref, w_ref, of_ref, ob_ref, g_ref):
    g = _group_rows(u_ref)
    for a in range(GROUPS_PER_SLAB):
        g_ref[:, a * GROUP_K:(a + 1) * GROUP_K] = g[a]
    for pq in range(PAIRS_PER_SLAB):
        r = _dot(jnp.concatenate([g[2 * pq], g[2 * pq + 1]], axis=1), w_ref[pq])
        for dr, o_ref in enumerate((of_ref, ob_ref)):
            for ri in range(2):
                k = dr * 2 + ri
                o_ref[:, _state_tile(ri, pq)] = r[:, k * LANES:(k + 1) * LANES].astype(BF16)


def _s5_state_in(u, w_in_state, layer, rows=S5_ROWS):
    t = u.shape[0]
    state = jax.ShapeDtypeStruct((t // CHUNK, N_SLABS * DIR_COLS), BF16)
    spec = pl.BlockSpec((rows // CHUNK, DIR_COLS), lambda s, i: (i, s))
    return pl.pallas_call(
        _s5_in_kernel,
        grid=(N_SLABS, t // rows),
        in_specs=[pl.BlockSpec((rows, LANES), lambda s, i: (i, s)),
                  pl.BlockSpec((None,) + w_in_state.shape[1:], lambda s, i: (layer * N_SLABS + s, 0, 0, 0))],
        out_specs=[spec, spec, pl.BlockSpec((rows // CHUNK, CHUNK * LANES), lambda s, i: (i, s))],
        out_shape=[state, state, jax.ShapeDtypeStruct((t // CHUNK, CHUNK * S5_WIDTH), BF16)],
        compiler_params=_params(("parallel", "parallel")),
        name="s5_state_in",
    )(u, w_in_state)


def _s5_scan_kernel(xf_ref, xb_ref, af_ref, ab_ref, of_ref, ob_ref, st_ref, *, cpt):
    half = DIR_COLS // 2
    pair_rows = 2 * SUBLANES
    units = xf_ref.shape[1] // DIR_COLS
    lower, upper = slice(0, SUBLANES), slice(SUBLANES, pair_rows)

    @pl.when(pl.program_id(1) == 0)
    def _():
        st_ref[...] = jnp.zeros_like(st_ref)

    def two_chunks(x_ref, a_ref, o_ref, c2, state, reverse):
        rows = pl.ds(pl.multiple_of(c2 * pair_rows, pair_rows), pair_rows)
        first, second = (upper, lower) if reverse else (lower, upper)
        new_state = []
        for un in range(units):
            re = slice(un * DIR_COLS, un * DIR_COLS + half)
            im = slice(un * DIR_COLS + half, (un + 1) * DIR_COLS)
            ar, ai = a_ref[:, re], a_ref[:, im]
            xr = x_ref[rows, re].astype(F32)
            xi = x_ref[rows, im].astype(F32)
            s0r, s0i = state[un]
            s1r, s1i = ar * s0r - ai * s0i + xr[first], ar * s0i + ai * s0r + xi[first]
            s2r, s2i = ar * s1r - ai * s1i + xr[second], ar * s1i + ai * s1r + xi[second]
            enter_r, enter_i = ((s1r, s0r), (s1i, s0i)) if reverse else ((s0r, s1r), (s0i, s1i))
            o_ref[rows, re] = jnp.concatenate(enter_r, axis=0).astype(BF16)
            o_ref[rows, im] = jnp.concatenate(enter_i, axis=0).astype(BF16)
            new_state.append((s2r, s2i))
        return new_state

    def body(i, state):
        fwd, bwd = state
        return (two_chunks(xf_ref, af_ref, of_ref, i, fwd, False),
                two_chunks(xb_ref, ab_ref, ob_ref, cpt // 2 - 1 - i, bwd, True))

    def load(dr):
        return [(st_ref[dr, :, pl.ds(un * DIR_COLS, half)], st_ref[dr, :, pl.ds(un * DIR_COLS + half, half)])
                for un in range(units)]

    fwd, bwd = lax.fori_loop(0, cpt // 2, body, (load(0), load(1)))
    for dr, state in enumerate((fwd, bwd)):
        for un, (sr, si) in enumerate(state):
            st_ref[dr, :, pl.ds(un * DIR_COLS, half)] = sr
            st_ref[dr, :, pl.ds(un * DIR_COLS + half, half)] = si


def _s5_scan(sin_f, sin_b, a16, n_chunks, cpt=64, width=2 * DIR_COLS):
    nct = n_chunks // cpt
    rows = cpt * SUBLANES
    up = pl.BlockSpec((rows, width), lambda c, k: (k, c))
    down = pl.BlockSpec((rows, width), lambda c, k: (nct - 1 - k, c))
    coef = lambda dr: pl.BlockSpec((None, SUBLANES, width), lambda c, k: (dr, 0, c))
    return pl.pallas_call(
        functools.partial(_s5_scan_kernel, cpt=cpt),
        grid=(sin_f.shape[1] // width, nct),
        in_specs=[up, down, coef(0), coef(1)],
        out_specs=[up, down],
        out_shape=[jax.ShapeDtypeStruct(sin_f.shape, BF16), jax.ShapeDtypeStruct(sin_b.shape, BF16)],
        scratch_shapes=[pltpu.VMEM((2, SUBLANES, width), F32)],
        compiler_params=_params(("parallel", "arbitrary")),
        name="s5_scan",
    )(sin_f, sin_b, a16, a16)


def _s5_out_kernel(g_ref, cf_ref, cb_ref, wt_ref, wo_ref, y_ref):
    g = [g_ref[:, a * GROUP_K:(a + 1) * GROUP_K] for a in range(GROUPS_PER_SLAB)]
    ys = []
    for pq in range(PAIRS_PER_SLAB):
        carry = jnp.concatenate([c_ref[:, _state_tile(ri, pq)] for c_ref in (cf_ref, cb_ref) for ri in range(2)],
                                axis=1)
        from_state = _dot(carry, wo_ref[pq])
        for a2 in range(2):
            a = 2 * pq + a2
            ys.append((_dot(g[a], wt_ref[a]) + from_state[:, a2 * GROUP_K:(a2 + 1) * GROUP_K]).astype(BF16))
    halves = (_block_transpose([y[:, :LANES] for y in ys]), _block_transpose([y[:, LANES:] for y in ys]))
    per_tile = LAYOUT_ROWS // CHUNK
    for t in range(CHUNK):
        tok = halves[t // (CHUNK // 2)][t % (CHUNK // 2)]
        for k in range(y_ref.shape[0] // LAYOUT_ROWS):
            y_ref[_token_rows(t, k), :] = tok[k * per_tile:(k + 1) * per_tile, :]


def _s5_out(grouped, carry_f, carry_b, w_toep, w_out_state, layer, rows=S5_ROWS):
    t = grouped.shape[0] * CHUNK
    state = pl.BlockSpec((rows // CHUNK, DIR_COLS), lambda s, i: (i, s))
    return pl.pallas_call(
        _s5_out_kernel,
        grid=(N_SLABS, t // rows),
        in_specs=[pl.BlockSpec((rows // CHUNK, CHUNK * LANES), lambda s, i: (i, s)), state, state,
                  pl.BlockSpec((None,) + w_toep.shape[1:], lambda s, i: (layer * N_SLABS + s, 0, 0, 0)),
                  pl.BlockSpec((None,) + w_out_state.shape[1:], lambda s, i: (layer * N_SLABS + s, 0, 0, 0))],
        out_specs=pl.BlockSpec((rows, LANES), lambda s, i: (i, s)),
        out_shape=jax.ShapeDtypeStruct((t, S5_WIDTH), BF16),
        compiler_params=_params(("parallel", "parallel")),
        name="s5_out",
    )(grouped, carry_f, carry_b, w_toep, w_out_state)


def _s5_tables(lam_re, lam_im, log_dt, b_re, b_im, c_re, c_im, d):
    G, P, H, C, S = S5_GROUPS, S5_STATE, S5_GROUP, CHUNK, GROUPS_PER_SLAB
    lr = lam_re.astype(F32)
    li = lam_im.astype(F32)
    dt = jnp.exp(log_dt.astype(F32))[..., None]
    mag = jnp.exp(lr * dt)
    ab_re = mag * jnp.cos(li * dt)
    ab_im = mag * jnp.sin(li * dt)
    nr = ab_re - 1.0
    den = lr * lr + li * li
    coef_re = (nr * lr + ab_im * li) / den
    coef_im = (ab_im * lr - nr * li) / den
    br = b_re.astype(F32)
    bi = b_im.astype(F32)
    bb_re = coef_re[..., None] * br - coef_im[..., None] * bi
    bb_im = coef_re[..., None] * bi + coef_im[..., None] * br
    cr = c_re.astype(F32)
    ci = c_im.astype(F32)

    k = jnp.arange(C + 1, dtype=F32)[:, None, None, None]
    pmag = jnp.exp(k * (lr * dt))
    pw_re = pmag * jnp.cos(k * (li * dt))
    pw_im = pmag * jnp.sin(k * (li * dt))

    n_pairs = G // 2
    pw = jnp.stack([pw_re, pw_im], axis=0)
    up = jnp.arange(C)

    def table(k_fwd, k_bwd):
        return jnp.stack([pw[:, k_fwd, 0], pw[:, k_bwd, 1]], axis=0).transpose(3, 0, 1, 2, 4)

    by_state = lambda tab: tab.transpose(0, 1, 2, 4, 3).reshape(n_pairs, 2, 2, 2, P, C)
    lag_pw = by_state(table(up, C - 1 - up))
    out_pw = by_state(table(up + 1, C - up))
    in_pw = table(C - 1 - up, up).reshape(n_pairs, 2, 2, 2, C, P)
    in_pw = jnp.tile(in_pw, (1, 1, 1, 1, 1, 2))
    ct = jnp.stack([cr, ci], axis=0).transpose(2, 1, 0, 4, 3).reshape(n_pairs, 2, 2, 2, P, H)
    bbt = jnp.stack([bb_re, bb_im], axis=0).transpose(2, 1, 0, 4, 3).reshape(n_pairs, 2, 2, 2, H, P)
    own = jnp.eye(2, dtype=F32)[None, :, None, None, None, :, None]
    bb_own = (bbt[:, :, :, :, :, None, :] * own).reshape(n_pairs, 2, 2, 2, H, 2 * P)
    skip = jnp.eye(H, dtype=F32)[None] * d.astype(F32).reshape(G, 1, H)
    skip = jnp.pad(skip, ((0, 0), (0, 0), ((C - 1) * H, 0))).reshape(n_pairs, 2, H, GROUP_K)

    a16 = jnp.stack([pw_re[C], pw_im[C]], axis=0)
    a16 = a16.reshape(2, 2, N_SLABS, S, P).transpose(1, 2, 0, 3, 4)
    a16 = jnp.broadcast_to(a16.reshape(2, 1, N_SLABS * DIR_COLS), (2, SUBLANES, N_SLABS * DIR_COLS))
    return (ct, bbt, bb_own, lag_pw, out_pw, in_pw, skip), a16


def _s5_weights(*stacked_params):
    depth = stacked_params[0].shape[0]
    tabs, a16 = jax.vmap(_s5_tables)(*stacked_params)
    tabs = [a.reshape((depth * a.shape[1],) + a.shape[2:]) for a in tabs]
    rep_h = jnp.tile(jnp.eye(S5_GROUP, dtype=BF16), (1, CHUNK))
    rep_k = jnp.repeat(jnp.eye(CHUNK, dtype=BF16), S5_GROUP, axis=1)
    wt, wi, wo = _s5_pack(*tabs, rep_h, rep_k)
    wt = wt.reshape(depth * N_SLABS, GROUPS_PER_SLAB, GROUP_K, GROUP_K)
    wi = wi.reshape(depth * N_SLABS, PAIRS_PER_SLAB, 2 * GROUP_K, 4 * LANES)
    wo = wo.reshape(depth * N_SLABS, PAIRS_PER_SLAB, 4 * LANES, 2 * GROUP_K)
    return wi, wt, wo, a16


def _s5_pack_kernel(ct_ref, bbt_ref, bbo_ref, lagp_ref, outp_ref, inp_ref, skip_ref, reph_ref, repk_ref,
                    wt_ref, wi_ref, wo_ref):
    def split(x):
        hi = x.astype(BF16)
        return hi, (x - hi.astype(F32)).astype(BF16)

    def exact_dot(a, b):
        a_hi, a_lo = split(a)
        if b.dtype == BF16:
            return _dot(a_hi, b) + _dot(a_lo, b)
        b_hi, b_lo = split(b)
        return _dot(a_hi, b_hi) + (_dot(a_hi, b_lo) + _dot(a_lo, b_hi))

    rep_h = reph_ref[...]
    rep_k = repk_ref[...]
    wide = 2 * GROUP_K
    for a2 in range(2):
        lag_part = []
        for dr in range(2):
            c_re = exact_dot(ct_ref[a2, dr, 0], rep_h)
            c_im = exact_dot(ct_ref[a2, dr, 1], rep_h)

            def times_power(ref):
                p_re = exact_dot(ref[a2, dr, 0], rep_k)
                p_im = exact_dot(ref[a2, dr, 1], rep_k)
                return c_re * p_re - c_im * p_im, c_re * p_im + c_im * p_re

            l_re, l_im = times_power(lagp_ref)
            lag_part.append(exact_dot(bbt_ref[a2, dr, 0], l_re) - exact_dot(bbt_ref[a2, dr, 1], l_im))
            w_re, w_im = times_power(outp_ref)
            for ri, val in ((0, w_re), (1, -w_im)):
                rows = pl.ds((dr * 2 + ri) * LANES + a2 * S5_STATE, S5_STATE)
                wo_ref[rows, a2 * GROUP_K:(a2 + 1) * GROUP_K] = val.astype(BF16)
                wo_ref[rows, (1 - a2) * GROUP_K:(2 - a2) * GROUP_K] = jnp.zeros((S5_STATE, GROUP_K), BF16)
            b_re = bbo_ref[a2, dr, 0]
            b_im = bbo_ref[a2, dr, 1]
            for t in range(CHUNK):
                e_re = inp_ref[a2, dr, 0, t:t + 1, :]
                e_im = inp_ref[a2, dr, 1, t:t + 1, :]
                rows = pl.ds(a2 * GROUP_K + t * S5_GROUP, S5_GROUP)
                wi_ref[rows, (2 * dr) * LANES:(2 * dr + 1) * LANES] = (b_re * e_re - b_im * e_im).astype(BF16)
                wi_ref[rows, (2 * dr + 1) * LANES:(2 * dr + 2) * LANES] = (b_re * e_im + b_im * e_re).astype(BF16)
        zero = jnp.zeros((S5_GROUP, GROUP_K), F32)
        lags = (jnp.concatenate([lag_part[1] + skip_ref[a2], zero], axis=1)
                + pltpu.roll(jnp.concatenate([lag_part[0], zero], axis=1), (CHUNK - 1) * S5_GROUP, 1))
        for t in range(CHUNK):
            shift = (CHUNK - 1 - t) * S5_GROUP
            window = lags if shift == 0 else pltpu.roll(lags, wide - shift, 1)
            wt_ref[a2, t * S5_GROUP:(t + 1) * S5_GROUP, :] = window[:, :GROUP_K].astype(BF16)


def _s5_pack(ct, bbt, bb_own, lag_pw, out_pw, in_pw, skip, rep_h, rep_k):
    n_pairs = ct.shape[0]
    per_pair = lambda a: pl.BlockSpec((None,) + a.shape[1:], lambda i: (i,) + (0,) * (a.ndim - 1))
    tabs = (ct, bbt, bb_own, lag_pw, out_pw, in_pw, skip)
    return pl.pallas_call(
        _s5_pack_kernel,
        grid=(n_pairs,),
        in_specs=[per_pair(a) for a in tabs] + [_const_spec(rep_h.shape), _const_spec(rep_k.shape)],
        out_specs=[pl.BlockSpec((2, GROUP_K, GROUP_K), lambda i: (i, 0, 0)),
                   pl.BlockSpec((None, 2 * GROUP_K, 4 * LANES), lambda i: (i, 0, 0)),
                   pl.BlockSpec((None, 4 * LANES, 2 * GROUP_K), lambda i: (i, 0, 0))],
        out_shape=[jax.ShapeDtypeStruct((2 * n_pairs, GROUP_K, GROUP_K), BF16),
                   jax.ShapeDtypeStruct((n_pairs, 2 * GROUP_K, 4 * LANES), BF16),
                   jax.ShapeDtypeStruct((n_pairs, 4 * LANES, 2 * GROUP_K), BF16)],
        compiler_params=_params(("parallel",)),
        name="s5_pack",
    )(*tabs, rep_h, rep_k)


def _attn_kernel(sink_ref, q_ref, kl_ref, kc_ref, kr_ref, bias_first_ref, bias_mid_ref, bias_last_ref, o_ref):
    nk = 3 * BLOCK
    kv_all = jnp.concatenate([kl_ref[...], kc_ref[...], kr_ref[...]], axis=0)
    low = lax.broadcasted_iota(jnp.int32, (kv_all.shape[0], LANES), 1) < HEAD_DIM
    zero = jnp.zeros((kv_all.shape[0], LANES), BF16)

    def diag_parts(tile, kvh):
        other = pltpu.roll(tile, HEAD_DIM, 1)
        first, second = (tile, other) if kvh == 0 else (other, tile)
        return jnp.where(low, first, zero), jnp.where(low, zero, second)

    def window(parts, qb):
        return jnp.concatenate([p[qb * BLOCK:qb * BLOCK + nk] for p in parts], axis=0)

    top = lax.broadcasted_iota(jnp.int32, (2 * nk, LANES), 0) < nk
    ones_bd = (top == (lax.broadcasted_iota(jnp.int32, (2 * nk, LANES), 1) < HEAD_DIM)).astype(BF16)

    upper = lax.broadcasted_iota(jnp.int32, (2 * BLOCK, 1), 0) < BLOCK
    bias_refs = [bias_first_ref] + [bias_mid_ref] * (ATT_QB - 2) + [bias_last_ref]
    for kvh in range(N_KV_HEADS):
        k_parts = diag_parts(kv_all[:, :LANES], kvh)
        v_parts = diag_parts(kv_all[:, LANES:], kvh)
        for qb, bias_ref in enumerate(bias_refs):
            rows = slice(qb * BLOCK, (qb + 1) * BLOCK)
            kbd = window(k_parts, qb)
            vbd = jnp.concatenate([window(v_parts, qb), ones_bd], axis=1)
            tiles = [(kvh * 2 + pair) * LANES for pair in range(2)]
            q2 = jnp.concatenate([q_ref[rows, tl:tl + LANES] for tl in tiles], axis=0)
            s = lax.dot_general(q2, kbd, (((1,), (1,)), ((), ())), preferred_element_type=F32)
            s = s + bias_ref[kvh]
            ps, sinks = [], []
            for e in range(2):
                se = s[:, e * nk:(e + 1) * nk]
                sk = jnp.where(upper, sink_ref[4 * kvh + e], sink_ref[4 * kvh + 2 + e])
                mx = jnp.maximum(jnp.max(se, axis=-1, keepdims=True), sk)
                ps.append(jnp.exp2((se - mx).astype(BF16)))
                sinks.append(jnp.broadcast_to(jnp.exp2(sk - mx), (2 * BLOCK, HEAD_DIM)))
            nd = _dot(jnp.concatenate(ps, axis=1), vbd)
            o = (nd[:, :LANES] / (nd[:, LANES:] + jnp.concatenate(sinks, axis=1))).astype(BF16)
            for pair, tl in enumerate(tiles):
                o_ref[rows, tl:tl + LANES] = o[pair * BLOCK:(pair + 1) * BLOCK, :]


def _attention(q, kv, sink, bias, bsz, seq):
    nb = seq // BLOCK
    assert nb % ATT_QB == 0 and ATT_QB >= 2
    nj = nb // ATT_QB
    rows = ATT_QB * BLOCK
    centre = lambda b, j: (b * nj + j, 0)
    left = lambda b, j: (b * nb + jnp.maximum(ATT_QB * j - 1, 0), 0)
    right = lambda b, j: (b * nb + jnp.minimum(ATT_QB * (j + 1), nb - 1), 0)
    variant = (None,) + bias.shape[1:]
    bias_first = pl.BlockSpec(variant, lambda b, j: (jnp.where(j == 0, 0, 1), 0, 0, 0))
    bias_mid = pl.BlockSpec(variant, lambda b, j: (1, 0, 0, 0), pipeline_mode=pl.Buffered(1))
    bias_last = pl.BlockSpec(variant, lambda b, j: (jnp.where(j == nj - 1, 2, 1), 0, 0, 0))
    return pl.pallas_call(
        _attn_kernel,
        grid=(bsz, nj),
        in_specs=[pl.BlockSpec(memory_space=pltpu.SMEM),
                  pl.BlockSpec((rows, ATT_WIDTH), centre),
                  pl.BlockSpec((BLOCK, 2 * KV_WIDTH), left),
                  pl.BlockSpec((rows, 2 * KV_WIDTH), centre),
                  pl.BlockSpec((BLOCK, 2 * KV_WIDTH), right),
                  bias_first, bias_mid, bias_last],
        out_specs=pl.BlockSpec((rows, ATT_WIDTH), centre),
        out_shape=jax.ShapeDtypeStruct((bsz * seq, ATT_WIDTH), BF16),
        compiler_params=_params(("parallel", "parallel")),
        name="attention",
    )(sink, q, kv, kv, kv, bias, bias, bias)


def _t5_bucket(rel):
    half = NUM_BUCKETS // 2
    max_exact = half // 2
    ret = jnp.where(rel > 0, half, 0)
    n = jnp.abs(rel)
    nf = jnp.maximum(n, 1).astype(jnp.float32)
    large = max_exact + (jnp.log(nf / max_exact) / math.log(MAX_DISTANCE / max_exact)
                         * (half - max_exact)).astype(jnp.int32)
    large = jnp.minimum(large, half - 1)
    return ret + jnp.where(n < max_exact, n, large)


def _band_bias(rel_bias):
    q_loc = jnp.arange(BLOCK, dtype=jnp.int32)
    k_loc = jnp.arange(3 * BLOCK, dtype=jnp.int32)
    rel = (k_loc[None, :] - BLOCK) - q_loc[:, None]
    onehot = (_t5_bucket(rel)[None] == jnp.arange(NUM_BUCKETS, dtype=jnp.int32)[:, None, None]).astype(F32)
    bias = jnp.einsum('bh,bqk->hqk', rel_bias.astype(F32), onehot, precision=lax.Precision.HIGHEST)
    bias = jnp.where((jnp.abs(rel) <= WINDOW)[None], bias * LOG2E, NEG_INF)
    keep = jnp.stack([k_loc >= BLOCK, k_loc >= 0, k_loc < 2 * BLOCK], axis=0)
    bias = jnp.where(keep[:, None, None, :], bias[None], NEG_INF)
    bias = bias.reshape(3, N_Q_HEADS // 2, 2, BLOCK, 3 * BLOCK).transpose(0, 1, 3, 2, 4)
    return bias.reshape(3, N_KV_HEADS, 2 * BLOCK, 6 * BLOCK)


def _merge_ffn_body(h_ref, y_ref, yb_ref, g1_ref, g2_ref, gf_ref, unperm_ref, wg_ref, wglu_ref, wa_ref, wb_ref,
                    wo_ref, fg_ref, fu_ref, fd_ref, o_ref, acc_ref, *, nxt, final_norm):
    bsz = h_ref.shape[0]
    h = _tile_rows(h_ref)
    hn = _rms(h, g1_ref[...]).astype(BF16)
    gates = _sigmoid(_dot(hn, wg_ref[...]))
    y = jnp.concatenate([_dot(unperm_ref[...], y_ref[rows, :]) for rows, _ in _layout_tiles()], axis=0)
    z = _gelu_tanh(y)
    za = (z * _sigmoid(_dot(z.astype(BF16), wglu_ref[...]))).astype(BF16)
    yb = _tile_rows(yb_ref)
    merged = gates[:, :D_MODEL] * _dot(za, wa_ref[...]) + gates[:, D_MODEL:] * _dot(yb, wb_ref[...])
    h = h + _dot(merged.astype(BF16), wo_ref[...])

    hn = _rms(h, g2_ref[...]).astype(BF16)
    acc_ref[...] = h
    for j in range(D_FF // FF_TILE):
        cols = slice(j * FF_TILE, (j + 1) * FF_TILE)
        gate = _dot(hn, fg_ref[:, cols])
        up = _dot(hn, fu_ref[:, cols])
        act = (gate * _sigmoid(gate) * up).astype(BF16)
        acc_ref[...] += _dot(act, fd_ref[cols, :])
    out = acc_ref[...]
    if final_norm:
        out = _rms(out, gf_ref[...])
    else:
        gn_ref, wn_ref, perm_ref, u_ref, q_ref, kv_ref = nxt
        _inproj_rows(out, gn_ref, wn_ref, perm_ref, u_ref, q_ref, kv_ref)
    for rows, tok in _layout_tiles():
        o_ref[:, tok, :] = out[rows, :].reshape(bsz, LAYOUT_TOK, D_MODEL)


def _merge_ffn_last_kernel(*refs):
    *ins, o_ref, acc_ref = refs
    _merge_ffn_body(*ins, o_ref, acc_ref, nxt=None, final_norm=True)


def _merge_ffn_next_kernel(*refs):
    *ins, gn_ref, wn_ref, perm_ref, o_ref, u_ref, q_ref, kv_ref, acc_ref = refs
    _merge_ffn_body(*ins, o_ref, acc_ref, nxt=(gn_ref, wn_ref, perm_ref, u_ref, q_ref, kv_ref), final_norm=False)


def _merge_ffn(h3, y, yb3, g1, g2, gf, unperm, weights, next_inproj=None):
    bsz, seq, _ = h3.shape
    vec = _const_spec((1, D_MODEL))
    in_specs = [_tok_spec(bsz, D_MODEL), pl.BlockSpec((TILE_ROWS, S5_WIDTH), lambda i: (i, 0)),
                _tok_spec(bsz, ATT_WIDTH), vec, vec, vec, _const_spec(unperm.shape)]
    in_specs += [_const_spec(w.shape) for w in weights]
    args = [h3, y, yb3, g1, g2, gf, unperm, *weights]
    out_specs, out_shape = [_tok_spec(bsz, D_MODEL)], [jax.ShapeDtypeStruct((bsz, seq, D_MODEL), F32)]
    if next_inproj is None:
        body = _merge_ffn_last_kernel
    else:
        body = _merge_ffn_next_kernel
        in_specs += [_const_spec(a.shape) for a in next_inproj]
        args += list(next_inproj)
        specs, shapes = _inproj_out(bsz, seq)
        out_specs += specs
        out_shape += shapes
    outs = pl.pallas_call(
        body,
        grid=(seq // TOK_TILE,),
        in_specs=in_specs,
        out_specs=out_specs,
        out_shape=out_shape,
        scratch_shapes=[pltpu.VMEM((TILE_ROWS, D_MODEL), F32)],
        compiler_params=_params(("parallel",)),
        name="merge_ffn",
    )(*args)
    return outs[0] if next_inproj is None else outs


def kernel(x, norm1_g, norm2_g, final_g, w_in, s5_lambda_re, s5_lambda_im, s5_log_dt, s5_b_re, s5_b_im,
           s5_c_re, s5_c_im, s5_d, s5_w_glu, attn_sink, rel_bias, w_branch_a, w_branch_b, w_out,
           ffn_w_gate, ffn_w_up, ffn_w_down):
    bsz, seq, _ = x.shape
    depth = w_in.shape[0]
    assert seq % BLOCK == 0 and seq % CHUNK == 0 and bsz == SUBLANES
    n_chunks = seq // CHUNK
    t = bsz * seq
    bias = _band_bias(rel_bias)
    o_k = S5_WIDTH + ATT_WIDTH
    o_g = o_k + 2 * KV_WIDTH
    col_scale = jnp.concatenate([jnp.ones((S5_WIDTH,), F32), jnp.full((ATT_WIDTH,), LOG2E * HEAD_DIM ** -0.5, F32),
                                 jnp.ones((2 * KV_WIDTH,), F32)])
    gf = final_g.reshape(1, D_MODEL).astype(F32)
    perm = _to_chunk_order()
    unperm = perm.T
    w_in_state, w_toep, w_out_state, a16 = _s5_weights(
        s5_lambda_re, s5_lambda_im, s5_log_dt, s5_b_re, s5_b_im, s5_c_re, s5_c_im, s5_d)
    gain1 = lambda layer: norm1_g[layer].reshape(1, D_MODEL).astype(F32)
    w_uqkv = lambda layer: (w_in[layer][:, :o_g] * col_scale).astype(BF16)
    h = x
    u, q, kv = _inproj(h, gain1(0), w_uqkv(0), perm)
    for layer in range(depth):
        g1 = gain1(layer)
        g2 = norm2_g[layer].reshape(1, D_MODEL).astype(F32)
        w_gates = w_in[layer][:, o_g:].astype(BF16)
        sin_f, sin_b, grouped = _s5_state_in(u, w_in_state, layer)
        carry_f, carry_b = _s5_scan(sin_f, sin_b, a16[layer], n_chunks)
        y = _s5_out(grouped, carry_f, carry_b, w_toep, w_out_state, layer)
        yb = _attention(q.reshape(t, ATT_WIDTH), kv.reshape(t, 2 * KV_WIDTH),
                        attn_sink[layer].astype(F32) * LOG2E, bias, bsz, seq)
        weights = [w.astype(BF16) for w in (
            w_gates, s5_w_glu[layer], w_branch_a[layer], w_branch_b[layer], w_out[layer],
            ffn_w_gate[layer], ffn_w_up[layer], ffn_w_down[layer])]
        yb3 = yb.reshape(bsz, seq, ATT_WIDTH)
        if layer == depth - 1:
            h = _merge_ffn(h, y, yb3, g1, g2, gf, unperm, weights)
        else:
            h, u, q, kv = _merge_ffn(h, y, yb3, g1, g2, gf, unperm, weights,
                                     next_inproj=(gain1(layer + 1), w_uqkv(layer + 1), perm))
    return h
```

```python
import functools
import math

import jax
import jax.numpy as jnp
from jax import lax
from jax.experimental import pallas as pl
from jax.experimental.pallas import tpu as pltpu

F32 = jnp.float32
BF16 = jnp.bfloat16

D_MODEL = 1024
S5_WIDTH = 512
S5_GROUP = 16
S5_GROUPS = 32
S5_STATE = 64
HEAD_DIM = 64
N_Q_HEADS = 8
N_KV_HEADS = 2
Q_PER_KV = N_Q_HEADS // N_KV_HEADS
ATT_WIDTH = N_Q_HEADS * HEAD_DIM
KV_WIDTH = N_KV_HEADS * HEAD_DIM
WINDOW = 128
BLOCK = 128
NUM_BUCKETS = 32
MAX_DISTANCE = 128
D_FF = 2816
RMS_EPS = 1e-6
NEG_INF = -1e30
LOG2E = math.log2(math.e)

LANES = 128
SUBLANES = 8
CHUNK = 16
GROUPS_PER_SLAB = LANES // S5_GROUP
N_SLABS = S5_WIDTH // LANES
PAIRS_PER_SLAB = GROUPS_PER_SLAB // 2
GROUP_K = CHUNK * S5_GROUP
DIR_COLS = 2 * GROUPS_PER_SLAB * S5_STATE
TOK_TILE = 64
TILE_ROWS = SUBLANES * TOK_TILE
LAYOUT_TOK = 32
LAYOUT_ROWS = SUBLANES * LAYOUT_TOK
S5_ROWS = 16 * TILE_ROWS
FF_TILE = 256
ATT_QB = 16
VMEM_LIMIT = 56 * 1024 * 1024


def _rms(x, g):
    return x * lax.rsqrt(jnp.mean(x * x, axis=-1, keepdims=True) + RMS_EPS) * g


def _gelu_tanh(x):
    return 0.5 * x * (1.0 + jnp.tanh(math.sqrt(2.0 / math.pi) * (x + 0.044715 * (x * x * x))))


def _sigmoid(x):
    return 1.0 / (1.0 + jnp.exp(-x))


def _dot(a, b):
    return jnp.dot(a, b, preferred_element_type=F32)


def _const_spec(shape):
    nd = len(shape)
    return pl.BlockSpec(shape, lambda *_: (0,) * nd, pipeline_mode=pl.Buffered(1))


def _layer_spec(stacked, layer):
    tail = (0,) * (stacked.ndim - 1)
    return pl.BlockSpec((None,) + stacked.shape[1:], lambda *_: (layer,) + tail, pipeline_mode=pl.Buffered(1))


def _params(sem):
    return pltpu.CompilerParams(dimension_semantics=sem, vmem_limit_bytes=VMEM_LIMIT)


def _to_chunk_order():
    cpt = LAYOUT_TOK // CHUNK
    dst = jnp.arange(LAYOUT_ROWS)
    t, c, b = dst // (cpt * SUBLANES), (dst // SUBLANES) % cpt, dst % SUBLANES
    src = b * LAYOUT_TOK + c * CHUNK + t
    return (src[:, None] == jnp.arange(LAYOUT_ROWS)[None, :]).astype(BF16)


def _layout_tiles():
    return [(slice(s * LAYOUT_ROWS, (s + 1) * LAYOUT_ROWS), slice(s * LAYOUT_TOK, (s + 1) * LAYOUT_TOK))
            for s in range(TOK_TILE // LAYOUT_TOK)]


def _tile_rows(ref):
    return jnp.concatenate([ref[:, tok, :].reshape(LAYOUT_ROWS, ref.shape[2]) for _, tok in _layout_tiles()], axis=0)


def _inproj_rows(h, g_ref, w_ref, perm_ref, u_ref, q_ref, kv_ref):
    bsz = q_ref.shape[0]
    hn = _rms(h, g_ref[...]).astype(BF16)
    r = _dot(hn, w_ref[...])
    for rows, tok in _layout_tiles():
        u_ref[rows, :] = _dot(perm_ref[...], r[rows, :S5_WIDTH].astype(BF16)).astype(BF16)
        q_ref[:, tok, :] = r[rows, S5_WIDTH:S5_WIDTH + ATT_WIDTH].astype(BF16).reshape(bsz, LAYOUT_TOK, ATT_WIDTH)
        kv_ref[:, tok, :] = r[rows, S5_WIDTH + ATT_WIDTH:].astype(BF16).reshape(bsz, LAYOUT_TOK, 2 * KV_WIDTH)


def _inproj_kernel(x_ref, g_ref, w_ref, perm_ref, u_ref, q_ref, kv_ref):
    _inproj_rows(_tile_rows(x_ref), g_ref, w_ref, perm_ref, u_ref, q_ref, kv_ref)


def _tok_spec(bsz, width):
    return pl.BlockSpec((bsz, TOK_TILE, width), lambda i: (0, i, 0))


def _inproj_out(bsz, seq):
    specs = [pl.BlockSpec((TILE_ROWS, S5_WIDTH), lambda i: (i, 0)), _tok_spec(bsz, ATT_WIDTH),
             _tok_spec(bsz, 2 * KV_WIDTH)]
    shapes = [jax.ShapeDtypeStruct((bsz * seq, S5_WIDTH), BF16),
              jax.ShapeDtypeStruct((bsz, seq, ATT_WIDTH), BF16),
              jax.ShapeDtypeStruct((bsz, seq, 2 * KV_WIDTH), BF16)]
    return specs, shapes


def _inproj(h3, g, w_stacked, perm, layer):
    bsz, seq, _ = h3.shape
    assert bsz * TOK_TILE == TILE_ROWS
    specs, shapes = _inproj_out(bsz, seq)
    return pl.pallas_call(
        _inproj_kernel,
        grid=(seq // TOK_TILE,),
        in_specs=[_tok_spec(bsz, D_MODEL), _const_spec((1, D_MODEL)), _layer_spec(w_stacked, layer),
                  _const_spec(perm.shape)],
        out_specs=specs,
        out_shape=shapes,
        compiler_params=_params(("parallel",)),
        name="inproj",
    )(h3, g, w_stacked, perm)


def _block_transpose(x):
    x = list(x)
    blk = lax.broadcasted_iota(jnp.int32, x[0].shape, 1) // S5_GROUP
    for d in (4, 2, 1):
        keep = (blk & d) == 0
        for i in range(GROUPS_PER_SLAB):
            if i & d:
                continue
            xi, xj = x[i], x[i + d]
            x[i] = jnp.where(keep, xi, pltpu.roll(xj, d * S5_GROUP, 1))
            x[i + d] = jnp.where(keep, pltpu.roll(xi, LANES - d * S5_GROUP, 1), xj)
    return x


def _token_rows(t, k):
    return pl.ds(k * LAYOUT_ROWS + t * (LAYOUT_ROWS // CHUNK), LAYOUT_ROWS // CHUNK)


def _group_rows(u_ref):
    tiles = u_ref.shape[0] // LAYOUT_ROWS
    tok = [jnp.concatenate([u_ref[_token_rows(t, k), :] for k in range(tiles)], axis=0) for t in range(CHUNK)]
    lo = _block_transpose(tok[:CHUNK // 2])
    hi = _block_transpose(tok[CHUNK // 2:])
    return [jnp.concatenate([lo[a], hi[a]], axis=1) for a in range(GROUPS_PER_SLAB)]


def _state_tile(ri, pq):
    return pl.ds(ri * (DIR_COLS // 2) + pq * LANES, LANES)


def _s5_in_kernel(u_ref, w_ref, of_ref, ob_ref, g_ref):
    g = _group_rows(u_ref)
    for a in range(GROUPS_PER_SLAB):
        g_ref[:, a * GROUP_K:(a + 1) * GROUP_K] = g[a]
    for pq in range(PAIRS_PER_SLAB):
        r = _dot(jnp.concatenate([g[2 * pq], g[2 * pq + 1]], axis=1), w_ref[pq])
        for dr, o_ref in enumerate((of_ref, ob_ref)):
            for ri in range(2):
                k = dr * 2 + ri
                o_ref[:, _state_tile(ri, pq)] = r[:, k * LANES:(k + 1) * LANES].astype(BF16)


def _s5_state_in(u, w_in_state, layer, rows=S5_ROWS):
    t = u.shape[0]
    state = jax.ShapeDtypeStruct((t // CHUNK, N_SLABS * DIR_COLS), BF16)
    spec = pl.BlockSpec((rows // CHUNK, DIR_COLS), lambda s, i: (i, s))
    return pl.pallas_call(
        _s5_in_kernel,
        grid=(N_SLABS, t // rows),
        in_specs=[pl.BlockSpec((rows, LANES), lambda s, i: (i, s)),
                  pl.BlockSpec((None,) + w_in_state.shape[1:], lambda s, i: (layer * N_SLABS + s, 0, 0, 0))],
        out_specs=[spec, spec, pl.BlockSpec((rows // CHUNK, CHUNK * LANES), lambda s, i: (i, s))],
        out_shape=[state, state, jax.ShapeDtypeStruct((t // CHUNK, CHUNK * S5_WIDTH), BF16)],
        compiler_params=_params(("parallel", "parallel")),
        name="s5_state_in",
    )(u, w_in_state)


def _s5_scan_kernel(xf_ref, xb_ref, af_ref, ab_ref, of_ref, ob_ref, st_ref, *, cpt):
    half = DIR_COLS // 2
    pair_rows = 2 * SUBLANES
    units = xf_ref.shape[1] // DIR_COLS
    lower, upper = slice(0, SUBLANES), slice(SUBLANES, pair_rows)

    @pl.when(pl.program_id(1) == 0)
    def _():
        st_ref[...] = jnp.zeros_like(st_ref)

    def two_chunks(x_ref, a_ref, o_ref, c2, state, reverse):
        rows = pl.ds(pl.multiple_of(c2 * pair_rows, pair_rows), pair_rows)
        first, second = (upper, lower) if reverse else (lower, upper)
        new_state = []
        for un in range(units):
            re = slice(un * DIR_COLS, un * DIR_COLS + half)
            im = slice(un * DIR_COLS + half, (un + 1) * DIR_COLS)
            ar, ai = a_ref[:, re], a_ref[:, im]
            xr = x_ref[rows, re].astype(F32)
            xi = x_ref[rows, im].astype(F32)
            s0r, s0i = state[un]
            s1r, s1i = ar * s0r - ai * s0i + xr[first], ar * s0i + ai * s0r + xi[first]
            s2r, s2i = ar * s1r - ai * s1i + xr[second], ar * s1i + ai * s1r + xi[second]
            enter_r, enter_i = ((s1r, s0r), (s1i, s0i)) if reverse else ((s0r, s1r), (s0i, s1i))
            o_ref[rows, re] = jnp.concatenate(enter_r, axis=0).astype(BF16)
            o_ref[rows, im] = jnp.concatenate(enter_i, axis=0).astype(BF16)
            new_state.append((s2r, s2i))
        return new_state

    def body(i, state):
        fwd, bwd = state
        return (two_chunks(xf_ref, af_ref, of_ref, i, fwd, False),
                two_chunks(xb_ref, ab_ref, ob_ref, cpt // 2 - 1 - i, bwd, True))

    def load(dr):
        return [(st_ref[dr, :, pl.ds(un * DIR_COLS, half)], st_ref[dr, :, pl.ds(un * DIR_COLS + half, half)])
                for un in range(units)]

    fwd, bwd = lax.fori_loop(0, cpt // 2, body, (load(0), load(1)))
    for dr, state in enumerate((fwd, bwd)):
        for un, (sr, si) in enumerate(state):
            st_ref[dr, :, pl.ds(un * DIR_COLS, half)] = sr
            st_ref[dr, :, pl.ds(un * DIR_COLS + half, half)] = si


def _s5_scan(sin_f, sin_b, a16, n_chunks, cpt=64, width=2 * DIR_COLS):
    nct = n_chunks // cpt
    rows = cpt * SUBLANES
    up = pl.BlockSpec((rows, width), lambda c, k: (k, c))
    down = pl.BlockSpec((rows, width), lambda c, k: (nct - 1 - k, c))
    coef = lambda dr: pl.BlockSpec((None, SUBLANES, width), lambda c, k: (dr, 0, c))
    return pl.pallas_call(
        functools.partial(_s5_scan_kernel, cpt=cpt),
        grid=(sin_f.shape[1] // width, nct),
        in_specs=[up, down, coef(0), coef(1)],
        out_specs=[up, down],
        out_shape=[jax.ShapeDtypeStruct(sin_f.shape, BF16), jax.ShapeDtypeStruct(sin_b.shape, BF16)],
        scratch_shapes=[pltpu.VMEM((2, SUBLANES, width), F32)],
        compiler_params=_params(("parallel", "arbitrary")),
        name="s5_scan",
    )(sin_f, sin_b, a16, a16)


def _s5_out_kernel(g_ref, cf_ref, cb_ref, wt_ref, wo_ref, y_ref):
    g = [g_ref[:, a * GROUP_K:(a + 1) * GROUP_K] for a in range(GROUPS_PER_SLAB)]
    ys = []
    for pq in range(PAIRS_PER_SLAB):
        carry = jnp.concatenate([c_ref[:, _state_tile(ri, pq)] for c_ref in (cf_ref, cb_ref) for ri in range(2)],
                                axis=1)
        from_state = _dot(carry, wo_ref[pq])
        for a2 in range(2):
            a = 2 * pq + a2
            ys.append((_dot(g[a], wt_ref[a]) + from_state[:, a2 * GROUP_K:(a2 + 1) * GROUP_K]).astype(BF16))
    halves = (_block_transpose([y[:, :LANES] for y in ys]), _block_transpose([y[:, LANES:] for y in ys]))
    per_tile = LAYOUT_ROWS // CHUNK
    for t in range(CHUNK):
        tok = halves[t // (CHUNK // 2)][t % (CHUNK // 2)]
        for k in range(y_ref.shape[0] // LAYOUT_ROWS):
            y_ref[_token_rows(t, k), :] = tok[k * per_tile:(k + 1) * per_tile, :]


def _s5_out(grouped, carry_f, carry_b, w_toep, w_out_state, layer, rows=S5_ROWS):
    t = grouped.shape[0] * CHUNK
    state = pl.BlockSpec((rows // CHUNK, DIR_COLS), lambda s, i: (i, s))
    return pl.pallas_call(
        _s5_out_kernel,
        grid=(N_SLABS, t // rows),
        in_specs=[pl.BlockSpec((rows // CHUNK, CHUNK * LANES), lambda s, i: (i, s)), state, state,
                  pl.BlockSpec((None,) + w_toep.shape[1:], lambda s, i: (layer * N_SLABS + s, 0, 0, 0)),
                  pl.BlockSpec((None,) + w_out_state.shape[1:], lambda s, i: (layer * N_SLABS + s, 0, 0, 0))],
        out_specs=pl.BlockSpec((rows, LANES), lambda s, i: (i, s)),
        out_shape=jax.ShapeDtypeStruct((t, S5_WIDTH), BF16),
        compiler_params=_params(("parallel", "parallel")),
        name="s5_out",
    )(grouped, carry_f, carry_b, w_toep, w_out_state)


def _s5_tables(lam_re, lam_im, log_dt, b_re, b_im, c_re, c_im, d):
    G, P, H, C, S = S5_GROUPS, S5_STATE, S5_GROUP, CHUNK, GROUPS_PER_SLAB
    lr = lam_re.astype(F32)
    li = lam_im.astype(F32)
    dt = jnp.exp(log_dt.astype(F32))[..., None]
    mag = jnp.exp(lr * dt)
    ab_re = mag * jnp.cos(li * dt)
    ab_im = mag * jnp.sin(li * dt)
    nr = ab_re - 1.0
    den = lr * lr + li * li
    coef_re = (nr * lr + ab_im * li) / den
    coef_im = (ab_im * lr - nr * li) / den
    br = b_re.astype(F32)
    bi = b_im.astype(F32)
    bb_re = coef_re[..., None] * br - coef_im[..., None] * bi
    bb_im = coef_re[..., None] * bi + coef_im[..., None] * br
    cr = c_re.astype(F32)
    ci = c_im.astype(F32)

    k = jnp.arange(C + 1, dtype=F32)[:, None, None, None]
    pmag = jnp.exp(k * (lr * dt))
    pw_re = pmag * jnp.cos(k * (li * dt))
    pw_im = pmag * jnp.sin(k * (li * dt))

    n_pairs = G // 2
    pw = jnp.stack([pw_re, pw_im], axis=0)
    up = jnp.arange(C)

    def table(k_fwd, k_bwd):
        return jnp.stack([pw[:, k_fwd, 0], pw[:, k_bwd, 1]], axis=0).transpose(3, 0, 1, 2, 4)

    by_state = lambda tab: tab.transpose(0, 1, 2, 4, 3).reshape(n_pairs, 2, 2, 2, P, C)
    lag_pw = by_state(table(up, C - 1 - up))
    out_pw = by_state(table(up + 1, C - up))
    in_pw = table(C - 1 - up, up).reshape(n_pairs, 2, 2, 2, C, P)
    in_pw = jnp.tile(in_pw, (1, 1, 1, 1, 1, 2))
    ct = jnp.stack([cr, ci], axis=0).transpose(2, 1, 0, 4, 3).reshape(n_pairs, 2, 2, 2, P, H)
    bbt = jnp.stack([bb_re, bb_im], axis=0).transpose(2, 1, 0, 4, 3).reshape(n_pairs, 2, 2, 2, H, P)
    own = jnp.eye(2, dtype=F32)[None, :, None, None, None, :, None]
    bb_own = (bbt[:, :, :, :, :, None, :] * own).reshape(n_pairs, 2, 2, 2, H, 2 * P)
    skip = jnp.eye(H, dtype=F32)[None] * d.astype(F32).reshape(G, 1, H)
    skip = jnp.pad(skip, ((0, 0), (0, 0), ((C - 1) * H, 0))).reshape(n_pairs, 2, H, GROUP_K)

    a16 = jnp.stack([pw_re[C], pw_im[C]], axis=0)
    a16 = a16.reshape(2, 2, N_SLABS, S, P).transpose(1, 2, 0, 3, 4)
    a16 = jnp.broadcast_to(a16.reshape(2, 1, N_SLABS * DIR_COLS), (2, SUBLANES, N_SLABS * DIR_COLS))
    return (ct, bbt, bb_own, lag_pw, out_pw, in_pw, skip), a16


def _s5_weights(*stacked_params):
    depth = stacked_params[0].shape[0]
    tabs, a16 = jax.vmap(_s5_tables)(*stacked_params)
    tabs = [a.reshape((depth * a.shape[1],) + a.shape[2:]) for a in tabs]
    rep_h = jnp.tile(jnp.eye(S5_GROUP, dtype=BF16), (1, CHUNK))
    rep_k = jnp.repeat(jnp.eye(CHUNK, dtype=BF16), S5_GROUP, axis=1)
    wt, wi, wo = _s5_pack(*tabs, rep_h, rep_k)
    wt = wt.reshape(depth * N_SLABS, GROUPS_PER_SLAB, GROUP_K, GROUP_K)
    wi = wi.reshape(depth * N_SLABS, PAIRS_PER_SLAB, 2 * GROUP_K, 4 * LANES)
    wo = wo.reshape(depth * N_SLABS, PAIRS_PER_SLAB, 4 * LANES, 2 * GROUP_K)
    return wi, wt, wo, a16


def _s5_pack_kernel(ct_ref, bbt_ref, bbo_ref, lagp_ref, outp_ref, inp_ref, skip_ref, reph_ref, repk_ref,
                    wt_ref, wi_ref, wo_ref):
    def split(x):
        hi = x.astype(BF16)
        return hi, (x - hi.astype(F32)).astype(BF16)

    def exact_dot(a, b):
        a_hi, a_lo = split(a)
        if b.dtype == BF16:
            return _dot(a_hi, b) + _dot(a_lo, b)
        b_hi, b_lo = split(b)
        return _dot(a_hi, b_hi) + (_dot(a_hi, b_lo) + _dot(a_lo, b_hi))

    rep_h = reph_ref[...]
    rep_k = repk_ref[...]
    wide = 2 * GROUP_K
    for a2 in range(2):
        lag_part = []
        for dr in range(2):
            c_re = exact_dot(ct_ref[a2, dr, 0], rep_h)
            c_im = exact_dot(ct_ref[a2, dr, 1], rep_h)

            def times_power(ref):
                p_re = exact_dot(ref[a2, dr, 0], rep_k)
                p_im = exact_dot(ref[a2, dr, 1], rep_k)
                return c_re * p_re - c_im * p_im, c_re * p_im + c_im * p_re

            l_re, l_im = times_power(lagp_ref)
            lag_part.append(exact_dot(bbt_ref[a2, dr, 0], l_re) - exact_dot(bbt_ref[a2, dr, 1], l_im))
            w_re, w_im = times_power(outp_ref)
            for ri, val in ((0, w_re), (1, -w_im)):
                rows = pl.ds((dr * 2 + ri) * LANES + a2 * S5_STATE, S5_STATE)
                wo_ref[rows, a2 * GROUP_K:(a2 + 1) * GROUP_K] = val.astype(BF16)
                wo_ref[rows, (1 - a2) * GROUP_K:(2 - a2) * GROUP_K] = jnp.zeros((S5_STATE, GROUP_K), BF16)
            b_re = bbo_ref[a2, dr, 0]
            b_im = bbo_ref[a2, dr, 1]
            for t in range(CHUNK):
                e_re = inp_ref[a2, dr, 0, t:t + 1, :]
                e_im = inp_ref[a2, dr, 1, t:t + 1, :]
                rows = pl.ds(a2 * GROUP_K + t * S5_GROUP, S5_GROUP)
                wi_ref[rows, (2 * dr) * LANES:(2 * dr + 1) * LANES] = (b_re * e_re - b_im * e_im).astype(BF16)
                wi_ref[rows, (2 * dr + 1) * LANES:(2 * dr + 2) * LANES] = (b_re * e_im + b_im * e_re).astype(BF16)
        zero = jnp.zeros((S5_GROUP, GROUP_K), F32)
        lags = (jnp.concatenate([lag_part[1] + skip_ref[a2], zero], axis=1)
                + pltpu.roll(jnp.concatenate([lag_part[0], zero], axis=1), (CHUNK - 1) * S5_GROUP, 1))
        for t in range(CHUNK):
            shift = (CHUNK - 1 - t) * S5_GROUP
            window = lags if shift == 0 else pltpu.roll(lags, wide - shift, 1)
            wt_ref[a2, t * S5_GROUP:(t + 1) * S5_GROUP, :] = window[:, :GROUP_K].astype(BF16)


def _s5_pack(ct, bbt, bb_own, lag_pw, out_pw, in_pw, skip, rep_h, rep_k):
    n_pairs = ct.shape[0]
    per_pair = lambda a: pl.BlockSpec((None,) + a.shape[1:], lambda i: (i,) + (0,) * (a.ndim - 1))
    tabs = (ct, bbt, bb_own, lag_pw, out_pw, in_pw, skip)
    return pl.pallas_call(
        _s5_pack_kernel,
        grid=(n_pairs,),
        in_specs=[per_pair(a) for a in tabs] + [_const_spec(rep_h.shape), _const_spec(rep_k.shape)],
        out_specs=[pl.BlockSpec((2, GROUP_K, GROUP_K), lambda i: (i, 0, 0)),
                   pl.BlockSpec((None, 2 * GROUP_K, 4 * LANES), lambda i: (i, 0, 0)),
                   pl.BlockSpec((None, 4 * LANES, 2 * GROUP_K), lambda i: (i, 0, 0))],
        out_shape=[jax.ShapeDtypeStruct((2 * n_pairs, GROUP_K, GROUP_K), BF16),
                   jax.ShapeDtypeStruct((n_pairs, 2 * GROUP_K, 4 * LANES), BF16),
                   jax.ShapeDtypeStruct((n_pairs, 4 * LANES, 2 * GROUP_K), BF16)],
        compiler_params=_params(("parallel",)),
        name="s5_pack",
    )(*tabs, rep_h, rep_k)


def _attn_kernel(sink_ref, q_ref, kl_ref, kc_ref, kr_ref, bias_first_ref, bias_mid_ref, bias_last_ref, o_ref):
    nk = 3 * BLOCK
    kv_all = jnp.concatenate([kl_ref[...], kc_ref[...], kr_ref[...]], axis=0)
    low = lax.broadcasted_iota(jnp.int32, (kv_all.shape[0], LANES), 1) < HEAD_DIM
    zero = jnp.zeros((kv_all.shape[0], LANES), BF16)

    def diag_parts(tile, kvh):
        other = pltpu.roll(tile, HEAD_DIM, 1)
        first, second = (tile, other) if kvh == 0 else (other, tile)
        return jnp.where(low, first, zero), jnp.where(low, zero, second)

    def window(parts, qb):
        return jnp.concatenate([p[qb * BLOCK:qb * BLOCK + nk] for p in parts], axis=0)

    top = lax.broadcasted_iota(jnp.int32, (2 * nk, LANES), 0) < nk
    ones_bd = (top == (lax.broadcasted_iota(jnp.int32, (2 * nk, LANES), 1) < HEAD_DIM)).astype(BF16)

    upper = lax.broadcasted_iota(jnp.int32, (2 * BLOCK, 1), 0) < BLOCK
    bias_refs = [bias_first_ref] + [bias_mid_ref] * (ATT_QB - 2) + [bias_last_ref]
    for kvh in range(N_KV_HEADS):
        k_parts = diag_parts(kv_all[:, :LANES], kvh)
        v_parts = diag_parts(kv_all[:, LANES:], kvh)
        for qb, bias_ref in enumerate(bias_refs):
            rows = slice(qb * BLOCK, (qb + 1) * BLOCK)
            kbd = window(k_parts, qb)
            vbd = jnp.concatenate([window(v_parts, qb), ones_bd], axis=1)
            tiles = [(kvh * 2 + pair) * LANES for pair in range(2)]
            q2 = jnp.concatenate([q_ref[rows, tl:tl + LANES] for tl in tiles], axis=0)
            s = lax.dot_general(q2, kbd, (((1,), (1,)), ((), ())), preferred_element_type=F32)
            s = s + bias_ref[kvh]
            ps, sinks = [], []
            for e in range(2):
                se = s[:, e * nk:(e + 1) * nk]
                sk = jnp.where(upper, sink_ref[4 * kvh + e], sink_ref[4 * kvh + 2 + e])
                mx = jnp.maximum(jnp.max(se, axis=-1, keepdims=True), sk)
                ps.append(jnp.exp2((se - mx).astype(BF16)))
                sinks.append(jnp.broadcast_to(jnp.exp2(sk - mx), (2 * BLOCK, HEAD_DIM)))
            nd = _dot(jnp.concatenate(ps, axis=1), vbd)
            o = (nd[:, :LANES] / (nd[:, LANES:] + jnp.concatenate(sinks, axis=1))).astype(BF16)
            for pair, tl in enumerate(tiles):
                o_ref[rows, tl:tl + LANES] = o[pair * BLOCK:(pair + 1) * BLOCK, :]


def _attention(q, kv, sink, bias, bsz, seq):
    nb = seq // BLOCK
    assert nb % ATT_QB == 0 and ATT_QB >= 2
    nj = nb // ATT_QB
    rows = ATT_QB * BLOCK
    centre = lambda b, j: (b * nj + j, 0)
    left = lambda b, j: (b * nb + jnp.maximum(ATT_QB * j - 1, 0), 0)
    right = lambda b, j: (b * nb + jnp.minimum(ATT_QB * (j + 1), nb - 1), 0)
    variant = (None,) + bias.shape[1:]
    bias_first = pl.BlockSpec(variant, lambda b, j: (jnp.where(j == 0, 0, 1), 0, 0, 0))
    bias_mid = pl.BlockSpec(variant, lambda b, j: (1, 0, 0, 0), pipeline_mode=pl.Buffered(1))
    bias_last = pl.BlockSpec(variant, lambda b, j: (jnp.where(j == nj - 1, 2, 1), 0, 0, 0))
    return pl.pallas_call(
        _attn_kernel,
        grid=(bsz, nj),
        in_specs=[pl.BlockSpec(memory_space=pltpu.SMEM),
                  pl.BlockSpec((rows, ATT_WIDTH), centre),
                  pl.BlockSpec((BLOCK, 2 * KV_WIDTH), left),
                  pl.BlockSpec((rows, 2 * KV_WIDTH), centre),
                  pl.BlockSpec((BLOCK, 2 * KV_WIDTH), right),
                  bias_first, bias_mid, bias_last],
        out_specs=pl.BlockSpec((rows, ATT_WIDTH), centre),
        out_shape=jax.ShapeDtypeStruct((bsz * seq, ATT_WIDTH), BF16),
        compiler_params=_params(("parallel", "parallel")),
        name="attention",
    )(sink, q, kv, kv, kv, bias, bias, bias)


def _t5_bucket(rel):
    half = NUM_BUCKETS // 2
    max_exact = half // 2
    ret = jnp.where(rel > 0, half, 0)
    n = jnp.abs(rel)
    nf = jnp.maximum(n, 1).astype(jnp.float32)
    large = max_exact + (jnp.log(nf / max_exact) / math.log(MAX_DISTANCE / max_exact)
                         * (half - max_exact)).astype(jnp.int32)
    large = jnp.minimum(large, half - 1)
    return ret + jnp.where(n < max_exact, n, large)


def _band_bias(rel_bias):
    q_loc = jnp.arange(BLOCK, dtype=jnp.int32)
    k_loc = jnp.arange(3 * BLOCK, dtype=jnp.int32)
    rel = (k_loc[None, :] - BLOCK) - q_loc[:, None]
    onehot = (_t5_bucket(rel)[None] == jnp.arange(NUM_BUCKETS, dtype=jnp.int32)[:, None, None]).astype(F32)
    bias = jnp.einsum('bh,bqk->hqk', rel_bias.astype(F32), onehot, precision=lax.Precision.HIGHEST)
    bias = jnp.where((jnp.abs(rel) <= WINDOW)[None], bias * LOG2E, NEG_INF)
    keep = jnp.stack([k_loc >= BLOCK, k_loc >= 0, k_loc < 2 * BLOCK], axis=0)
    bias = jnp.where(keep[:, None, None, :], bias[None], NEG_INF)
    bias = bias.reshape(3, N_Q_HEADS // 2, 2, BLOCK, 3 * BLOCK).transpose(0, 1, 3, 2, 4)
    return bias.reshape(3, N_KV_HEADS, 2 * BLOCK, 6 * BLOCK)


def _merge_ffn_body(h_ref, y_ref, yb_ref, g1_ref, g2_ref, gf_ref, unperm_ref, wg_ref, wglu_ref, wa_ref, wb_ref,
                    wo_ref, fg_ref, fu_ref, fd_ref, o_ref, acc_ref, *, nxt, final_norm):
    bsz = h_ref.shape[0]
    h = _tile_rows(h_ref)
    hn = _rms(h, g1_ref[...]).astype(BF16)
    gates = _sigmoid(_dot(hn, wg_ref[...]))
    y = jnp.concatenate([_dot(unperm_ref[...], y_ref[rows, :]) for rows, _ in _layout_tiles()], axis=0)
    z = _gelu_tanh(y)
    za = (z * _sigmoid(_dot(z.astype(BF16), wglu_ref[...]))).astype(BF16)
    yb = _tile_rows(yb_ref)
    merged = gates[:, :D_MODEL] * _dot(za, wa_ref[...]) + gates[:, D_MODEL:] * _dot(yb, wb_ref[...])
    h = h + _dot(merged.astype(BF16), wo_ref[...])

    hn = _rms(h, g2_ref[...]).astype(BF16)
    acc_ref[...] = h
    for j in range(D_FF // FF_TILE):
        cols = slice(j * FF_TILE, (j + 1) * FF_TILE)
        gate = _dot(hn, fg_ref[:, cols])
        up = _dot(hn, fu_ref[:, cols])
        act = (gate * _sigmoid(gate) * up).astype(BF16)
        acc_ref[...] += _dot(act, fd_ref[cols, :])
    out = acc_ref[...]
    if final_norm:
        out = _rms(out, gf_ref[...])
    else:
        gn_ref, wn_ref, perm_ref, u_ref, q_ref, kv_ref = nxt
        _inproj_rows(out, gn_ref, wn_ref, perm_ref, u_ref, q_ref, kv_ref)
    for rows, tok in _layout_tiles():
        o_ref[:, tok, :] = out[rows, :].reshape(bsz, LAYOUT_TOK, D_MODEL)


def _merge_ffn_last_kernel(*refs):
    *ins, o_ref, acc_ref = refs
    _merge_ffn_body(*ins, o_ref, acc_ref, nxt=None, final_norm=True)


def _merge_ffn_next_kernel(*refs):
    *ins, gn_ref, wn_ref, perm_ref, o_ref, u_ref, q_ref, kv_ref, acc_ref = refs
    _merge_ffn_body(*ins, o_ref, acc_ref, nxt=(gn_ref, wn_ref, perm_ref, u_ref, q_ref, kv_ref), final_norm=False)


def _merge_ffn(h3, y, yb3, g1, g2, gf, unperm, weights, layer, next_inproj=None):
    bsz, seq, _ = h3.shape
    vec = _const_spec((1, D_MODEL))
    in_specs = [_tok_spec(bsz, D_MODEL), pl.BlockSpec((TILE_ROWS, S5_WIDTH), lambda i: (i, 0)),
                _tok_spec(bsz, ATT_WIDTH), vec, vec, vec, _const_spec(unperm.shape)]
    in_specs += [_layer_spec(w, layer) for w in weights]
    args = [h3, y, yb3, g1, g2, gf, unperm, *weights]
    out_specs, out_shape = [_tok_spec(bsz, D_MODEL)], [jax.ShapeDtypeStruct((bsz, seq, D_MODEL), F32)]
    if next_inproj is None:
        body = _merge_ffn_last_kernel
    else:
        body = _merge_ffn_next_kernel
        gain_next, w_next, perm = next_inproj
        in_specs += [vec, _layer_spec(w_next, layer + 1), _const_spec(perm.shape)]
        args += [gain_next, w_next, perm]
        specs, shapes = _inproj_out(bsz, seq)
        out_specs += specs
        out_shape += shapes
    outs = pl.pallas_call(
        body,
        grid=(seq // TOK_TILE,),
        in_specs=in_specs,
        out_specs=out_specs,
        out_shape=out_shape,
        scratch_shapes=[pltpu.VMEM((TILE_ROWS, D_MODEL), F32)],
        compiler_params=_params(("parallel",)),
        name="merge_ffn",
    )(*args)
    return outs[0] if next_inproj is None else outs


def kernel(x, norm1_g, norm2_g, final_g, w_in, s5_lambda_re, s5_lambda_im, s5_log_dt, s5_b_re, s5_b_im,
           s5_c_re, s5_c_im, s5_d, s5_w_glu, attn_sink, rel_bias, w_branch_a, w_branch_b, w_out,
           ffn_w_gate, ffn_w_up, ffn_w_down):
    bsz, seq, _ = x.shape
    depth = w_in.shape[0]
    assert seq % BLOCK == 0 and seq % CHUNK == 0 and bsz == SUBLANES
    n_chunks = seq // CHUNK
    t = bsz * seq
    bias = _band_bias(rel_bias)
    o_k = S5_WIDTH + ATT_WIDTH
    o_g = o_k + 2 * KV_WIDTH
    col_scale = jnp.concatenate([jnp.ones((S5_WIDTH,), F32), jnp.full((ATT_WIDTH,), LOG2E * HEAD_DIM ** -0.5, F32),
                                 jnp.ones((2 * KV_WIDTH,), F32)])
    gf = final_g.reshape(1, D_MODEL).astype(F32)
    perm = _to_chunk_order()
    unperm = perm.T
    w_in_state, w_toep, w_out_state, a16 = _s5_weights(
        s5_lambda_re, s5_lambda_im, s5_log_dt, s5_b_re, s5_b_im, s5_c_re, s5_c_im, s5_d)
    gain1 = lambda layer: norm1_g[layer].reshape(1, D_MODEL).astype(F32)
    w_uqkv = (w_in[:, :, :o_g] * col_scale).astype(BF16)
    weights = [w.astype(BF16) for w in (w_in[:, :, o_g:], s5_w_glu, w_branch_a, w_branch_b, w_out,
                                        ffn_w_gate, ffn_w_up, ffn_w_down)]
    h = x
    u, q, kv = _inproj(h, gain1(0), w_uqkv, perm, 0)
    for layer in range(depth):
        g1 = gain1(layer)
        g2 = norm2_g[layer].reshape(1, D_MODEL).astype(F32)
        sin_f, sin_b, grouped = _s5_state_in(u, w_in_state, layer)
        carry_f, carry_b = _s5_scan(sin_f, sin_b, a16[layer], n_chunks)
        y = _s5_out(grouped, carry_f, carry_b, w_toep, w_out_state, layer)
        yb = _attention(q.reshape(t, ATT_WIDTH), kv.reshape(t, 2 * KV_WIDTH),
                        attn_sink[layer].astype(F32) * LOG2E, bias, bsz, seq)
        yb3 = yb.reshape(bsz, seq, ATT_WIDTH)
        if layer == depth - 1:
            h = _merge_ffn(h, y, yb3, g1, g2, gf, unperm, weights, layer)
        else:
            h, u, q, kv = _merge_ffn(h, y, yb3, g1, g2, gf, unperm, weights, layer,
                                     next_inproj=(gain1(layer + 1), w_uqkv, perm))
    return h
```

```python
import functools
import math

import jax
import jax.numpy as jnp
from jax import lax
from jax.experimental import pallas as pl
from jax.experimental.pallas import tpu as pltpu

F32 = jnp.float32
BF16 = jnp.bfloat16

D_MODEL = 1024
S5_WIDTH = 512
S5_GROUP = 16
S5_GROUPS = 32
S5_STATE = 64
HEAD_DIM = 64
N_Q_HEADS = 8
N_KV_HEADS = 2
Q_PER_KV = N_Q_HEADS // N_KV_HEADS
ATT_WIDTH = N_Q_HEADS * HEAD_DIM
KV_WIDTH = N_KV_HEADS * HEAD_DIM
WINDOW = 128
BLOCK = 128
NUM_BUCKETS = 32
MAX_DISTANCE = 128
D_FF = 2816
RMS_EPS = 1e-6
NEG_INF = -1e30
LOG2E = math.log2(math.e)

LANES = 128
SUBLANES = 8
CHUNK = 16
GROUPS_PER_SLAB = LANES // S5_GROUP
N_SLABS = S5_WIDTH // LANES
PAIRS_PER_SLAB = GROUPS_PER_SLAB // 2
GROUP_K = CHUNK * S5_GROUP
DIR_COLS = 2 * GROUPS_PER_SLAB * S5_STATE
TOK_TILE = 64
TILE_ROWS = SUBLANES * TOK_TILE
LAYOUT_TOK = 32
LAYOUT_ROWS = SUBLANES * LAYOUT_TOK
S5_ROWS = 32 * TILE_ROWS
FF_TILE = 256
ATT_QB = 16
VMEM_LIMIT = 56 * 1024 * 1024


def _rms(x, g):
    return x * lax.rsqrt(jnp.mean(x * x, axis=-1, keepdims=True) + RMS_EPS) * g


def _gelu_tanh(x):
    return 0.5 * x * (1.0 + jnp.tanh(math.sqrt(2.0 / math.pi) * (x + 0.044715 * (x * x * x))))


def _sigmoid(x):
    return 1.0 / (1.0 + jnp.exp(-x))


def _dot(a, b):
    return jnp.dot(a, b, preferred_element_type=F32)


def _const_spec(shape):
    nd = len(shape)
    return pl.BlockSpec(shape, lambda *_: (0,) * nd, pipeline_mode=pl.Buffered(1))


def _layer_spec(stacked, layer):
    tail = (0,) * (stacked.ndim - 1)
    return pl.BlockSpec((None,) + stacked.shape[1:], lambda *_: (layer,) + tail, pipeline_mode=pl.Buffered(1))


def _params(sem):
    return pltpu.CompilerParams(dimension_semantics=sem, vmem_limit_bytes=VMEM_LIMIT)


def _to_chunk_order():
    cpt = LAYOUT_TOK // CHUNK
    dst = jnp.arange(LAYOUT_ROWS)
    t, c, b = dst // (cpt * SUBLANES), (dst // SUBLANES) % cpt, dst % SUBLANES
    src = b * LAYOUT_TOK + c * CHUNK + t
    return (src[:, None] == jnp.arange(LAYOUT_ROWS)[None, :]).astype(BF16)


def _layout_tiles():
    return [(slice(s * LAYOUT_ROWS, (s + 1) * LAYOUT_ROWS), slice(s * LAYOUT_TOK, (s + 1) * LAYOUT_TOK))
            for s in range(TOK_TILE // LAYOUT_TOK)]


def _tile_rows(ref):
    return jnp.concatenate([ref[:, tok, :].reshape(LAYOUT_ROWS, ref.shape[2]) for _, tok in _layout_tiles()], axis=0)


def _inproj_rows(h, g_ref, w_ref, perm_ref, u_ref, q_ref, kv_ref):
    bsz = q_ref.shape[0]
    hn = _rms(h, g_ref[...]).astype(BF16)
    r = _dot(hn, w_ref[...])
    for rows, tok in _layout_tiles():
        u_ref[rows, :] = _dot(perm_ref[...], r[rows, :S5_WIDTH].astype(BF16)).astype(BF16)
        q_ref[:, tok, :] = r[rows, S5_WIDTH:S5_WIDTH + ATT_WIDTH].astype(BF16).reshape(bsz, LAYOUT_TOK, ATT_WIDTH)
        kv_ref[:, tok, :] = r[rows, S5_WIDTH + ATT_WIDTH:].astype(BF16).reshape(bsz, LAYOUT_TOK, 2 * KV_WIDTH)


def _inproj_kernel(x_ref, g_ref, w_ref, perm_ref, u_ref, q_ref, kv_ref):
    _inproj_rows(_tile_rows(x_ref), g_ref, w_ref, perm_ref, u_ref, q_ref, kv_ref)


def _tok_spec(bsz, width):
    return pl.BlockSpec((bsz, TOK_TILE, width), lambda i: (0, i, 0))


def _inproj_out(bsz, seq):
    specs = [pl.BlockSpec((TILE_ROWS, S5_WIDTH), lambda i: (i, 0)), _tok_spec(bsz, ATT_WIDTH),
             _tok_spec(bsz, 2 * KV_WIDTH)]
    shapes = [jax.ShapeDtypeStruct((bsz * seq, S5_WIDTH), BF16),
              jax.ShapeDtypeStruct((bsz, seq, ATT_WIDTH), BF16),
              jax.ShapeDtypeStruct((bsz, seq, 2 * KV_WIDTH), BF16)]
    return specs, shapes


def _inproj(h3, g, w_stacked, perm, layer):
    bsz, seq, _ = h3.shape
    assert bsz * TOK_TILE == TILE_ROWS
    specs, shapes = _inproj_out(bsz, seq)
    return pl.pallas_call(
        _inproj_kernel,
        grid=(seq // TOK_TILE,),
        in_specs=[_tok_spec(bsz, D_MODEL), _const_spec((1, D_MODEL)), _layer_spec(w_stacked, layer),
                  _const_spec(perm.shape)],
        out_specs=specs,
        out_shape=shapes,
        compiler_params=_params(("parallel",)),
        name="inproj",
    )(h3, g, w_stacked, perm)


def _block_transpose(x):
    x = list(x)
    blk = lax.broadcasted_iota(jnp.int32, x[0].shape, 1) // S5_GROUP
    for d in (4, 2, 1):
        keep = (blk & d) == 0
        for i in range(GROUPS_PER_SLAB):
            if i & d:
                continue
            xi, xj = x[i], x[i + d]
            x[i] = jnp.where(keep, xi, pltpu.roll(xj, d * S5_GROUP, 1))
            x[i + d] = jnp.where(keep, pltpu.roll(xi, LANES - d * S5_GROUP, 1), xj)
    return x


def _token_rows(t, k):
    return pl.ds(k * LAYOUT_ROWS + t * (LAYOUT_ROWS // CHUNK), LAYOUT_ROWS // CHUNK)


def _group_rows(u_ref):
    tiles = u_ref.shape[0] // LAYOUT_ROWS
    tok = [jnp.concatenate([u_ref[_token_rows(t, k), :] for k in range(tiles)], axis=0) for t in range(CHUNK)]
    lo = _block_transpose(tok[:CHUNK // 2])
    hi = _block_transpose(tok[CHUNK // 2:])
    return [jnp.concatenate([lo[a], hi[a]], axis=1) for a in range(GROUPS_PER_SLAB)]


def _state_tile(ri, pq):
    return pl.ds(ri * (DIR_COLS // 2) + pq * LANES, LANES)


def _s5_in_kernel(u_ref, w_ref, of_ref, ob_ref, g_ref):
    g = _group_rows(u_ref)
    for a in range(GROUPS_PER_SLAB):
        g_ref[:, a * GROUP_K:(a + 1) * GROUP_K] = g[a]
    for pq in range(PAIRS_PER_SLAB):
        r = _dot(jnp.concatenate([g[2 * pq], g[2 * pq + 1]], axis=1), w_ref[pq])
        for dr, o_ref in enumerate((of_ref, ob_ref)):
            for ri in range(2):
                k = dr * 2 + ri
                o_ref[:, _state_tile(ri, pq)] = r[:, k * LANES:(k + 1) * LANES].astype(BF16)


def _s5_state_in(u, w_in_state, layer, rows=S5_ROWS):
    t = u.shape[0]
    state = jax.ShapeDtypeStruct((t // CHUNK, N_SLABS * DIR_COLS), BF16)
    spec = pl.BlockSpec((rows // CHUNK, DIR_COLS), lambda s, i: (i, s))
    return pl.pallas_call(
        _s5_in_kernel,
        grid=(N_SLABS, t // rows),
        in_specs=[pl.BlockSpec((rows, LANES), lambda s, i: (i, s)),
                  pl.BlockSpec((None,) + w_in_state.shape[1:], lambda s, i: (layer * N_SLABS + s, 0, 0, 0))],
        out_specs=[spec, spec, pl.BlockSpec((rows // CHUNK, CHUNK * LANES), lambda s, i: (i, s))],
        out_shape=[state, state, jax.ShapeDtypeStruct((t // CHUNK, CHUNK * S5_WIDTH), BF16)],
        compiler_params=_params(("parallel", "parallel")),
        name="s5_state_in",
    )(u, w_in_state)


def _s5_scan_kernel(xf_ref, xb_ref, af_ref, ab_ref, of_ref, ob_ref, st_ref, *, cpt):
    half = DIR_COLS // 2
    pair_rows = 2 * SUBLANES
    units = xf_ref.shape[1] // DIR_COLS
    lower, upper = slice(0, SUBLANES), slice(SUBLANES, pair_rows)

    @pl.when(pl.program_id(1) == 0)
    def _():
        st_ref[...] = jnp.zeros_like(st_ref)

    def two_chunks(x_ref, a_ref, o_ref, c2, state, reverse):
        rows = pl.ds(pl.multiple_of(c2 * pair_rows, pair_rows), pair_rows)
        first, second = (upper, lower) if reverse else (lower, upper)
        new_state = []
        for un in range(units):
            re = slice(un * DIR_COLS, un * DIR_COLS + half)
            im = slice(un * DIR_COLS + half, (un + 1) * DIR_COLS)
            ar, ai = a_ref[:, re], a_ref[:, im]
            xr = x_ref[rows, re].astype(F32)
            xi = x_ref[rows, im].astype(F32)
            s0r, s0i = state[un]
            s1r, s1i = ar * s0r - ai * s0i + xr[first], ar * s0i + ai * s0r + xi[first]
            s2r, s2i = ar * s1r - ai * s1i + xr[second], ar * s1i + ai * s1r + xi[second]
            enter_r, enter_i = ((s1r, s0r), (s1i, s0i)) if reverse else ((s0r, s1r), (s0i, s1i))
            o_ref[rows, re] = jnp.concatenate(enter_r, axis=0).astype(BF16)
            o_ref[rows, im] = jnp.concatenate(enter_i, axis=0).astype(BF16)
            new_state.append((s2r, s2i))
        return new_state

    def body(i, state):
        fwd, bwd = state
        return (two_chunks(xf_ref, af_ref, of_ref, i, fwd, False),
                two_chunks(xb_ref, ab_ref, ob_ref, cpt // 2 - 1 - i, bwd, True))

    def load(dr):
        return [(st_ref[dr, :, pl.ds(un * DIR_COLS, half)], st_ref[dr, :, pl.ds(un * DIR_COLS + half, half)])
                for un in range(units)]

    fwd, bwd = lax.fori_loop(0, cpt // 2, body, (load(0), load(1)))
    for dr, state in enumerate((fwd, bwd)):
        for un, (sr, si) in enumerate(state):
            st_ref[dr, :, pl.ds(un * DIR_COLS, half)] = sr
            st_ref[dr, :, pl.ds(un * DIR_COLS + half, half)] = si


def _s5_scan(sin_f, sin_b, a16, n_chunks, cpt=64, width=2 * DIR_COLS):
    nct = n_chunks // cpt
    rows = cpt * SUBLANES
    up = pl.BlockSpec((rows, width), lambda c, k: (k, c))
    down = pl.BlockSpec((rows, width), lambda c, k: (nct - 1 - k, c))
    coef = lambda dr: pl.BlockSpec((None, SUBLANES, width), lambda c, k: (dr, 0, c))
    return pl.pallas_call(
        functools.partial(_s5_scan_kernel, cpt=cpt),
        grid=(sin_f.shape[1] // width, nct),
        in_specs=[up, down, coef(0), coef(1)],
        out_specs=[up, down],
        out_shape=[jax.ShapeDtypeStruct(sin_f.shape, BF16), jax.ShapeDtypeStruct(sin_b.shape, BF16)],
        scratch_shapes=[pltpu.VMEM((2, SUBLANES, width), F32)],
        compiler_params=_params(("parallel", "arbitrary")),
        name="s5_scan",
    )(sin_f, sin_b, a16, a16)


def _s5_out_kernel(g_ref, cf_ref, cb_ref, wt_ref, wo_ref, y_ref):
    g = [g_ref[:, a * GROUP_K:(a + 1) * GROUP_K] for a in range(GROUPS_PER_SLAB)]
    ys = []
    for pq in range(PAIRS_PER_SLAB):
        carry = jnp.concatenate([c_ref[:, _state_tile(ri, pq)] for c_ref in (cf_ref, cb_ref) for ri in range(2)],
                                axis=1)
        from_state = _dot(carry, wo_ref[pq])
        for a2 in range(2):
            a = 2 * pq + a2
            ys.append((_dot(g[a], wt_ref[a]) + from_state[:, a2 * GROUP_K:(a2 + 1) * GROUP_K]).astype(BF16))
    halves = (_block_transpose([y[:, :LANES] for y in ys]), _block_transpose([y[:, LANES:] for y in ys]))
    per_tile = LAYOUT_ROWS // CHUNK
    for t in range(CHUNK):
        tok = halves[t // (CHUNK // 2)][t % (CHUNK // 2)]
        for k in range(y_ref.shape[0] // LAYOUT_ROWS):
            y_ref[_token_rows(t, k), :] = tok[k * per_tile:(k + 1) * per_tile, :]


def _s5_out(grouped, carry_f, carry_b, w_toep, w_out_state, layer, rows=S5_ROWS):
    t = grouped.shape[0] * CHUNK
    state = pl.BlockSpec((rows // CHUNK, DIR_COLS), lambda s, i: (i, s))
    return pl.pallas_call(
        _s5_out_kernel,
        grid=(N_SLABS, t // rows),
        in_specs=[pl.BlockSpec((rows // CHUNK, CHUNK * LANES), lambda s, i: (i, s)), state, state,
                  pl.BlockSpec((None,) + w_toep.shape[1:], lambda s, i: (layer * N_SLABS + s, 0, 0, 0)),
                  pl.BlockSpec((None,) + w_out_state.shape[1:], lambda s, i: (layer * N_SLABS + s, 0, 0, 0))],
        out_specs=pl.BlockSpec((rows, LANES), lambda s, i: (i, s)),
        out_shape=jax.ShapeDtypeStruct((t, S5_WIDTH), BF16),
        compiler_params=_params(("parallel", "parallel")),
        name="s5_out",
    )(grouped, carry_f, carry_b, w_toep, w_out_state)


def _s5_tables(lam_re, lam_im, log_dt, b_re, b_im, c_re, c_im, d):
    G, P, H, C, S = S5_GROUPS, S5_STATE, S5_GROUP, CHUNK, GROUPS_PER_SLAB
    lr = lam_re.astype(F32)
    li = lam_im.astype(F32)
    dt = jnp.exp(log_dt.astype(F32))[..., None]
    mag = jnp.exp(lr * dt)
    ab_re = mag * jnp.cos(li * dt)
    ab_im = mag * jnp.sin(li * dt)
    nr = ab_re - 1.0
    den = lr * lr + li * li
    coef_re = (nr * lr + ab_im * li) / den
    coef_im = (ab_im * lr - nr * li) / den
    br = b_re.astype(F32)
    bi = b_im.astype(F32)
    bb_re = coef_re[..., None] * br - coef_im[..., None] * bi
    bb_im = coef_re[..., None] * bi + coef_im[..., None] * br
    cr = c_re.astype(F32)
    ci = c_im.astype(F32)

    k = jnp.arange(C + 1, dtype=F32)[:, None, None, None]
    pmag = jnp.exp(k * (lr * dt))
    pw_re = pmag * jnp.cos(k * (li * dt))
    pw_im = pmag * jnp.sin(k * (li * dt))

    n_pairs = G // 2
    pw = jnp.stack([pw_re, pw_im], axis=0)
    up = jnp.arange(C)

    def table(k_fwd, k_bwd):
        return jnp.stack([pw[:, k_fwd, 0], pw[:, k_bwd, 1]], axis=0).transpose(3, 0, 1, 2, 4)

    by_state = lambda tab: tab.transpose(0, 1, 2, 4, 3).reshape(n_pairs, 2, 2, 2, P, C)
    lag_pw = by_state(table(up, C - 1 - up))
    out_pw = by_state(table(up + 1, C - up))
    in_pw = table(C - 1 - up, up).reshape(n_pairs, 2, 2, 2, C, P)
    in_pw = jnp.tile(in_pw, (1, 1, 1, 1, 1, 2))
    ct = jnp.stack([cr, ci], axis=0).transpose(2, 1, 0, 4, 3).reshape(n_pairs, 2, 2, 2, P, H)
    bbt = jnp.stack([bb_re, bb_im], axis=0).transpose(2, 1, 0, 4, 3).reshape(n_pairs, 2, 2, 2, H, P)
    own = jnp.eye(2, dtype=F32)[None, :, None, None, None, :, None]
    bb_own = (bbt[:, :, :, :, :, None, :] * own).reshape(n_pairs, 2, 2, 2, H, 2 * P)
    skip = jnp.eye(H, dtype=F32)[None] * d.astype(F32).reshape(G, 1, H)
    skip = jnp.pad(skip, ((0, 0), (0, 0), ((C - 1) * H, 0))).reshape(n_pairs, 2, H, GROUP_K)

    a16 = jnp.stack([pw_re[C], pw_im[C]], axis=0)
    a16 = a16.reshape(2, 2, N_SLABS, S, P).transpose(1, 2, 0, 3, 4)
    a16 = jnp.broadcast_to(a16.reshape(2, 1, N_SLABS * DIR_COLS), (2, SUBLANES, N_SLABS * DIR_COLS))
    return (ct, bbt, bb_own, lag_pw, out_pw, in_pw, skip), a16


def _s5_weights(*stacked_params):
    depth = stacked_params[0].shape[0]
    tabs, a16 = jax.vmap(_s5_tables)(*stacked_params)
    tabs = [a.reshape((depth * a.shape[1],) + a.shape[2:]) for a in tabs]
    rep_h = jnp.tile(jnp.eye(S5_GROUP, dtype=BF16), (1, CHUNK))
    rep_k = jnp.repeat(jnp.eye(CHUNK, dtype=BF16), S5_GROUP, axis=1)
    wt, wi, wo = _s5_pack(*tabs, rep_h, rep_k)
    wt = wt.reshape(depth * N_SLABS, GROUPS_PER_SLAB, GROUP_K, GROUP_K)
    wi = wi.reshape(depth * N_SLABS, PAIRS_PER_SLAB, 2 * GROUP_K, 4 * LANES)
    wo = wo.reshape(depth * N_SLABS, PAIRS_PER_SLAB, 4 * LANES, 2 * GROUP_K)
    return wi, wt, wo, a16


def _s5_pack_kernel(ct_ref, bbt_ref, bbo_ref, lagp_ref, outp_ref, inp_ref, skip_ref, reph_ref, repk_ref,
                    wt_ref, wi_ref, wo_ref):
    def split(x):
        hi = x.astype(BF16)
        return hi, (x - hi.astype(F32)).astype(BF16)

    def exact_dot(a, b):
        a_hi, a_lo = split(a)
        if b.dtype == BF16:
            return _dot(a_hi, b) + _dot(a_lo, b)
        b_hi, b_lo = split(b)
        return _dot(a_hi, b_hi) + (_dot(a_hi, b_lo) + _dot(a_lo, b_hi))

    rep_h = reph_ref[...]
    rep_k = repk_ref[...]
    wide = 2 * GROUP_K
    for a2 in range(2):
        lag_part = []
        for dr in range(2):
            c_re = exact_dot(ct_ref[a2, dr, 0], rep_h)
            c_im = exact_dot(ct_ref[a2, dr, 1], rep_h)

            def times_power(ref):
                p_re = exact_dot(ref[a2, dr, 0], rep_k)
                p_im = exact_dot(ref[a2, dr, 1], rep_k)
                return c_re * p_re - c_im * p_im, c_re * p_im + c_im * p_re

            l_re, l_im = times_power(lagp_ref)
            lag_part.append(exact_dot(bbt_ref[a2, dr, 0], l_re) - exact_dot(bbt_ref[a2, dr, 1], l_im))
            w_re, w_im = times_power(outp_ref)
            for ri, val in ((0, w_re), (1, -w_im)):
                rows = pl.ds((dr * 2 + ri) * LANES + a2 * S5_STATE, S5_STATE)
                wo_ref[rows, a2 * GROUP_K:(a2 + 1) * GROUP_K] = val.astype(BF16)
                wo_ref[rows, (1 - a2) * GROUP_K:(2 - a2) * GROUP_K] = jnp.zeros((S5_STATE, GROUP_K), BF16)
            b_re = bbo_ref[a2, dr, 0]
            b_im = bbo_ref[a2, dr, 1]
            for t in range(CHUNK):
                e_re = inp_ref[a2, dr, 0, t:t + 1, :]
                e_im = inp_ref[a2, dr, 1, t:t + 1, :]
                rows = pl.ds(a2 * GROUP_K + t * S5_GROUP, S5_GROUP)
                wi_ref[rows, (2 * dr) * LANES:(2 * dr + 1) * LANES] = (b_re * e_re - b_im * e_im).astype(BF16)
                wi_ref[rows, (2 * dr + 1) * LANES:(2 * dr + 2) * LANES] = (b_re * e_im + b_im * e_re).astype(BF16)
        zero = jnp.zeros((S5_GROUP, GROUP_K), F32)
        lags = (jnp.concatenate([lag_part[1] + skip_ref[a2], zero], axis=1)
                + pltpu.roll(jnp.concatenate([lag_part[0], zero], axis=1), (CHUNK - 1) * S5_GROUP, 1))
        for t in range(CHUNK):
            shift = (CHUNK - 1 - t) * S5_GROUP
            window = lags if shift == 0 else pltpu.roll(lags, wide - shift, 1)
            wt_ref[a2, t * S5_GROUP:(t + 1) * S5_GROUP, :] = window[:, :GROUP_K].astype(BF16)


def _s5_pack(ct, bbt, bb_own, lag_pw, out_pw, in_pw, skip, rep_h, rep_k):
    n_pairs = ct.shape[0]
    per_pair = lambda a: pl.BlockSpec((None,) + a.shape[1:], lambda i: (i,) + (0,) * (a.ndim - 1))
    tabs = (ct, bbt, bb_own, lag_pw, out_pw, in_pw, skip)
    return pl.pallas_call(
        _s5_pack_kernel,
        grid=(n_pairs,),
        in_specs=[per_pair(a) for a in tabs] + [_const_spec(rep_h.shape), _const_spec(rep_k.shape)],
        out_specs=[pl.BlockSpec((2, GROUP_K, GROUP_K), lambda i: (i, 0, 0)),
                   pl.BlockSpec((None, 2 * GROUP_K, 4 * LANES), lambda i: (i, 0, 0)),
                   pl.BlockSpec((None, 4 * LANES, 2 * GROUP_K), lambda i: (i, 0, 0))],
        out_shape=[jax.ShapeDtypeStruct((2 * n_pairs, GROUP_K, GROUP_K), BF16),
                   jax.ShapeDtypeStruct((n_pairs, 2 * GROUP_K, 4 * LANES), BF16),
                   jax.ShapeDtypeStruct((n_pairs, 4 * LANES, 2 * GROUP_K), BF16)],
        compiler_params=_params(("parallel",)),
        name="s5_pack",
    )(*tabs, rep_h, rep_k)


def _attn_kernel(sink_ref, q_ref, kl_ref, kc_ref, kr_ref, bias_first_ref, bias_mid_ref, bias_last_ref, o_ref):
    nk = 3 * BLOCK
    kv_all = jnp.concatenate([kl_ref[...], kc_ref[...], kr_ref[...]], axis=0)
    low = lax.broadcasted_iota(jnp.int32, (kv_all.shape[0], LANES), 1) < HEAD_DIM
    zero = jnp.zeros((kv_all.shape[0], LANES), BF16)

    def diag_parts(tile, kvh):
        other = pltpu.roll(tile, HEAD_DIM, 1)
        first, second = (tile, other) if kvh == 0 else (other, tile)
        return jnp.where(low, first, zero), jnp.where(low, zero, second)

    def window(parts, qb):
        return jnp.concatenate([p[qb * BLOCK:qb * BLOCK + nk] for p in parts], axis=0)

    top = lax.broadcasted_iota(jnp.int32, (2 * nk, LANES), 0) < nk
    ones_bd = (top == (lax.broadcasted_iota(jnp.int32, (2 * nk, LANES), 1) < HEAD_DIM)).astype(BF16)

    upper = lax.broadcasted_iota(jnp.int32, (2 * BLOCK, 1), 0) < BLOCK
    bias_refs = [bias_first_ref] + [bias_mid_ref] * (ATT_QB - 2) + [bias_last_ref]
    for kvh in range(N_KV_HEADS):
        k_parts = diag_parts(kv_all[:, :LANES], kvh)
        v_parts = diag_parts(kv_all[:, LANES:], kvh)
        for qb, bias_ref in enumerate(bias_refs):
            rows = slice(qb * BLOCK, (qb + 1) * BLOCK)
            kbd = window(k_parts, qb)
            vbd = jnp.concatenate([window(v_parts, qb), ones_bd], axis=1)
            tiles = [(kvh * 2 + pair) * LANES for pair in range(2)]
            q2 = jnp.concatenate([q_ref[rows, tl:tl + LANES] for tl in tiles], axis=0)
            s = lax.dot_general(q2, kbd, (((1,), (1,)), ((), ())), preferred_element_type=F32)
            s = s + bias_ref[kvh]
            ps, sinks = [], []
            for e in range(2):
                se = s[:, e * nk:(e + 1) * nk]
                sk = jnp.where(upper, sink_ref[4 * kvh + e], sink_ref[4 * kvh + 2 + e])
                mx = jnp.maximum(jnp.max(se, axis=-1, keepdims=True), sk)
                ps.append(jnp.exp2((se - mx).astype(BF16)))
                sinks.append(jnp.broadcast_to(jnp.exp2(sk - mx), (2 * BLOCK, HEAD_DIM)))
            nd = _dot(jnp.concatenate(ps, axis=1), vbd)
            o = (nd[:, :LANES] / (nd[:, LANES:] + jnp.concatenate(sinks, axis=1))).astype(BF16)
            for pair, tl in enumerate(tiles):
                o_ref[rows, tl:tl + LANES] = o[pair * BLOCK:(pair + 1) * BLOCK, :]


def _attention(q, kv, sink, bias, bsz, seq):
    nb = seq // BLOCK
    assert nb % ATT_QB == 0 and ATT_QB >= 2
    nj = nb // ATT_QB
    rows = ATT_QB * BLOCK
    centre = lambda b, j: (b * nj + j, 0)
    left = lambda b, j: (b * nb + jnp.maximum(ATT_QB * j - 1, 0), 0)
    right = lambda b, j: (b * nb + jnp.minimum(ATT_QB * (j + 1), nb - 1), 0)
    variant = (None,) + bias.shape[1:]
    bias_first = pl.BlockSpec(variant, lambda b, j: (jnp.where(j == 0, 0, 1), 0, 0, 0))
    bias_mid = pl.BlockSpec(variant, lambda b, j: (1, 0, 0, 0), pipeline_mode=pl.Buffered(1))
    bias_last = pl.BlockSpec(variant, lambda b, j: (jnp.where(j == nj - 1, 2, 1), 0, 0, 0))
    return pl.pallas_call(
        _attn_kernel,
        grid=(bsz, nj),
        in_specs=[pl.BlockSpec(memory_space=pltpu.SMEM),
                  pl.BlockSpec((rows, ATT_WIDTH), centre),
                  pl.BlockSpec((BLOCK, 2 * KV_WIDTH), left),
                  pl.BlockSpec((rows, 2 * KV_WIDTH), centre),
                  pl.BlockSpec((BLOCK, 2 * KV_WIDTH), right),
                  bias_first, bias_mid, bias_last],
        out_specs=pl.BlockSpec((rows, ATT_WIDTH), centre),
        out_shape=jax.ShapeDtypeStruct((bsz * seq, ATT_WIDTH), BF16),
        compiler_params=_params(("parallel", "parallel")),
        name="attention",
    )(sink, q, kv, kv, kv, bias, bias, bias)


def _t5_bucket(rel):
    half = NUM_BUCKETS // 2
    max_exact = half // 2
    ret = jnp.where(rel > 0, half, 0)
    n = jnp.abs(rel)
    nf = jnp.maximum(n, 1).astype(jnp.float32)
    large = max_exact + (jnp.log(nf / max_exact) / math.log(MAX_DISTANCE / max_exact)
                         * (half - max_exact)).astype(jnp.int32)
    large = jnp.minimum(large, half - 1)
    return ret + jnp.where(n < max_exact, n, large)


def _band_bias(rel_bias):
    q_loc = jnp.arange(BLOCK, dtype=jnp.int32)
    k_loc = jnp.arange(3 * BLOCK, dtype=jnp.int32)
    rel = (k_loc[None, :] - BLOCK) - q_loc[:, None]
    onehot = (_t5_bucket(rel)[None] == jnp.arange(NUM_BUCKETS, dtype=jnp.int32)[:, None, None]).astype(F32)
    bias = jnp.einsum('bh,bqk->hqk', rel_bias.astype(F32), onehot, precision=lax.Precision.HIGHEST)
    bias = jnp.where((jnp.abs(rel) <= WINDOW)[None], bias * LOG2E, NEG_INF)
    keep = jnp.stack([k_loc >= BLOCK, k_loc >= 0, k_loc < 2 * BLOCK], axis=0)
    bias = jnp.where(keep[:, None, None, :], bias[None], NEG_INF)
    bias = bias.reshape(3, N_Q_HEADS // 2, 2, BLOCK, 3 * BLOCK).transpose(0, 1, 3, 2, 4)
    return bias.reshape(3, N_KV_HEADS, 2 * BLOCK, 6 * BLOCK)


def _merge_ffn_body(h_ref, y_ref, yb_ref, g1_ref, g2_ref, gf_ref, unperm_ref, wg_ref, wglu_ref, wa_ref, wb_ref,
                    wo_ref, fg_ref, fu_ref, fd_ref, o_ref, acc_ref, *, nxt, final_norm):
    bsz = h_ref.shape[0]
    h = _tile_rows(h_ref)
    hn = _rms(h, g1_ref[...]).astype(BF16)
    gates = _sigmoid(_dot(hn, wg_ref[...]))
    y = jnp.concatenate([_dot(unperm_ref[...], y_ref[rows, :]) for rows, _ in _layout_tiles()], axis=0)
    z = _gelu_tanh(y)
    za = (z * _sigmoid(_dot(z.astype(BF16), wglu_ref[...]))).astype(BF16)
    yb = _tile_rows(yb_ref)
    merged = gates[:, :D_MODEL] * _dot(za, wa_ref[...]) + gates[:, D_MODEL:] * _dot(yb, wb_ref[...])
    h = h + _dot(merged.astype(BF16), wo_ref[...])

    hn = _rms(h, g2_ref[...]).astype(BF16)
    acc_ref[...] = h
    for j in range(D_FF // FF_TILE):
        cols = slice(j * FF_TILE, (j + 1) * FF_TILE)
        gate = _dot(hn, fg_ref[:, cols])
        up = _dot(hn, fu_ref[:, cols])
        act = (gate * _sigmoid(gate) * up).astype(BF16)
        acc_ref[...] += _dot(act, fd_ref[cols, :])
    out = acc_ref[...]
    if final_norm:
        out = _rms(out, gf_ref[...])
    else:
        gn_ref, wn_ref, perm_ref, u_ref, q_ref, kv_ref = nxt
        _inproj_rows(out, gn_ref, wn_ref, perm_ref, u_ref, q_ref, kv_ref)
    for rows, tok in _layout_tiles():
        o_ref[:, tok, :] = out[rows, :].reshape(bsz, LAYOUT_TOK, D_MODEL)


def _merge_ffn_last_kernel(*refs):
    *ins, o_ref, acc_ref = refs
    _merge_ffn_body(*ins, o_ref, acc_ref, nxt=None, final_norm=True)


def _merge_ffn_next_kernel(*refs):
    *ins, gn_ref, wn_ref, perm_ref, o_ref, u_ref, q_ref, kv_ref, acc_ref = refs
    _merge_ffn_body(*ins, o_ref, acc_ref, nxt=(gn_ref, wn_ref, perm_ref, u_ref, q_ref, kv_ref), final_norm=False)


def _merge_ffn(h3, y, yb3, g1, g2, gf, unperm, weights, layer, next_inproj=None):
    bsz, seq, _ = h3.shape
    vec = _const_spec((1, D_MODEL))
    in_specs = [_tok_spec(bsz, D_MODEL), pl.BlockSpec((TILE_ROWS, S5_WIDTH), lambda i: (i, 0)),
                _tok_spec(bsz, ATT_WIDTH), vec, vec, vec, _const_spec(unperm.shape)]
    in_specs += [_layer_spec(w, layer) for w in weights]
    args = [h3, y, yb3, g1, g2, gf, unperm, *weights]
    out_specs, out_shape = [_tok_spec(bsz, D_MODEL)], [jax.ShapeDtypeStruct((bsz, seq, D_MODEL), F32)]
    if next_inproj is None:
        body = _merge_ffn_last_kernel
    else:
        body = _merge_ffn_next_kernel
        gain_next, w_next, perm = next_inproj
        in_specs += [vec, _layer_spec(w_next, layer + 1), _const_spec(perm.shape)]
        args += [gain_next, w_next, perm]
        specs, shapes = _inproj_out(bsz, seq)
        out_specs += specs
        out_shape += shapes
    outs = pl.pallas_call(
        body,
        grid=(seq // TOK_TILE,),
        in_specs=in_specs,
        out_specs=out_specs,
        out_shape=out_shape,
        scratch_shapes=[pltpu.VMEM((TILE_ROWS, D_MODEL), F32)],
        compiler_params=_params(("parallel",)),
        name="merge_ffn",
    )(*args)
    return outs[0] if next_inproj is None else outs


def kernel(x, norm1_g, norm2_g, final_g, w_in, s5_lambda_re, s5_lambda_im, s5_log_dt, s5_b_re, s5_b_im,
           s5_c_re, s5_c_im, s5_d, s5_w_glu, attn_sink, rel_bias, w_branch_a, w_branch_b, w_out,
           ffn_w_gate, ffn_w_up, ffn_w_down):
    bsz, seq, _ = x.shape
    depth = w_in.shape[0]
    assert seq % BLOCK == 0 and seq % CHUNK == 0 and bsz == SUBLANES
    n_chunks = seq // CHUNK
    t = bsz * seq
    bias = _band_bias(rel_bias)
    o_k = S5_WIDTH + ATT_WIDTH
    o_g = o_k + 2 * KV_WIDTH
    col_scale = jnp.concatenate([jnp.ones((S5_WIDTH,), F32), jnp.full((ATT_WIDTH,), LOG2E * HEAD_DIM ** -0.5, F32),
                                 jnp.ones((2 * KV_WIDTH,), F32)])
    gf = final_g.reshape(1, D_MODEL).astype(F32)
    perm = _to_chunk_order()
    unperm = perm.T
    w_in_state, w_toep, w_out_state, a16 = _s5_weights(
        s5_lambda_re, s5_lambda_im, s5_log_dt, s5_b_re, s5_b_im, s5_c_re, s5_c_im, s5_d)
    gain1 = lambda layer: norm1_g[layer].reshape(1, D_MODEL).astype(F32)
    w_uqkv = (w_in[:, :, :o_g] * col_scale).astype(BF16)
    weights = [w.astype(BF16) for w in (w_in[:, :, o_g:], s5_w_glu, w_branch_a, w_branch_b, w_out,
                                        ffn_w_gate, ffn_w_up, ffn_w_down)]
    h = x
    u, q, kv = _inproj(h, gain1(0), w_uqkv, perm, 0)
    for layer in range(depth):
        g1 = gain1(layer)
        g2 = norm2_g[layer].reshape(1, D_MODEL).astype(F32)
        sin_f, sin_b, grouped = _s5_state_in(u, w_in_state, layer)
        carry_f, carry_b = _s5_scan(sin_f, sin_b, a16[layer], n_chunks)
        y = _s5_out(grouped, carry_f, carry_b, w_toep, w_out_state, layer)
        yb = _attention(q.reshape(t, ATT_WIDTH), kv.reshape(t, 2 * KV_WIDTH),
                        attn_sink[layer].astype(F32) * LOG2E, bias, bsz, seq)
        yb3 = yb.reshape(bsz, seq, ATT_WIDTH)
        if layer == depth - 1:
            h = _merge_ffn(h, y, yb3, g1, g2, gf, unperm, weights, layer)
        else:
            h, u, q, kv = _merge_ffn(h, y, yb3, g1, g2, gf, unperm, weights, layer,
                                     next_inproj=(gain1(layer + 1), w_uqkv, perm))
    return h
```

```python
import functools
import math

import jax
import jax.numpy as jnp
from jax import lax
from jax.experimental import pallas as pl
from jax.experimental.pallas import tpu as pltpu

F32 = jnp.float32
BF16 = jnp.bfloat16

D_MODEL = 1024
S5_WIDTH = 512
S5_GROUP = 16
S5_GROUPS = 32
S5_STATE = 64
HEAD_DIM = 64
N_Q_HEADS = 8
N_KV_HEADS = 2
Q_PER_KV = N_Q_HEADS // N_KV_HEADS
ATT_WIDTH = N_Q_HEADS * HEAD_DIM
KV_WIDTH = N_KV_HEADS * HEAD_DIM
WINDOW = 128
BLOCK = 128
NUM_BUCKETS = 32
MAX_DISTANCE = 128
D_FF = 2816
RMS_EPS = 1e-6
NEG_INF = -1e30
LOG2E = math.log2(math.e)

LANES = 128
SUBLANES = 8
CHUNK = 16
GROUPS_PER_SLAB = LANES // S5_GROUP
N_SLABS = S5_WIDTH // LANES
PAIRS_PER_SLAB = GROUPS_PER_SLAB // 2
GROUP_K = CHUNK * S5_GROUP
DIR_COLS = 2 * GROUPS_PER_SLAB * S5_STATE
TOK_TILE = 64
TILE_ROWS = SUBLANES * TOK_TILE
INPROJ_TOK = 2 * TOK_TILE
LAYOUT_TOK = 32
LAYOUT_ROWS = SUBLANES * LAYOUT_TOK
S5_ROWS = 32 * TILE_ROWS
FF_TILE = 256
ATT_QB = 16
VMEM_LIMIT = 56 * 1024 * 1024


def _rms(x, g):
    return x * lax.rsqrt(jnp.mean(x * x, axis=-1, keepdims=True) + RMS_EPS) * g


def _gelu_tanh(x):
    return 0.5 * x * (1.0 + jnp.tanh(math.sqrt(2.0 / math.pi) * (x + 0.044715 * (x * x * x))))


def _sigmoid(x):
    return 1.0 / (1.0 + jnp.exp(-x))


def _dot(a, b):
    return jnp.dot(a, b, preferred_element_type=F32)


def _const_spec(shape):
    nd = len(shape)
    return pl.BlockSpec(shape, lambda *_: (0,) * nd, pipeline_mode=pl.Buffered(1))


def _layer_spec(stacked, layer):
    tail = (0,) * (stacked.ndim - 1)
    return pl.BlockSpec((None,) + stacked.shape[1:], lambda *_: (layer,) + tail, pipeline_mode=pl.Buffered(1))


def _params(sem):
    return pltpu.CompilerParams(dimension_semantics=sem, vmem_limit_bytes=VMEM_LIMIT)


def _to_chunk_order():
    cpt = LAYOUT_TOK // CHUNK
    dst = jnp.arange(LAYOUT_ROWS)
    t, c, b = dst // (cpt * SUBLANES), (dst // SUBLANES) % cpt, dst % SUBLANES
    src = b * LAYOUT_TOK + c * CHUNK + t
    return (src[:, None] == jnp.arange(LAYOUT_ROWS)[None, :]).astype(BF16)


def _layout_tiles():
    return [(slice(s * LAYOUT_ROWS, (s + 1) * LAYOUT_ROWS), slice(s * LAYOUT_TOK, (s + 1) * LAYOUT_TOK))
            for s in range(TOK_TILE // LAYOUT_TOK)]


def _tile_rows(ref):
    return jnp.concatenate([ref[:, tok, :].reshape(LAYOUT_ROWS, ref.shape[2]) for _, tok in _layout_tiles()], axis=0)


def _inproj_rows(h, g_ref, w_ref, perm_ref, u_ref, q_ref, kv_ref):
    bsz = q_ref.shape[0]
    hn = _rms(h, g_ref[...]).astype(BF16)
    r = _dot(hn, w_ref[...])
    for rows, tok in _layout_tiles():
        u_ref[rows, :] = _dot(perm_ref[...], r[rows, :S5_WIDTH].astype(BF16)).astype(BF16)
        q_ref[:, tok, :] = r[rows, S5_WIDTH:S5_WIDTH + ATT_WIDTH].astype(BF16).reshape(bsz, LAYOUT_TOK, ATT_WIDTH)
        kv_ref[:, tok, :] = r[rows, S5_WIDTH + ATT_WIDTH:].astype(BF16).reshape(bsz, LAYOUT_TOK, 2 * KV_WIDTH)


def _inproj_kernel(x_ref, g_ref, w_ref, perm_ref, u_ref, q_ref, kv_ref):
    for part in range(x_ref.shape[1] // TOK_TILE):
        tok = pl.ds(part * TOK_TILE, TOK_TILE)
        _inproj_rows(_tile_rows(x_ref.at[:, tok]), g_ref, w_ref, perm_ref,
                     u_ref.at[pl.ds(part * TILE_ROWS, TILE_ROWS)], q_ref.at[:, tok], kv_ref.at[:, tok])


def _tok_spec(bsz, width, tok=TOK_TILE):
    return pl.BlockSpec((bsz, tok, width), lambda i: (0, i, 0))


def _inproj_out(bsz, seq, tok=TOK_TILE):
    specs = [pl.BlockSpec((bsz * tok, S5_WIDTH), lambda i: (i, 0)), _tok_spec(bsz, ATT_WIDTH, tok),
             _tok_spec(bsz, 2 * KV_WIDTH, tok)]
    shapes = [jax.ShapeDtypeStruct((bsz * seq, S5_WIDTH), BF16),
              jax.ShapeDtypeStruct((bsz, seq, ATT_WIDTH), BF16),
              jax.ShapeDtypeStruct((bsz, seq, 2 * KV_WIDTH), BF16)]
    return specs, shapes


def _inproj(h3, g, w_stacked, perm, layer):
    bsz, seq, _ = h3.shape
    assert bsz * TOK_TILE == TILE_ROWS and seq % INPROJ_TOK == 0
    specs, shapes = _inproj_out(bsz, seq, INPROJ_TOK)
    return pl.pallas_call(
        _inproj_kernel,
        grid=(seq // INPROJ_TOK,),
        in_specs=[_tok_spec(bsz, D_MODEL, INPROJ_TOK), _const_spec((1, D_MODEL)), _layer_spec(w_stacked, layer),
                  _const_spec(perm.shape)],
        out_specs=specs,
        out_shape=shapes,
        compiler_params=_params(("parallel",)),
        name="inproj",
    )(h3, g, w_stacked, perm)


def _block_transpose(x):
    x = list(x)
    blk = lax.broadcasted_iota(jnp.int32, x[0].shape, 1) // S5_GROUP
    for d in (4, 2, 1):
        keep = (blk & d) == 0
        for i in range(GROUPS_PER_SLAB):
            if i & d:
                continue
            xi, xj = x[i], x[i + d]
            x[i] = jnp.where(keep, xi, pltpu.roll(xj, d * S5_GROUP, 1))
            x[i + d] = jnp.where(keep, pltpu.roll(xi, LANES - d * S5_GROUP, 1), xj)
    return x


def _token_rows(t, k):
    return pl.ds(k * LAYOUT_ROWS + t * (LAYOUT_ROWS // CHUNK), LAYOUT_ROWS // CHUNK)


def _group_rows(u_ref):
    tiles = u_ref.shape[0] // LAYOUT_ROWS
    tok = [jnp.concatenate([u_ref[_token_rows(t, k), :] for k in range(tiles)], axis=0) for t in range(CHUNK)]
    lo = _block_transpose(tok[:CHUNK // 2])
    hi = _block_transpose(tok[CHUNK // 2:])
    return [jnp.concatenate([lo[a], hi[a]], axis=1) for a in range(GROUPS_PER_SLAB)]


def _state_tile(ri, pq):
    return pl.ds(ri * (DIR_COLS // 2) + pq * LANES, LANES)


def _s5_in_kernel(u_ref, w_ref, of_ref, ob_ref, g_ref):
    g = _group_rows(u_ref)
    for a in range(GROUPS_PER_SLAB):
        g_ref[:, a * GROUP_K:(a + 1) * GROUP_K] = g[a]
    for pq in range(PAIRS_PER_SLAB):
        r = _dot(jnp.concatenate([g[2 * pq], g[2 * pq + 1]], axis=1), w_ref[pq])
        for dr, o_ref in enumerate((of_ref, ob_ref)):
            for ri in range(2):
                k = dr * 2 + ri
                o_ref[:, _state_tile(ri, pq)] = r[:, k * LANES:(k + 1) * LANES].astype(BF16)


def _s5_state_in(u, w_in_state, layer, rows=S5_ROWS):
    t = u.shape[0]
    state = jax.ShapeDtypeStruct((t // CHUNK, N_SLABS * DIR_COLS), BF16)
    spec = pl.BlockSpec((rows // CHUNK, DIR_COLS), lambda s, i: (i, s))
    return pl.pallas_call(
        _s5_in_kernel,
        grid=(N_SLABS, t // rows),
        in_specs=[pl.BlockSpec((rows, LANES), lambda s, i: (i, s)),
                  pl.BlockSpec((None,) + w_in_state.shape[1:], lambda s, i: (layer * N_SLABS + s, 0, 0, 0))],
        out_specs=[spec, spec, pl.BlockSpec((rows // CHUNK, CHUNK * LANES), lambda s, i: (i, s))],
        out_shape=[state, state, jax.ShapeDtypeStruct((t // CHUNK, CHUNK * S5_WIDTH), BF16)],
        compiler_params=_params(("parallel", "parallel")),
        name="s5_state_in",
    )(u, w_in_state)


def _s5_scan_kernel(xf_ref, xb_ref, af_ref, ab_ref, of_ref, ob_ref, st_ref, *, cpt):
    half = DIR_COLS // 2
    pair_rows = 2 * SUBLANES
    units = xf_ref.shape[1] // DIR_COLS
    lower, upper = slice(0, SUBLANES), slice(SUBLANES, pair_rows)

    @pl.when(pl.program_id(1) == 0)
    def _():
        st_ref[...] = jnp.zeros_like(st_ref)

    def two_chunks(x_ref, a_ref, o_ref, c2, state, reverse):
        rows = pl.ds(pl.multiple_of(c2 * pair_rows, pair_rows), pair_rows)
        first, second = (upper, lower) if reverse else (lower, upper)
        new_state = []
        for un in range(units):
            re = slice(un * DIR_COLS, un * DIR_COLS + half)
            im = slice(un * DIR_COLS + half, (un + 1) * DIR_COLS)
            ar, ai = a_ref[:, re], a_ref[:, im]
            xr = x_ref[rows, re].astype(F32)
            xi = x_ref[rows, im].astype(F32)
            s0r, s0i = state[un]
            s1r, s1i = ar * s0r - ai * s0i + xr[first], ar * s0i + ai * s0r + xi[first]
            s2r, s2i = ar * s1r - ai * s1i + xr[second], ar * s1i + ai * s1r + xi[second]
            enter_r, enter_i = ((s1r, s0r), (s1i, s0i)) if reverse else ((s0r, s1r), (s0i, s1i))
            o_ref[rows, re] = jnp.concatenate(enter_r, axis=0).astype(BF16)
            o_ref[rows, im] = jnp.concatenate(enter_i, axis=0).astype(BF16)
            new_state.append((s2r, s2i))
        return new_state

    def body(i, state):
        fwd, bwd = state
        return (two_chunks(xf_ref, af_ref, of_ref, i, fwd, False),
                two_chunks(xb_ref, ab_ref, ob_ref, cpt // 2 - 1 - i, bwd, True))

    def load(dr):
        return [(st_ref[dr, :, pl.ds(un * DIR_COLS, half)], st_ref[dr, :, pl.ds(un * DIR_COLS + half, half)])
                for un in range(units)]

    fwd, bwd = lax.fori_loop(0, cpt // 2, body, (load(0), load(1)))
    for dr, state in enumerate((fwd, bwd)):
        for un, (sr, si) in enumerate(state):
            st_ref[dr, :, pl.ds(un * DIR_COLS, half)] = sr
            st_ref[dr, :, pl.ds(un * DIR_COLS + half, half)] = si


def _s5_scan(sin_f, sin_b, a16, n_chunks, cpt=64, width=2 * DIR_COLS):
    nct = n_chunks // cpt
    rows = cpt * SUBLANES
    up = pl.BlockSpec((rows, width), lambda c, k: (k, c))
    down = pl.BlockSpec((rows, width), lambda c, k: (nct - 1 - k, c))
    coef = lambda dr: pl.BlockSpec((None, SUBLANES, width), lambda c, k: (dr, 0, c))
    return pl.pallas_call(
        functools.partial(_s5_scan_kernel, cpt=cpt),
        grid=(sin_f.shape[1] // width, nct),
        in_specs=[up, down, coef(0), coef(1)],
        out_specs=[up, down],
        out_shape=[jax.ShapeDtypeStruct(sin_f.shape, BF16), jax.ShapeDtypeStruct(sin_b.shape, BF16)],
        scratch_shapes=[pltpu.VMEM((2, SUBLANES, width), F32)],
        compiler_params=_params(("parallel", "arbitrary")),
        name="s5_scan",
    )(sin_f, sin_b, a16, a16)


def _s5_out_kernel(g_ref, cf_ref, cb_ref, wt_ref, wo_ref, y_ref):
    g = [g_ref[:, a * GROUP_K:(a + 1) * GROUP_K] for a in range(GROUPS_PER_SLAB)]
    ys = []
    for pq in range(PAIRS_PER_SLAB):
        carry = jnp.concatenate([c_ref[:, _state_tile(ri, pq)] for c_ref in (cf_ref, cb_ref) for ri in range(2)],
                                axis=1)
        from_state = _dot(carry, wo_ref[pq])
        for a2 in range(2):
            a = 2 * pq + a2
            ys.append((_dot(g[a], wt_ref[a]) + from_state[:, a2 * GROUP_K:(a2 + 1) * GROUP_K]).astype(BF16))
    halves = (_block_transpose([y[:, :LANES] for y in ys]), _block_transpose([y[:, LANES:] for y in ys]))
    per_tile = LAYOUT_ROWS // CHUNK
    for t in range(CHUNK):
        tok = halves[t // (CHUNK // 2)][t % (CHUNK // 2)]
        for k in range(y_ref.shape[0] // LAYOUT_ROWS):
            y_ref[_token_rows(t, k), :] = tok[k * per_tile:(k + 1) * per_tile, :]


def _s5_out(grouped, carry_f, carry_b, w_toep, w_out_state, layer, rows=S5_ROWS):
    t = grouped.shape[0] * CHUNK
    state = pl.BlockSpec((rows // CHUNK, DIR_COLS), lambda s, i: (i, s))
    return pl.pallas_call(
        _s5_out_kernel,
        grid=(N_SLABS, t // rows),
        in_specs=[pl.BlockSpec((rows // CHUNK, CHUNK * LANES), lambda s, i: (i, s)), state, state,
                  pl.BlockSpec((None,) + w_toep.shape[1:], lambda s, i: (layer * N_SLABS + s, 0, 0, 0)),
                  pl.BlockSpec((None,) + w_out_state.shape[1:], lambda s, i: (layer * N_SLABS + s, 0, 0, 0))],
        out_specs=pl.BlockSpec((rows, LANES), lambda s, i: (i, s)),
        out_shape=jax.ShapeDtypeStruct((t, S5_WIDTH), BF16),
        compiler_params=_params(("parallel", "parallel")),
        name="s5_out",
    )(grouped, carry_f, carry_b, w_toep, w_out_state)


def _s5_tables(lam_re, lam_im, log_dt, b_re, b_im, c_re, c_im, d):
    G, P, H, C, S = S5_GROUPS, S5_STATE, S5_GROUP, CHUNK, GROUPS_PER_SLAB
    lr = lam_re.astype(F32)
    li = lam_im.astype(F32)
    dt = jnp.exp(log_dt.astype(F32))[..., None]
    mag = jnp.exp(lr * dt)
    ab_re = mag * jnp.cos(li * dt)
    ab_im = mag * jnp.sin(li * dt)
    nr = ab_re - 1.0
    den = lr * lr + li * li
    coef_re = (nr * lr + ab_im * li) / den
    coef_im = (ab_im * lr - nr * li) / den
    br = b_re.astype(F32)
    bi = b_im.astype(F32)
    bb_re = coef_re[..., None] * br - coef_im[..., None] * bi
    bb_im = coef_re[..., None] * bi + coef_im[..., None] * br
    cr = c_re.astype(F32)
    ci = c_im.astype(F32)

    k = jnp.arange(C + 1, dtype=F32)[:, None, None, None]
    pmag = jnp.exp(k * (lr * dt))
    pw_re = pmag * jnp.cos(k * (li * dt))
    pw_im = pmag * jnp.sin(k * (li * dt))

    n_pairs = G // 2
    pw = jnp.stack([pw_re, pw_im], axis=0)
    up = jnp.arange(C)

    def table(k_fwd, k_bwd):
        return jnp.stack([pw[:, k_fwd, 0], pw[:, k_bwd, 1]], axis=0).transpose(3, 0, 1, 2, 4)

    by_state = lambda tab: tab.transpose(0, 1, 2, 4, 3).reshape(n_pairs, 2, 2, 2, P, C)
    lag_pw = by_state(table(up, C - 1 - up))
    out_pw = by_state(table(up + 1, C - up))
    in_pw = table(C - 1 - up, up).reshape(n_pairs, 2, 2, 2, C, P)
    in_pw = jnp.tile(in_pw, (1, 1, 1, 1, 1, 2))
    ct = jnp.stack([cr, ci], axis=0).transpose(2, 1, 0, 4, 3).reshape(n_pairs, 2, 2, 2, P, H)
    bbt = jnp.stack([bb_re, bb_im], axis=0).transpose(2, 1, 0, 4, 3).reshape(n_pairs, 2, 2, 2, H, P)
    own = jnp.eye(2, dtype=F32)[None, :, None, None, None, :, None]
    bb_own = (bbt[:, :, :, :, :, None, :] * own).reshape(n_pairs, 2, 2, 2, H, 2 * P)
    skip = jnp.eye(H, dtype=F32)[None] * d.astype(F32).reshape(G, 1, H)
    skip = jnp.pad(skip, ((0, 0), (0, 0), ((C - 1) * H, 0))).reshape(n_pairs, 2, H, GROUP_K)

    a16 = jnp.stack([pw_re[C], pw_im[C]], axis=0)
    a16 = a16.reshape(2, 2, N_SLABS, S, P).transpose(1, 2, 0, 3, 4)
    a16 = jnp.broadcast_to(a16.reshape(2, 1, N_SLABS * DIR_COLS), (2, SUBLANES, N_SLABS * DIR_COLS))
    return (ct, bbt, bb_own, lag_pw, out_pw, in_pw, skip), a16


def _s5_weights(*stacked_params):
    depth = stacked_params[0].shape[0]
    tabs, a16 = jax.vmap(_s5_tables)(*stacked_params)
    tabs = [a.reshape((depth * a.shape[1],) + a.shape[2:]) for a in tabs]
    rep_h = jnp.tile(jnp.eye(S5_GROUP, dtype=BF16), (1, CHUNK))
    rep_k = jnp.repeat(jnp.eye(CHUNK, dtype=BF16), S5_GROUP, axis=1)
    wt, wi, wo = _s5_pack(*tabs, rep_h, rep_k)
    wt = wt.reshape(depth * N_SLABS, GROUPS_PER_SLAB, GROUP_K, GROUP_K)
    wi = wi.reshape(depth * N_SLABS, PAIRS_PER_SLAB, 2 * GROUP_K, 4 * LANES)
    wo = wo.reshape(depth * N_SLABS, PAIRS_PER_SLAB, 4 * LANES, 2 * GROUP_K)
    return wi, wt, wo, a16


def _s5_pack_kernel(ct_ref, bbt_ref, bbo_ref, lagp_ref, outp_ref, inp_ref, skip_ref, reph_ref, repk_ref,
                    wt_ref, wi_ref, wo_ref):
    def split(x):
        hi = x.astype(BF16)
        return hi, (x - hi.astype(F32)).astype(BF16)

    def exact_dot(a, b):
        a_hi, a_lo = split(a)
        if b.dtype == BF16:
            return _dot(a_hi, b) + _dot(a_lo, b)
        b_hi, b_lo = split(b)
        return _dot(a_hi, b_hi) + (_dot(a_hi, b_lo) + _dot(a_lo, b_hi))

    rep_h = reph_ref[...]
    rep_k = repk_ref[...]
    wide = 2 * GROUP_K
    for a2 in range(2):
        lag_part = []
        for dr in range(2):
            c_re = exact_dot(ct_ref[a2, dr, 0], rep_h)
            c_im = exact_dot(ct_ref[a2, dr, 1], rep_h)

            def times_power(ref):
                p_re = exact_dot(ref[a2, dr, 0], rep_k)
                p_im = exact_dot(ref[a2, dr, 1], rep_k)
                return c_re * p_re - c_im * p_im, c_re * p_im + c_im * p_re

            l_re, l_im = times_power(lagp_ref)
            lag_part.append(exact_dot(bbt_ref[a2, dr, 0], l_re) - exact_dot(bbt_ref[a2, dr, 1], l_im))
            w_re, w_im = times_power(outp_ref)
            for ri, val in ((0, w_re), (1, -w_im)):
                rows = pl.ds((dr * 2 + ri) * LANES + a2 * S5_STATE, S5_STATE)
                wo_ref[rows, a2 * GROUP_K:(a2 + 1) * GROUP_K] = val.astype(BF16)
                wo_ref[rows, (1 - a2) * GROUP_K:(2 - a2) * GROUP_K] = jnp.zeros((S5_STATE, GROUP_K), BF16)
            b_re = bbo_ref[a2, dr, 0]
            b_im = bbo_ref[a2, dr, 1]
            for t in range(CHUNK):
                e_re = inp_ref[a2, dr, 0, t:t + 1, :]
                e_im = inp_ref[a2, dr, 1, t:t + 1, :]
                rows = pl.ds(a2 * GROUP_K + t * S5_GROUP, S5_GROUP)
                wi_ref[rows, (2 * dr) * LANES:(2 * dr + 1) * LANES] = (b_re * e_re - b_im * e_im).astype(BF16)
                wi_ref[rows, (2 * dr + 1) * LANES:(2 * dr + 2) * LANES] = (b_re * e_im + b_im * e_re).astype(BF16)
        zero = jnp.zeros((S5_GROUP, GROUP_K), F32)
        lags = (jnp.concatenate([lag_part[1] + skip_ref[a2], zero], axis=1)
                + pltpu.roll(jnp.concatenate([lag_part[0], zero], axis=1), (CHUNK - 1) * S5_GROUP, 1))
        for t in range(CHUNK):
            shift = (CHUNK - 1 - t) * S5_GROUP
            window = lags if shift == 0 else pltpu.roll(lags, wide - shift, 1)
            wt_ref[a2, t * S5_GROUP:(t + 1) * S5_GROUP, :] = window[:, :GROUP_K].astype(BF16)


def _s5_pack(ct, bbt, bb_own, lag_pw, out_pw, in_pw, skip, rep_h, rep_k):
    n_pairs = ct.shape[0]
    per_pair = lambda a: pl.BlockSpec((None,) + a.shape[1:], lambda i: (i,) + (0,) * (a.ndim - 1))
    tabs = (ct, bbt, bb_own, lag_pw, out_pw, in_pw, skip)
    return pl.pallas_call(
        _s5_pack_kernel,
        grid=(n_pairs,),
        in_specs=[per_pair(a) for a in tabs] + [_const_spec(rep_h.shape), _const_spec(rep_k.shape)],
        out_specs=[pl.BlockSpec((2, GROUP_K, GROUP_K), lambda i: (i, 0, 0)),
                   pl.BlockSpec((None, 2 * GROUP_K, 4 * LANES), lambda i: (i, 0, 0)),
                   pl.BlockSpec((None, 4 * LANES, 2 * GROUP_K), lambda i: (i, 0, 0))],
        out_shape=[jax.ShapeDtypeStruct((2 * n_pairs, GROUP_K, GROUP_K), BF16),
                   jax.ShapeDtypeStruct((n_pairs, 2 * GROUP_K, 4 * LANES), BF16),
                   jax.ShapeDtypeStruct((n_pairs, 4 * LANES, 2 * GROUP_K), BF16)],
        compiler_params=_params(("parallel",)),
        name="s5_pack",
    )(*tabs, rep_h, rep_k)


def _attn_kernel(sink_ref, q_ref, kl_ref, kc_ref, kr_ref, bias_first_ref, bias_mid_ref, bias_last_ref, o_ref):
    nk = 3 * BLOCK
    kv_all = jnp.concatenate([kl_ref[...], kc_ref[...], kr_ref[...]], axis=0)
    low = lax.broadcasted_iota(jnp.int32, (kv_all.shape[0], LANES), 1) < HEAD_DIM
    zero = jnp.zeros((kv_all.shape[0], LANES), BF16)

    def diag_parts(tile, kvh):
        other = pltpu.roll(tile, HEAD_DIM, 1)
        first, second = (tile, other) if kvh == 0 else (other, tile)
        return jnp.where(low, first, zero), jnp.where(low, zero, second)

    def window(parts, qb):
        return jnp.concatenate([p[qb * BLOCK:qb * BLOCK + nk] for p in parts], axis=0)

    top = lax.broadcasted_iota(jnp.int32, (2 * nk, LANES), 0) < nk
    ones_bd = (top == (lax.broadcasted_iota(jnp.int32, (2 * nk, LANES), 1) < HEAD_DIM)).astype(BF16)

    upper = lax.broadcasted_iota(jnp.int32, (2 * BLOCK, 1), 0) < BLOCK
    bias_refs = [bias_first_ref] + [bias_mid_ref] * (ATT_QB - 2) + [bias_last_ref]
    for kvh in range(N_KV_HEADS):
        k_parts = diag_parts(kv_all[:, :LANES], kvh)
        v_parts = diag_parts(kv_all[:, LANES:], kvh)
        for qb, bias_ref in enumerate(bias_refs):
            rows = slice(qb * BLOCK, (qb + 1) * BLOCK)
            kbd = window(k_parts, qb)
            vbd = jnp.concatenate([window(v_parts, qb), ones_bd], axis=1)
            tiles = [(kvh * 2 + pair) * LANES for pair in range(2)]
            q2 = jnp.concatenate([q_ref[rows, tl:tl + LANES] for tl in tiles], axis=0)
            s = lax.dot_general(q2, kbd, (((1,), (1,)), ((), ())), preferred_element_type=F32)
            s = s + bias_ref[kvh]
            ps, sinks = [], []
            for e in range(2):
                se = s[:, e * nk:(e + 1) * nk]
                sk = jnp.where(upper, sink_ref[4 * kvh + e], sink_ref[4 * kvh + 2 + e])
                mx = jnp.maximum(jnp.max(se, axis=-1, keepdims=True), sk)
                ps.append(jnp.exp2((se - mx).astype(BF16)))
                sinks.append(jnp.broadcast_to(jnp.exp2(sk - mx), (2 * BLOCK, HEAD_DIM)))
            nd = _dot(jnp.concatenate(ps, axis=1), vbd)
            o = (nd[:, :LANES] / (nd[:, LANES:] + jnp.concatenate(sinks, axis=1))).astype(BF16)
            for pair, tl in enumerate(tiles):
                o_ref[rows, tl:tl + LANES] = o[pair * BLOCK:(pair + 1) * BLOCK, :]


def _attention(q, kv, sink, bias, bsz, seq):
    nb = seq // BLOCK
    assert nb % ATT_QB == 0 and ATT_QB >= 2
    nj = nb // ATT_QB
    rows = ATT_QB * BLOCK
    centre = lambda b, j: (b * nj + j, 0)
    left = lambda b, j: (b * nb + jnp.maximum(ATT_QB * j - 1, 0), 0)
    right = lambda b, j: (b * nb + jnp.minimum(ATT_QB * (j + 1), nb - 1), 0)
    variant = (None,) + bias.shape[1:]
    bias_first = pl.BlockSpec(variant, lambda b, j: (jnp.where(j == 0, 0, 1), 0, 0, 0))
    bias_mid = pl.BlockSpec(variant, lambda b, j: (1, 0, 0, 0), pipeline_mode=pl.Buffered(1))
    bias_last = pl.BlockSpec(variant, lambda b, j: (jnp.where(j == nj - 1, 2, 1), 0, 0, 0))
    return pl.pallas_call(
        _attn_kernel,
        grid=(bsz, nj),
        in_specs=[pl.BlockSpec(memory_space=pltpu.SMEM),
                  pl.BlockSpec((rows, ATT_WIDTH), centre),
                  pl.BlockSpec((BLOCK, 2 * KV_WIDTH), left),
                  pl.BlockSpec((rows, 2 * KV_WIDTH), centre),
                  pl.BlockSpec((BLOCK, 2 * KV_WIDTH), right),
                  bias_first, bias_mid, bias_last],
        out_specs=pl.BlockSpec((rows, ATT_WIDTH), centre),
        out_shape=jax.ShapeDtypeStruct((bsz * seq, ATT_WIDTH), BF16),
        compiler_params=_params(("parallel", "parallel")),
        name="attention",
    )(sink, q, kv, kv, kv, bias, bias, bias)


def _t5_bucket(rel):
    half = NUM_BUCKETS // 2
    max_exact = half // 2
    ret = jnp.where(rel > 0, half, 0)
    n = jnp.abs(rel)
    nf = jnp.maximum(n, 1).astype(jnp.float32)
    large = max_exact + (jnp.log(nf / max_exact) / math.log(MAX_DISTANCE / max_exact)
                         * (half - max_exact)).astype(jnp.int32)
    large = jnp.minimum(large, half - 1)
    return ret + jnp.where(n < max_exact, n, large)


def _band_bias(rel_bias):
    q_loc = jnp.arange(BLOCK, dtype=jnp.int32)
    k_loc = jnp.arange(3 * BLOCK, dtype=jnp.int32)
    rel = (k_loc[None, :] - BLOCK) - q_loc[:, None]
    onehot = (_t5_bucket(rel)[None] == jnp.arange(NUM_BUCKETS, dtype=jnp.int32)[:, None, None]).astype(F32)
    bias = jnp.einsum('bh,bqk->hqk', rel_bias.astype(F32), onehot, precision=lax.Precision.HIGHEST)
    bias = jnp.where((jnp.abs(rel) <= WINDOW)[None], bias * LOG2E, NEG_INF)
    keep = jnp.stack([k_loc >= BLOCK, k_loc >= 0, k_loc < 2 * BLOCK], axis=0)
    bias = jnp.where(keep[:, None, None, :], bias[None], NEG_INF)
    bias = bias.reshape(3, N_Q_HEADS // 2, 2, BLOCK, 3 * BLOCK).transpose(0, 1, 3, 2, 4)
    return bias.reshape(3, N_KV_HEADS, 2 * BLOCK, 6 * BLOCK)


def _merge_ffn_body(h_ref, y_ref, yb_ref, g1_ref, g2_ref, gf_ref, unperm_ref, wg_ref, wglu_ref, wa_ref, wb_ref,
                    wo_ref, fg_ref, fu_ref, fd_ref, o_ref, acc_ref, *, nxt, final_norm):
    bsz = h_ref.shape[0]
    h = _tile_rows(h_ref)
    hn = _rms(h, g1_ref[...]).astype(BF16)
    gates = _sigmoid(_dot(hn, wg_ref[...]))
    y = jnp.concatenate([_dot(unperm_ref[...], y_ref[rows, :]) for rows, _ in _layout_tiles()], axis=0)
    z = _gelu_tanh(y)
    za = (z * _sigmoid(_dot(z.astype(BF16), wglu_ref[...]))).astype(BF16)
    yb = _tile_rows(yb_ref)
    merged = gates[:, :D_MODEL] * _dot(za, wa_ref[...]) + gates[:, D_MODEL:] * _dot(yb, wb_ref[...])
    h = h + _dot(merged.astype(BF16), wo_ref[...])

    hn = _rms(h, g2_ref[...]).astype(BF16)
    acc_ref[...] = h
    for j in range(D_FF // FF_TILE):
        cols = slice(j * FF_TILE, (j + 1) * FF_TILE)
        gate = _dot(hn, fg_ref[:, cols])
        up = _dot(hn, fu_ref[:, cols])
        act = (gate * _sigmoid(gate) * up).astype(BF16)
        acc_ref[...] += _dot(act, fd_ref[cols, :])
    out = acc_ref[...]
    if final_norm:
        out = _rms(out, gf_ref[...])
    else:
        gn_ref, wn_ref, perm_ref, u_ref, q_ref, kv_ref = nxt
        _inproj_rows(out, gn_ref, wn_ref, perm_ref, u_ref, q_ref, kv_ref)
    for rows, tok in _layout_tiles():
        o_ref[:, tok, :] = out[rows, :].reshape(bsz, LAYOUT_TOK, D_MODEL)


def _merge_ffn_last_kernel(*refs):
    *ins, o_ref, acc_ref = refs
    _merge_ffn_body(*ins, o_ref, acc_ref, nxt=None, final_norm=True)


def _merge_ffn_next_kernel(*refs):
    *ins, gn_ref, wn_ref, perm_ref, o_ref, u_ref, q_ref, kv_ref, acc_ref = refs
    _merge_ffn_body(*ins, o_ref, acc_ref, nxt=(gn_ref, wn_ref, perm_ref, u_ref, q_ref, kv_ref), final_norm=False)


def _merge_ffn(h3, y, yb3, g1, g2, gf, unperm, weights, layer, next_inproj=None):
    bsz, seq, _ = h3.shape
    vec = _const_spec((1, D_MODEL))
    in_specs = [_tok_spec(bsz, D_MODEL), pl.BlockSpec((TILE_ROWS, S5_WIDTH), lambda i: (i, 0)),
                _tok_spec(bsz, ATT_WIDTH), vec, vec, vec, _const_spec(unperm.shape)]
    in_specs += [_layer_spec(w, layer) for w in weights]
    args = [h3, y, yb3, g1, g2, gf, unperm, *weights]
    out_specs, out_shape = [_tok_spec(bsz, D_MODEL)], [jax.ShapeDtypeStruct((bsz, seq, D_MODEL), F32)]
    if next_inproj is None:
        body = _merge_ffn_last_kernel
    else:
        body = _merge_ffn_next_kernel
        gain_next, w_next, perm = next_inproj
        in_specs += [vec, _layer_spec(w_next, layer + 1), _const_spec(perm.shape)]
        args += [gain_next, w_next, perm]
        specs, shapes = _inproj_out(bsz, seq)
        out_specs += specs
        out_shape += shapes
    outs = pl.pallas_call(
        body,
        grid=(seq // TOK_TILE,),
        in_specs=in_specs,
        out_specs=out_specs,
        out_shape=out_shape,
        scratch_shapes=[pltpu.VMEM((TILE_ROWS, D_MODEL), F32)],
        compiler_params=_params(("parallel",)),
        name="merge_ffn",
    )(*args)
    return outs[0] if next_inproj is None else outs


def kernel(x, norm1_g, norm2_g, final_g, w_in, s5_lambda_re, s5_lambda_im, s5_log_dt, s5_b_re, s5_b_im,
           s5_c_re, s5_c_im, s5_d, s5_w_glu, attn_sink, rel_bias, w_branch_a, w_branch_b, w_out,
           ffn_w_gate, ffn_w_up, ffn_w_down):
    bsz, seq, _ = x.shape
    depth = w_in.shape[0]
    assert seq % BLOCK == 0 and seq % CHUNK == 0 and bsz == SUBLANES
    n_chunks = seq // CHUNK
    t = bsz * seq
    bias = _band_bias(rel_bias)
    o_k = S5_WIDTH + ATT_WIDTH
    o_g = o_k + 2 * KV_WIDTH
    col_scale = jnp.concatenate([jnp.ones((S5_WIDTH,), F32), jnp.full((ATT_WIDTH,), LOG2E * HEAD_DIM ** -0.5, F32),
                                 jnp.ones((2 * KV_WIDTH,), F32)])
    gf = final_g.reshape(1, D_MODEL).astype(F32)
    perm = _to_chunk_order()
    unperm = perm.T
    w_in_state, w_toep, w_out_state, a16 = _s5_weights(
        s5_lambda_re, s5_lambda_im, s5_log_dt, s5_b_re, s5_b_im, s5_c_re, s5_c_im, s5_d)
    gain1 = lambda layer: norm1_g[layer].reshape(1, D_MODEL).astype(F32)
    w_uqkv = (w_in[:, :, :o_g] * col_scale).astype(BF16)
    weights = [w.astype(BF16) for w in (w_in[:, :, o_g:], s5_w_glu, w_branch_a, w_branch_b, w_out,
                                        ffn_w_gate, ffn_w_up, ffn_w_down)]
    h = x
    u, q, kv = _inproj(h, gain1(0), w_uqkv, perm, 0)
    for layer in range(depth):
        g1 = gain1(layer)
        g2 = norm2_g[layer].reshape(1, D_MODEL).astype(F32)
        sin_f, sin_b, grouped = _s5_state_in(u, w_in_state, layer)
        carry_f, carry_b = _s5_scan(sin_f, sin_b, a16[layer], n_chunks)
        y = _s5_out(grouped, carry_f, carry_b, w_toep, w_out_state, layer)
        yb = _attention(q.reshape(t, ATT_WIDTH), kv.reshape(t, 2 * KV_WIDTH),
                        attn_sink[layer].astype(F32) * LOG2E, bias, bsz, seq)
        yb3 = yb.reshape(bsz, seq, ATT_WIDTH)
        if layer == depth - 1:
            h = _merge_ffn(h, y, yb3, g1, g2, gf, unperm, weights, layer)
        else:
            h, u, q, kv = _merge_ffn(h, y, yb3, g1, g2, gf, unperm, weights, layer,
                                     next_inproj=(gain1(layer + 1), w_uqkv, perm))
    return h
```
